```python
import math
import jax, jax.numpy as jnp
from jax import lax
import numpy as np

D_MODEL = 2048
BATCH = 4
SEQ = 2048
DEPTH = 4

GRID_W = 64
CTX_LEN = 256
N_MIXERS = 3
N_MOD = 6
MLP_HIDDEN = 4 * D_MODEL
NORM_EPS = 1e-6
FNET_GROUPS = 8
FNET_GROUP_DIM = D_MODEL // FNET_GROUPS
HGRN_EXPAND = 128
HGRN_HEADS = D_MODEL // HGRN_EXPAND
HGRN_DK = HGRN_EXPAND
HGRN_DV = D_MODEL // HGRN_HEADS
HGRN_F = HGRN_HEADS * HGRN_DK
HGRN_CHUNK = 16
DIFF_HEAD_DIM = 128
DIFF_HEADS = D_MODEL // (2 * DIFF_HEAD_DIM)
DIFF_V_DIM = 2 * DIFF_HEAD_DIM
Q_BLOCK = 128
ROPE_BASE = 10000.0

kernel_name = 'hybrid_fnet_hgrn2_diffattn_dit'

F32 = jnp.float32


def rms_norm(x, g):
    xf = x.astype(F32)
    y = xf * lax.rsqrt(jnp.mean(xf * xf, axis=-1, keepdims=True) + NORM_EPS)
    return (y * g.astype(F32)).astype(x.dtype)


def modulate(h, shift, scale):
    return h * (1 + scale) + shift


def sqrelu_mlp(h, w1, w2):
    return jnp.square(jax.nn.relu(h @ w1)) @ w2


def axial_rope_tables(n_tokens, dim):
    rows = n_tokens // GRID_W
    row = jnp.broadcast_to(jnp.arange(rows)[:, None], (rows, GRID_W)).reshape(-1).astype(F32)
    col = jnp.broadcast_to(jnp.arange(GRID_W)[None, :], (rows, GRID_W)).reshape(-1).astype(F32)
    n_freq = dim // 4
    inv = ROPE_BASE ** (-jnp.arange(n_freq, dtype=F32) / n_freq)
    ang = jnp.concatenate([row[:, None] * inv, col[:, None] * inv], axis=-1)
    return jnp.cos(ang), jnp.sin(ang)


def apply_rope(x, cos, sin):
    xf = x.astype(F32)
    x1, x2 = jnp.split(xf, 2, axis=-1)
    return jnp.concatenate([x1 * cos - x2 * sin, x1 * sin + x2 * cos], axis=-1).astype(x.dtype)


def fourier_mix(h):
    B, L, D = h.shape
    hg = h.astype(F32).reshape(B, L, FNET_GROUPS, FNET_GROUP_DIM)
    return jnp.fft.fftn(hg, axes=(1, 3)).real.reshape(B, L, D).astype(h.dtype)


def hgrn_lower_bounds(p):
    cs = jnp.cumsum(jax.nn.softmax(p.astype(F32), axis=0), axis=0)
    return cs - cs[0:1]


def chunk_gla(q, k, v, logf, s0):
    B, H, L, dk = q.shape
    C = HGRN_CHUNK
    N = L // C
    r = lambda t: jnp.moveaxis(t.astype(F32).reshape(B, H, N, C, t.shape[-1]), 2, 0)
    bcum = jnp.cumsum(logf.astype(F32).reshape(B, H, N, C, dk), axis=3)
    xs = (r(q), r(k), r(v), jnp.moveaxis(bcum, 2, 0))
    mask = jnp.tril(jnp.ones((C, C), dtype=bool))[:, :, None]

    def step(s, inp):
        qc, kc, vc, bc = inp
        bl = bc[:, :, -1:, :]
        decay = jnp.exp(jnp.where(mask, bc[:, :, :, None, :] - bc[:, :, None, :, :], -jnp.inf))
        a = jnp.einsum('bhtd,bhsd,bhtsd->bhts', qc, kc, decay)
        o = jnp.einsum('bhts,bhsv->bhtv', a, vc) + jnp.einsum('bhtd,bhdv->bhtv', qc * jnp.exp(bc), s)
        s = jnp.exp(bl[:, :, 0, :])[..., None] * s + jnp.einsum('bhsd,bhsv->bhdv', kc * jnp.exp(bl - bc), vc)
        return s, o

    s_final, o = lax.scan(step, s0.astype(F32), xs)
    return jnp.moveaxis(o, 0, 2).reshape(B, H, L, v.shape[-1]), s_final


def hgrn2_mixer(h_lat, h_ctx, w_in, lb, gnorm, w_out, ctx_out):
    lb = lb.reshape(2, HGRN_HEADS, HGRN_DK)

    def project(h):
        B, L, _ = h.shape
        q, zf, zb, v, g = jnp.split(h @ w_in, [HGRN_F, 2 * HGRN_F, 3 * HGRN_F, 3 * HGRN_F + D_MODEL], axis=-1)
        hd = lambda t: t.reshape(B, L, HGRN_HEADS, -1).transpose(0, 2, 1, 3)
        return hd(q), hd(zf), hd(zb), hd(v), hd(g)

    def forget(z, lbd):
        zf = z.astype(F32)
        l = lbd[None, :, None, :]
        logf = jnp.logaddexp(jnp.log(l), jnp.log1p(-l) + jax.nn.log_sigmoid(zf))
        key = (1 - l) * jax.nn.sigmoid(-zf)
        return logf, key

    def readout(o, g, dtype):
        B, H, L, _ = o.shape
        y = rms_norm(o, gnorm) * jax.nn.silu(g.astype(F32))
        return y.transpose(0, 2, 1, 3).reshape(B, L, H * HGRN_DV).astype(dtype) @ w_out

    flip = lambda t: jnp.flip(t, axis=2)
    qc, zfc, zbc, vc, gc = project(h_ctx)
    ql, zfl, zbl, vl, gl = project(h_lat)
    s0 = jnp.zeros((qc.shape[0], HGRN_HEADS, HGRN_DK, HGRN_DV), F32)
    lfc, kfc = forget(zfc, lb[0])
    o_cf, s_f = chunk_gla(qc, kfc, vc, lfc, s0)
    lfl, kfl = forget(zfl, lb[0])
    o_lf, _ = chunk_gla(ql, kfl, vl, lfl, s_f)
    lbc, kbc = forget(zbc, lb[1])
    o_cb, s_b = chunk_gla(flip(qc), flip(kbc), flip(vc), flip(lbc), s0)
    lbl, kbl = forget(zbl, lb[1])
    o_lb, _ = chunk_gla(flip(ql), flip(kbl), flip(vl), flip(lbl), s_b)
    y_lat = readout(o_lf + flip(o_lb), gl, h_lat.dtype)
    y_ctx = readout(o_cf + flip(o_cb), gc, h_ctx.dtype) if ctx_out else None
    return y_lat, y_ctx


def diff_core(q, k, v, lam):
    s = jnp.einsum('bhmqd,bhmkd->bhmqk', q, k).astype(F32) * (DIFF_HEAD_DIM ** -0.5)
    p = jax.nn.softmax(s, axis=-1)
    a = p[:, :, 0] - lam * p[:, :, 1]
    return jnp.einsum('bhqk,bhkv->bhqv', a.astype(v.dtype), v)


def diff_attention(h_lat, h_ctx, w_qkv, lam_p, subln, w_out, lambda_init, cos, sin, ctx_out):
    def project(h):
        B, L, _ = h.shape
        q, k, v = jnp.split(h @ w_qkv, 3, axis=-1)
        q = q.reshape(B, L, DIFF_HEADS, 2, DIFF_HEAD_DIM).transpose(0, 2, 3, 1, 4)
        k = k.reshape(B, L, DIFF_HEADS, 2, DIFF_HEAD_DIM).transpose(0, 2, 3, 1, 4)
        v = v.reshape(B, L, DIFF_HEADS, DIFF_V_DIM).transpose(0, 2, 1, 3)
        return q, k, v

    def merge(o, dtype):
        B, H, L, _ = o.shape
        o = rms_norm(o, subln) * (1 - lambda_init)
        return o.transpose(0, 2, 1, 3).reshape(B, L, H * DIFF_V_DIM).astype(dtype) @ w_out

    lp = lam_p.astype(F32)
    lam = jnp.exp(jnp.sum(lp[0] * lp[1])) - jnp.exp(jnp.sum(lp[2] * lp[3])) + lambda_init
    q_c, k_c, v_c = project(h_ctx)
    q_l, k_l, v_l = project(h_lat)
    q_l, k_l = apply_rope(q_l, cos, sin), apply_rope(k_l, cos, sin)
    k_all = jnp.concatenate([k_c, k_l], axis=3)
    v_all = jnp.concatenate([v_c, v_l], axis=2)
    B, H, _, L, d = q_l.shape
    nb = L // Q_BLOCK
    q_blocks = jnp.moveaxis(q_l.reshape(B, H, 2, nb, Q_BLOCK, d), 3, 0)
    o_blocks = lax.map(lambda qb: diff_core(qb, k_all, v_all, lam), q_blocks)
    o_l = jnp.moveaxis(o_blocks, 0, 2).reshape(B, H, L, DIFF_V_DIM)
    y_lat = merge(o_l, h_lat.dtype)
    y_ctx = merge(diff_core(q_c, k_c, v_c, lam), h_ctx.dtype) if ctx_out else None
    return y_lat, y_ctx


def setup_inputs(seed: int = 0) -> dict:
    key = jax.random.key(seed)
    ks = jax.random.split(key, 20)
    n_slot = lambda m: len(range(m, DEPTH, N_MIXERS))
    n_a, n_b, n_c = n_slot(0), n_slot(1), n_slot(2)
    D = D_MODEL
    nrm = lambda k, shape, s: jax.random.normal(k, shape, F32) * s
    return {
        'x': nrm(ks[0], (BATCH, SEQ, D), 1.0),
        'c': nrm(ks[1], (BATCH, D), 1.0),
        'ctx': nrm(ks[2], (BATCH, CTX_LEN, D), 1.0),
        'c_ctx': nrm(ks[3], (D,), 1.0),
        'w_mod': nrm(ks[4], (DEPTH, D, N_MOD * D), D ** -0.5),
        'b_mod': nrm(ks[5], (DEPTH, N_MOD * D), 0.01),
        'norm_g': 1.0 + nrm(ks[6], (DEPTH, 4, D), 0.02),
        'w_mlp_in': nrm(ks[7], (DEPTH, D, MLP_HIDDEN), D ** -0.5),
        'w_mlp_out': nrm(ks[8], (DEPTH, MLP_HIDDEN, D), MLP_HIDDEN ** -0.5),
        'fnet_w_out': nrm(ks[9], (n_a, D, D), D ** -0.5),
        'hgrn_w_in': nrm(ks[10], (n_b, D, 3 * HGRN_F + 2 * D), D ** -0.5),
        'hgrn_lb': nrm(ks[11], (DEPTH, 2, HGRN_F), 0.1),
        'hgrn_gnorm': 1.0 + nrm(ks[12], (n_b, HGRN_DV), 0.02),
        'hgrn_w_out': nrm(ks[13], (n_b, D, D), D ** -0.5),
        'diff_w_qkv': nrm(ks[14], (n_c, D, 3 * D), D ** -0.5),
        'diff_lambda': nrm(ks[15], (n_c, 4, DIFF_HEAD_DIM), 0.1),
        'diff_subln': 1.0 + nrm(ks[16], (n_c, DIFF_V_DIM), 0.02),
        'diff_w_out': nrm(ks[17], (n_c, D, D), D ** -0.5),
    }


def reference(x, c, ctx, c_ctx, w_mod, b_mod, norm_g, w_mlp_in, w_mlp_out, fnet_w_out, hgrn_w_in, hgrn_lb, hgrn_gnorm, hgrn_w_out, diff_w_qkv, diff_lambda, diff_subln, diff_w_out):
    L = x.shape[1]
    cos, sin = axial_rope_tables(L, DIFF_HEAD_DIM)
    lower_bounds = hgrn_lower_bounds(hgrn_lb)
    silu_c = jax.nn.silu(c)
    silu_cc = jax.nn.silu(c_ctx)
    for i in range(DEPTH):
        mixer, slot = i % N_MIXERS, i // N_MIXERS
        ctx_out = i < DEPTH - 1
        ctx_in = ctx_out or mixer != 0
        sh1, sc1, g1, sh2, sc2, g2 = jnp.split((silu_c @ w_mod[i] + b_mod[i])[:, None, :], N_MOD, axis=-1)
        h_lat = modulate(rms_norm(x, norm_g[i, 0]), sh1, sc1)
        h_ctx = None
        if ctx_in:
            csh1, csc1, cg1, csh2, csc2, cg2 = jnp.split(silu_cc @ w_mod[i] + b_mod[i], N_MOD, axis=-1)
            h_ctx = modulate(rms_norm(ctx, norm_g[i, 0]), csh1, csc1)
        if mixer == 0:
            y_lat = fourier_mix(h_lat) @ fnet_w_out[slot]
            y_ctx = fourier_mix(h_ctx) @ fnet_w_out[slot] if ctx_out else None
        elif mixer == 1:
            y_lat, y_ctx = hgrn2_mixer(h_lat, h_ctx, hgrn_w_in[slot], lower_bounds[i], hgrn_gnorm[slot], hgrn_w_out[slot], ctx_out)
        else:
            lambda_init = 0.8 - 0.6 * math.exp(-0.3 * i)
            y_lat, y_ctx = diff_attention(h_lat, h_ctx, diff_w_qkv[slot], diff_lambda[slot], diff_subln[slot], diff_w_out[slot], lambda_init, cos, sin, ctx_out)
        x = x + g1 * rms_norm(y_lat, norm_g[i, 1])
        x = x + g2 * rms_norm(sqrelu_mlp(modulate(rms_norm(x, norm_g[i, 2]), sh2, sc2), w_mlp_in[i], w_mlp_out[i]), norm_g[i, 3])
        if ctx_out:
            ctx = ctx + cg1 * rms_norm(y_ctx, norm_g[i, 1])
            ctx = ctx + cg2 * rms_norm(sqrelu_mlp(modulate(rms_norm(ctx, norm_g[i, 2]), csh2, csc2), w_mlp_in[i], w_mlp_out[i]), norm_g[i, 3])
    return x
```

```python
import functools
import math

import jax
import jax.numpy as jnp
from jax import lax
from jax.experimental import pallas as pl
from jax.experimental.pallas import tpu as pltpu

F32 = jnp.float32
BF16 = jnp.bfloat16

NORM_EPS = 1e-6
ROPE_BASE = 10000.0
GRID_W = 64
N_MOD = 6
N_MIXERS = 3
FNET_GROUPS = 8
HGRN_DK = 128
DIFF_HEAD_DIM = 128
MOD_ROWS = 8

GLA_SUB = 16
GLA_SEG = 256
V7X_VMEM_LIMIT = 56 * 1024 * 1024


def _params(sem, vmem=V7X_VMEM_LIMIT):
    return pltpu.CompilerParams(dimension_semantics=sem, vmem_limit_bytes=vmem)


def _silu(v):
    return v * jax.nn.sigmoid(v)


def _rms(y, gamma):
    return y * lax.rsqrt(jnp.mean(y * y, axis=-1, keepdims=True) + NORM_EPS) * gamma


def _mod_row(ref, tile, tiles_per_batch, n_batch):
    r = jnp.minimum(tile // tiles_per_batch, n_batch)
    return ref[pl.ds(r, 1), :]


def _mods_kernel(c_ref, w_ref, b_ref, o_ref):
    a = _silu(c_ref[...]).astype(BF16)
    o_ref[...] = jnp.dot(a, w_ref[...].astype(BF16), preferred_element_type=F32) + b_ref[...]


def _mods(cc, w_mod, b_mod):
    depth, d, n = w_mod.shape
    tn = 1024
    return pl.pallas_call(
        _mods_kernel,
        grid=(depth, n // tn),
        in_specs=[
            pl.BlockSpec((MOD_ROWS, d), lambda l, j: (0, 0)),
            pl.BlockSpec((None, d, tn), lambda l, j: (l, 0, j)),
            pl.BlockSpec((None, 1, tn), lambda l, j: (l, 0, j)),
        ],
        out_specs=pl.BlockSpec((None, MOD_ROWS, tn), lambda l, j: (l, 0, j)),
        out_shape=jax.ShapeDtypeStruct((depth, MOD_ROWS, n), F32),
        compiler_params=_params(("parallel", "parallel")),
        name="mods",
    )(cc, w_mod, b_mod.reshape(depth, 1, n))


def _norm_mod_kernel(x_ref, g_ref, sh_ref, sc_ref, *rest, tpb, nb, grow, groups):
    i = pl.program_id(0)
    sh = _mod_row(sh_ref, i, tpb, nb)
    sc = _mod_row(sc_ref, i, tpb, nb)
    h = _rms(x_ref[...], g_ref[grow:grow + 1, :]) * (1.0 + sc) + sh
    hb = h.astype(BF16)
    if groups == 0:
        (o_ref,) = rest
        o_ref[...] = hb
    else:
        wcs_ref, a_ref, b_ref = rest
        gd = hb.shape[1] // groups
        for g in range(groups):
            r = jnp.dot(hb[:, g * gd:(g + 1) * gd], wcs_ref[...], preferred_element_type=F32)
            a_ref[:, g * gd:(g + 1) * gd] = r[:, :gd].astype(BF16)
            b_ref[:, g * gd:(g + 1) * gd] = r[:, gd:].astype(BF16)


def _norm_mod(x, norm_g, mods, layer, grow, col, nrows, seq, nb, wcs=None):
    d = x.shape[1]
    tm = 256
    tpb = seq // tm
    groups = 0 if wcs is None else FNET_GROUPS
    in_specs = [
        pl.BlockSpec((tm, d), lambda i: (i, 0)),
        pl.BlockSpec((None, 4, d), lambda i: (layer, 0, 0)),
        pl.BlockSpec((None, MOD_ROWS, d), lambda i: (layer, 0, col)),
        pl.BlockSpec((None, MOD_ROWS, d), lambda i: (layer, 0, col + 1)),
    ]
    args = [x, norm_g, mods, mods]
    out_spec = pl.BlockSpec((tm, d), lambda i: (i, 0))
    out_shape = jax.ShapeDtypeStruct((nrows, d), BF16)
    if wcs is not None:
        in_specs.append(pl.BlockSpec(wcs.shape, lambda i: (0, 0)))
        args.append(wcs)
        out_spec = [out_spec, out_spec]
        out_shape = [out_shape, out_shape]
    return pl.pallas_call(
        functools.partial(_norm_mod_kernel, tpb=tpb, nb=nb, grow=grow, groups=groups),
        grid=(nrows // tm,),
        in_specs=in_specs,
        out_specs=out_spec,
        out_shape=out_shape,
        compiler_params=_params(("parallel",)),
        name="norm_mod",
    )(*args)


def _mm_kernel(a_ref, w_ref, *rest, rope_tiles, q_tiles, q_scale):
    if rope_tiles:
        cos_ref, sin_ref, o_ref, wb_ref = rest
    else:
        o_ref, wb_ref = rest
    j = pl.program_id(0)

    @pl.when(pl.program_id(1) == 0)
    def _():
        wb_ref[...] = w_ref[...].astype(BF16)

    acc = jnp.dot(a_ref[...], wb_ref[...], preferred_element_type=F32)
    if not rope_tiles:
        o_ref[...] = acc.astype(o_ref.dtype)
        return

    @pl.when(j < rope_tiles)
    def _():
        cos = cos_ref[...]
        sin = sin_ref[...]
        scale = jnp.where(j < q_tiles, q_scale, 1.0).astype(F32)
        hd = cos.shape[1]
        for c in range(acc.shape[1] // hd):
            blk = acc[:, c * hd:(c + 1) * hd]
            rot = pltpu.roll(blk, hd // 2, axis=1)
            o_ref[:, c * hd:(c + 1) * hd] = ((blk * cos + rot * sin) * scale).astype(o_ref.dtype)

    @pl.when(j >= rope_tiles)
    def _():
        o_ref[...] = acc.astype(o_ref.dtype)


def _mm(a, w, col0, ncols, out_dtype, rope=None, name="mm"):
    m, k = a.shape
    tm = min(1024, m)
    tn = 512
    assert m % tm == 0 and ncols % tn == 0 and col0 % tn == 0
    off = col0 // tn
    in_specs = [
        pl.BlockSpec((tm, k), lambda j, i: (i, 0)),
        pl.BlockSpec((k, tn), lambda j, i: (0, j + off)),
    ]
    args = [a, w]
    rope_tiles = q_tiles = 0
    q_scale = 1.0
    if rope is not None:
        cos2, sin2, rope_cols, q_cols, q_scale = rope
        hd = cos2.shape[1]
        in_specs += [pl.BlockSpec((tm, hd), lambda j, i: (i, 0))] * 2
        args += [cos2, sin2]
        rope_tiles, q_tiles = rope_cols // tn, q_cols // tn
    return pl.pallas_call(
        functools.partial(_mm_kernel, rope_tiles=rope_tiles, q_tiles=q_tiles, q_scale=q_scale),
        grid=(ncols // tn, m // tm),
        in_specs=in_specs,
        out_specs=pl.BlockSpec((tm, tn), lambda j, i: (i, j)),
        out_shape=jax.ShapeDtypeStruct((m, ncols), out_dtype),
        scratch_shapes=[pltpu.VMEM((k, tn), BF16)],
        compiler_params=_params(("parallel", "arbitrary")),
        name=name,
    )(*args)


def _mm_res_kernel(a_ref, w_ref, x_ref, g_ref, gate_ref, o_ref, acc_ref, *, tpb, nb, grow):
    i = pl.program_id(0)
    k = pl.program_id(1)
    part = jnp.dot(a_ref[...], w_ref[...].astype(BF16), preferred_element_type=F32)

    @pl.when(k == 0)
    def _():
        acc_ref[...] = part

    @pl.when(k > 0)
    def _():
        acc_ref[...] += part

    @pl.when(k == pl.num_programs(1) - 1)
    def _():
        gate = _mod_row(gate_ref, i, tpb, nb)
        o_ref[...] = x_ref[...] + gate * _rms(acc_ref[...], g_ref[grow:grow + 1, :])


def _mm_res(a, w, x, norm_g, mods, layer, grow, gate_col, nrows, seq, nb):
    kdim, d = w.shape
    tm = 512
    tk = 512
    tpb = seq // tm
    return pl.pallas_call(
        functools.partial(_mm_res_kernel, tpb=tpb, nb=nb, grow=grow),
        grid=(nrows // tm, kdim // tk),
        in_specs=[
            pl.BlockSpec((tm, tk), lambda i, k: (i, k)),
            pl.BlockSpec((tk, d), lambda i, k: (k, 0)),
            pl.BlockSpec((tm, d), lambda i, k: (i, 0)),
            pl.BlockSpec((None, 4, d), lambda i, k: (layer, 0, 0)),
            pl.BlockSpec((None, MOD_ROWS, d), lambda i, k: (layer, 0, gate_col)),
        ],
        out_specs=pl.BlockSpec((tm, d), lambda i, k: (i, 0)),
        out_shape=jax.ShapeDtypeStruct((nrows, d), F32),
        scratch_shapes=[pltpu.VMEM((tm, d), F32)],
        compiler_params=_params(("parallel", "arbitrary")),
        name="mm_res",
    )(a, w, x, norm_g, mods)


def _mlp_kernel(x_ref, g_ref, sh_ref, sc_ref, gate_ref, w1_ref, w2_ref, o_ref, h_ref, acc_ref,
                *, tpb, nb):
    i = pl.program_id(0)
    k = pl.program_id(1)

    @pl.when(k == 0)
    def _():
        sh = _mod_row(sh_ref, i, tpb, nb)
        sc = _mod_row(sc_ref, i, tpb, nb)
        h = _rms(x_ref[...], g_ref[2:3, :]) * (1.0 + sc) + sh
        h_ref[...] = h.astype(BF16)

    u = jnp.dot(h_ref[...], w1_ref[...].astype(BF16), preferred_element_type=F32)
    u = jnp.square(jnp.maximum(u, 0.0)).astype(BF16)
    part = jnp.dot(u, w2_ref[...].astype(BF16), preferred_element_type=F32)

    @pl.when(k == 0)
    def _():
        acc_ref[...] = part

    @pl.when(k > 0)
    def _():
        acc_ref[...] += part

    @pl.when(k == pl.num_programs(1) - 1)
    def _():
        gate = _mod_row(gate_ref, i, tpb, nb)
        o_ref[...] = x_ref[...] + gate * _rms(acc_ref[...], g_ref[3:4, :])


def _mlp(x, w1, w2, norm_g, mods, layer, nrows, seq, nb):
    d, hid = w1.shape[1], w1.shape[2]
    tm = 512
    th = 512
    tpb = seq // tm
    mod_spec = lambda col: pl.BlockSpec((None, MOD_ROWS, d), lambda i, k: (layer, 0, col))
    return pl.pallas_call(
        functools.partial(_mlp_kernel, tpb=tpb, nb=nb),
        grid=(nrows // tm, hid // th),
        in_specs=[
            pl.BlockSpec((tm, d), lambda i, k: (i, 0)),
            pl.BlockSpec((None, 4, d), lambda i, k: (layer, 0, 0)),
            mod_spec(3), mod_spec(4), mod_spec(5),
            pl.BlockSpec((None, d, th), lambda i, k: (layer, 0, k)),
            pl.BlockSpec((None, th, d), lambda i, k: (layer, k, 0)),
        ],
        out_specs=pl.BlockSpec((tm, d), lambda i, k: (i, 0)),
        out_shape=jax.ShapeDtypeStruct((nrows, d), F32),
        scratch_shapes=[pltpu.VMEM((tm, d), BF16), pltpu.VMEM((tm, d), F32)],
        compiler_params=_params(("parallel", "arbitrary")),
        name="mlp",
    )(x, norm_g, mods, mods, mods, w1, w2)


def _dft_tables(n):
    idx = jnp.arange(n, dtype=jnp.int32)
    ang = ((idx[:, None] * idx[None, :]) % n).astype(F32) * (2.0 * math.pi / n)
    return jnp.cos(ang), jnp.sin(ang)


def _posdft_kernel(c_ref, s_ref, a_ref, b_ref, *rest):
    o_ref = rest[-1]
    acc = jnp.dot(c_ref[...], a_ref[...], preferred_element_type=F32)
    acc += jnp.dot(s_ref[...], b_ref[...], preferred_element_type=F32)
    o_ref[...] = acc.astype(o_ref.dtype)


def _posdft(a, bn, nbatch, seq, row0, out_rows, prev=None):
    d = a.shape[1]
    cos, sin = _dft_tables(seq)
    cos, sin = cos.astype(BF16), sin.astype(BF16)
    tm = min(512, seq)
    tn = 512
    assert row0 % seq == 0
    rb0 = row0 // seq
    mb0 = row0 // tm
    in_specs = [
        pl.BlockSpec((tm, seq), lambda b, j, m: (m, 0)),
        pl.BlockSpec((tm, seq), lambda b, j, m: (m, 0)),
        pl.BlockSpec((seq, tn), lambda b, j, m: (rb0 + b, j)),
        pl.BlockSpec((seq, tn), lambda b, j, m: (rb0 + b, j)),
    ]
    args = [cos, sin, a, bn]
    aliases = {}
    if prev is not None:
        in_specs.append(pl.BlockSpec(memory_space=pl.ANY))
        args.append(prev)
        aliases = {4: 0}
    return pl.pallas_call(
        _posdft_kernel,
        grid=(nbatch, d // tn, seq // tm),
        in_specs=in_specs,
        out_specs=pl.BlockSpec((tm, tn), lambda b, j, m: (mb0 + b * (seq // tm) + m, j)),
        out_shape=jax.ShapeDtypeStruct((out_rows, d), BF16),
        input_output_aliases=aliases,
        compiler_params=_params(("parallel", "parallel", "arbitrary")),
        name="posdft",
    )(*args)


def _split3_dot(mask_bf16, v):
    hi = v.astype(BF16)
    r1 = v - hi.astype(F32)
    mid = r1.astype(BF16)
    lo = (r1 - mid.astype(F32)).astype(BF16)
    out = jnp.dot(mask_bf16, hi, preferred_element_type=F32)
    out += jnp.dot(mask_bf16, mid, preferred_element_type=F32)
    out += jnp.dot(mask_bf16, lo, preferred_element_type=F32)
    return out


def _gla_kernel(q_ref, z_ref, v_ref, g_ref, lbp_ref, gn_ref, y_ref,
                of_ref, st_ref, qd_ref, kd_ref, bc_ref, kk_ref, o_ref, *, layer, nseg):
    t = pl.program_id(2)
    fwd = t < nseg
    seg = t % nseg
    seg_rows = q_ref.shape[0]
    nsub = seg_rows // GLA_SUB

    @pl.when(seg == 0)
    def _():
        st_ref[...] = jnp.zeros_like(st_ref)

    p = lbp_ref[...]
    e = jnp.exp(p - jnp.max(p, axis=0, keepdims=True))
    sm = e / jnp.sum(e, axis=0, keepdims=True)
    lb2 = jnp.zeros_like(sm[0])
    for j in range(1, layer + 1):
        lb2 = lb2 + sm[j]
    lb = jnp.where(fwd, lb2[0:1, :], lb2[1:2, :])

    z = z_ref[...]
    ez = jnp.exp(-jnp.abs(z))
    r = 1.0 / (1.0 + ez)
    pos = z >= 0.0
    sig = jnp.where(pos, r, ez * r)
    nsig = jnp.where(pos, ez * r, r)
    logf = jnp.log(lb + (1.0 - lb) * sig)
    kk = (1.0 - lb) * nsig
    kk_ref[...] = kk

    half = 128
    ri = lax.broadcasted_iota(jnp.int32, (half, half), 0)
    ci = lax.broadcasted_iota(jnp.int32, (half, half), 1)
    same = (ri // GLA_SUB) == (ci // GLA_SUB)
    sgn = jnp.where(fwd, 1, -1)
    ahead = (ri - ci) * sgn
    incl = same & (ahead >= 0)
    excl = same & (ahead < 0)
    m_incl = jnp.where(incl, 1.0, 0.0).astype(BF16)
    m_excl = jnp.where(excl, 1.0, 0.0).astype(BF16)
    q = q_ref[...].astype(F32)
    for hseg in range(seg_rows // half):
        rows = slice(hseg * half, (hseg + 1) * half)
        lf = logf[rows]
        bc = _split3_dot(m_incl, lf)
        rest = _split3_dot(m_excl, lf)
        bc_ref[rows, :] = bc
        qd_ref[rows, :] = (q[rows] * jnp.exp(bc)).astype(BF16)
        kd_ref[rows, :] = (kk[rows] * jnp.exp(rest)).astype(BF16)

    rowi = lax.broadcasted_iota(jnp.int32, (GLA_SUB, HGRN_DK), 0)

    def step(j, st):
        i = jnp.where(fwd, j, nsub - 1 - j)
        r0 = pl.multiple_of(i * GLA_SUB, GLA_SUB)
        rows = pl.ds(r0, GLA_SUB)
        bc = bc_ref[rows, :]
        qs = q_ref[rows, :].astype(F32)
        ks = kk_ref[rows, :]
        vs = v_ref[rows, :].astype(F32)
        o = lax.dot_general(qd_ref[rows, :], st.astype(BF16), (((1,), (1,)), ((), ())),
                            preferred_element_type=F32)
        for s in range(GLA_SUB):
            keep = (rowi - s) * sgn >= 0
            dec = jnp.exp(jnp.where(keep, bc - bc[s:s + 1, :], -1e30))
            col = jnp.sum(qs * dec * ks[s:s + 1, :], axis=1, keepdims=True)
            o = o + col * vs[s:s + 1, :]
        o_ref[rows, :] = o
        btot = jnp.where(fwd, bc[GLA_SUB - 1:GLA_SUB, :], bc[0:1, :])
        upd = lax.dot_general(vs.astype(BF16), kd_ref[rows, :], (((0,), (0,)), ((), ())),
                              preferred_element_type=F32)
        return st * jnp.exp(btot) + upd

    st_ref[...] = lax.fori_loop(0, nsub, step, st_ref[...])

    @pl.when(fwd)
    def _():
        of_ref[pl.ds(pl.multiple_of(seg * seg_rows, seg_rows), seg_rows), :] = o_ref[...]

    @pl.when(jnp.logical_not(fwd))
    def _():
        fseg = jnp.where(seg == 0, 0, nseg - seg)
        o = o_ref[...] + of_ref[pl.ds(pl.multiple_of(fseg * seg_rows, seg_rows), seg_rows), :]
        y = _rms(o, gn_ref[...]) * _silu(g_ref[...].astype(F32))
        y_ref[...] = y.astype(y_ref.dtype)


def _gla(pq, pz, pvg, hgrn_lb, gnorm, layer, nb, seq, ctx_len):
    t_rows, f = pq.shape
    heads = f // HGRN_DK
    sr = GLA_SEG
    assert ctx_len == sr and seq % sr == 0
    nlat = seq // sr
    nseg = nlat + 1
    lat_blocks = nb * nlat
    depth = hgrn_lb.shape[0]

    def rb(b, t):
        seg = t % nseg
        lat = b * nlat + jnp.where(t < nseg, seg - 1, nlat - seg)
        return jnp.where(seg == 0, lat_blocks + b, lat)

    blk = lambda colf: pl.BlockSpec((sr, HGRN_DK), lambda b, h, t: (rb(b, t), colf(h, t)))
    return pl.pallas_call(
        functools.partial(_gla_kernel, layer=layer, nseg=nseg),
        grid=(nb, heads, 2 * nseg),
        in_specs=[
            blk(lambda h, t: h),
            blk(lambda h, t: jnp.where(t < nseg, 0, heads) + h),
            blk(lambda h, t: h),
            blk(lambda h, t: heads + h),
            pl.BlockSpec((depth, 2, HGRN_DK), lambda b, h, t: (0, 0, h)),
            pl.BlockSpec((1, HGRN_DK), lambda b, h, t: (0, 0)),
        ],
        out_specs=pl.BlockSpec((sr, HGRN_DK), lambda b, h, t: (rb(b, jnp.maximum(t, nseg)), h)),
        out_shape=jax.ShapeDtypeStruct((t_rows, f), BF16),
        scratch_shapes=[
            pltpu.VMEM((nseg * sr, HGRN_DK), F32),
            pltpu.VMEM((HGRN_DK, HGRN_DK), F32),
            pltpu.VMEM((sr, HGRN_DK), BF16),
            pltpu.VMEM((sr, HGRN_DK), BF16),
            pltpu.VMEM((sr, HGRN_DK), F32),
            pltpu.VMEM((sr, HGRN_DK), F32),
            pltpu.VMEM((sr, HGRN_DK), F32),
        ],
        compiler_params=_params(("parallel", "parallel", "arbitrary")),
        name="gla",
    )(pq, pz, pvg, pvg, hgrn_lb, gnorm.reshape(1, HGRN_DK))


def _dot_nt(a, b):
    return lax.dot_general(a, b, (((1,), (1,)), ((), ())), preferred_element_type=F32)


def _attn_kernel(q1_ref, q2_ref, k1l_ref, k2l_ref, k1c_ref, k2c_ref, vl_ref, vc_ref,
                 lam_ref, sub_ref, o_ref, *, lambda_init, nq):
    qi = pl.program_id(2)
    lp = lam_ref[...]
    lam = (jnp.exp(jnp.sum(lp[0:1, :] * lp[1:2, :], keepdims=True))
           - jnp.exp(jnp.sum(lp[2:3, :] * lp[3:4, :], keepdims=True)) + lambda_init)

    def probs(q_ref, kc_ref, kl_ref):
        sc = _dot_nt(q_ref[...], kc_ref[...])
        m = jnp.max(sc, axis=-1, keepdims=True)
        if kl_ref is not None:
            sl = _dot_nt(q_ref[...], kl_ref[...])
            m = jnp.maximum(m, jnp.max(sl, axis=-1, keepdims=True))
        pc = jnp.exp(sc - m)
        den = jnp.sum(pc, axis=-1, keepdims=True)
        pl_ = None
        if kl_ref is not None:
            pl_ = jnp.exp(sl - m)
            den = den + jnp.sum(pl_, axis=-1, keepdims=True)
        return pc, pl_, 1.0 / den

    def finish(o):
        y = _rms(o, sub_ref[...]) * (1.0 - lambda_init)
        o_ref[...] = y.astype(o_ref.dtype)

    @pl.when(qi < nq)
    def _():
        p1c, p1l, i1 = probs(q1_ref, k1c_ref, k1l_ref)
        p2c, p2l, i2 = probs(q2_ref, k2c_ref, k2l_ref)
        w2 = lam * i2
        ac = (p1c * i1 - p2c * w2).astype(BF16)
        al = (p1l * i1 - p2l * w2).astype(BF16)
        o = jnp.dot(ac, vc_ref[...], preferred_element_type=F32)
        o += jnp.dot(al, vl_ref[...], preferred_element_type=F32)
        finish(o)

    @pl.when(qi == nq)
    def _():
        p1c, _, i1 = probs(q1_ref, k1c_ref, None)
        p2c, _, i2 = probs(q2_ref, k2c_ref, None)
        ac = (p1c * i1 - p2c * (lam * i2)).astype(BF16)
        finish(jnp.dot(ac, vc_ref[...], preferred_element_type=F32))


def _attn(qkv, lam_p, subln, lambda_init, nb, seq, ctx_len):
    t_rows, d3 = qkv.shape
    d = d3 // 3
    hd = DIFF_HEAD_DIM
    heads = d // (2 * hd)
    tq = ctx_len
    nq = seq // tq
    lat_qblocks = nb * nq
    kcol = d // hd
    vcol = 2 * d // (2 * hd)
    qrow = lambda b, qi: jnp.where(qi == nq, lat_qblocks + b, b * nq + qi)
    q_spec = lambda m: pl.BlockSpec((tq, hd), lambda b, h, qi: (qrow(b, qi), 2 * h + m))
    kl_spec = lambda m: pl.BlockSpec((seq, hd), lambda b, h, qi: (b, kcol + 2 * h + m))
    kc_spec = lambda m: pl.BlockSpec((ctx_len, hd), lambda b, h, qi: (lat_qblocks + b, kcol + 2 * h + m))
    return pl.pallas_call(
        functools.partial(_attn_kernel, lambda_init=lambda_init, nq=nq),
        grid=(nb, heads, nq + 1),
        in_specs=[
            q_spec(0), q_spec(1), kl_spec(0), kl_spec(1), kc_spec(0), kc_spec(1),
            pl.BlockSpec((seq, 2 * hd), lambda b, h, qi: (b, vcol + h)),
            pl.BlockSpec((ctx_len, 2 * hd), lambda b, h, qi: (lat_qblocks + b, vcol + h)),
            pl.BlockSpec((4, hd), lambda b, h, qi: (0, 0)),
            pl.BlockSpec((1, 2 * hd), lambda b, h, qi: (0, 0)),
        ],
        out_specs=pl.BlockSpec((tq, 2 * hd), lambda b, h, qi: (qrow(b, qi), h)),
        out_shape=jax.ShapeDtypeStruct((t_rows, d), BF16),
        compiler_params=_params(("parallel", "parallel", "arbitrary")),
        name="diff_attn",
    )(qkv, qkv, qkv, qkv, qkv, qkv, qkv, qkv, lam_p, subln.reshape(1, 2 * hd))


def _rope_tables(seq, nb, ctx_rows):
    rows = seq // GRID_W
    row = jnp.repeat(jnp.arange(rows, dtype=F32), GRID_W)
    col = jnp.tile(jnp.arange(GRID_W, dtype=F32), rows)
    n_freq = DIFF_HEAD_DIM // 4
    inv = ROPE_BASE ** (-jnp.arange(n_freq, dtype=F32) / n_freq)
    ang = jnp.concatenate([row[:, None] * inv, col[:, None] * inv], axis=-1)
    cos, sin = jnp.cos(ang), jnp.sin(ang)
    cos2 = jnp.concatenate([cos, cos], axis=-1)
    sin2 = jnp.concatenate([-sin, sin], axis=-1)
    cos2 = jnp.concatenate([jnp.tile(cos2, (nb, 1)), jnp.ones((ctx_rows, DIFF_HEAD_DIM), F32)], axis=0)
    sin2 = jnp.concatenate([jnp.tile(sin2, (nb, 1)), jnp.zeros((ctx_rows, DIFF_HEAD_DIM), F32)], axis=0)
    return cos2, sin2


def kernel(x, c, ctx, c_ctx, w_mod, b_mod, norm_g, w_mlp_in, w_mlp_out, fnet_w_out, hgrn_w_in,
           hgrn_lb, hgrn_gnorm, hgrn_w_out, diff_w_qkv, diff_lambda, diff_subln, diff_w_out):
    nb, seq, d = x.shape
    ctx_len = ctx.shape[1]
    depth = w_mod.shape[0]
    lat_rows, ctx_rows = nb * seq, nb * ctx_len
    assert nb + 1 <= MOD_ROWS

    xs = jnp.concatenate([x.reshape(lat_rows, d), ctx.reshape(ctx_rows, d)], axis=0)
    cc = jnp.concatenate([c, c_ctx[None, :], jnp.zeros((MOD_ROWS - nb - 1, d), F32)], axis=0)
    mods = _mods(cc, w_mod, b_mod)

    for i in range(depth):
        mixer, slot = i % N_MIXERS, i // N_MIXERS
        ctx_out = i < depth - 1
        ctx_in = ctx_out or mixer != 0
        assert ctx_out or mixer == 0
        nrows = lat_rows + ctx_rows if ctx_in else lat_rows
        common = dict(nrows=nrows, seq=seq, nb=nb)
        if mixer == 0:
            gd = d // FNET_GROUPS
            cg, sg = _dft_tables(gd)
            wcs = jnp.concatenate([cg, -sg], axis=1).astype(BF16)
            a, bn = _norm_mod(xs, norm_g, mods, i, 0, 0, wcs=wcs, **common)
            y = _posdft(a, bn, nb, seq, 0, nrows)
            if ctx_in:
                y = _posdft(a, bn, nb, ctx_len, lat_rows, nrows, prev=y)
            w_out = fnet_w_out[slot]
        elif mixer == 1:
            h = _norm_mod(xs, norm_g, mods, i, 0, 0, **common)
            w_in = hgrn_w_in[slot]
            f = hgrn_lb.shape[-1]
            pq = _mm(h, w_in, 0, f, BF16, name="hgrn_q")
            pz = _mm(h, w_in, f, 2 * f, F32, name="hgrn_z")
            pvg = _mm(h, w_in, 3 * f, 2 * d, BF16, name="hgrn_vg")
            y = _gla(pq, pz, pvg, hgrn_lb, hgrn_gnorm[slot], i, nb, seq, ctx_len)
            w_out = hgrn_w_out[slot]
        else:
            h = _norm_mod(xs, norm_g, mods, i, 0, 0, **common)
            cos2, sin2 = _rope_tables(seq, nb, ctx_rows)
            rope = (cos2, sin2, 2 * d, d, DIFF_HEAD_DIM ** -0.5)
            qkv = _mm(h, diff_w_qkv[slot], 0, 3 * d, BF16, rope=rope, name="diff_qkv")
            lambda_init = 0.8 - 0.6 * math.exp(-0.3 * i)
            y = _attn(qkv, diff_lambda[slot], diff_subln[slot], lambda_init, nb, seq, ctx_len)
            w_out = diff_w_out[slot]
        xs = _mm_res(y, w_out, xs, norm_g, mods, i, 1, 2, **common)
        xs = _mlp(xs, w_mlp_in, w_mlp_out, norm_g, mods, i, **common)
    return xs[:lat_rows].reshape(nb, seq, d)
```

```python
import functools
import math

import jax
import jax.numpy as jnp
from jax import lax
from jax.experimental import pallas as pl
from jax.experimental.pallas import tpu as pltpu

F32 = jnp.float32
BF16 = jnp.bfloat16

NORM_EPS = 1e-6
ROPE_BASE = 10000.0
GRID_W = 64
N_MOD = 6
N_MIXERS = 3
FNET_GROUPS = 8
HGRN_DK = 128
DIFF_HEAD_DIM = 128
MOD_ROWS = 8

GLA_SUB = 16
GLA_SEG = 256
MLP_OUT_CHUNK = 512
V7X_VMEM_LIMIT = 56 * 1024 * 1024


def _params(sem, vmem=V7X_VMEM_LIMIT):
    return pltpu.CompilerParams(dimension_semantics=sem, vmem_limit_bytes=vmem)


def _silu(v):
    return v * jax.nn.sigmoid(v)


def _rms(y, gamma):
    return y * lax.rsqrt(jnp.mean(y * y, axis=-1, keepdims=True) + NORM_EPS) * gamma


def _mod_row(ref, tile, tiles_per_batch, n_batch):
    r = jnp.minimum(tile // tiles_per_batch, n_batch)
    return ref[pl.ds(r, 1), :]


def _mods_kernel(c_ref, w_ref, b_ref, o_ref):
    a = _silu(c_ref[...]).astype(BF16)
    o_ref[...] = jnp.dot(a, w_ref[...].astype(BF16), preferred_element_type=F32) + b_ref[...]


def _mods(cc, w_mod, b_mod):
    depth, d, n = w_mod.shape
    tn = 1024
    return pl.pallas_call(
        _mods_kernel,
        grid=(depth, n // tn),
        in_specs=[
            pl.BlockSpec((MOD_ROWS, d), lambda l, j: (0, 0)),
            pl.BlockSpec((None, d, tn), lambda l, j: (l, 0, j)),
            pl.BlockSpec((None, 1, tn), lambda l, j: (l, 0, j)),
        ],
        out_specs=pl.BlockSpec((None, MOD_ROWS, tn), lambda l, j: (l, 0, j)),
        out_shape=jax.ShapeDtypeStruct((depth, MOD_ROWS, n), F32),
        compiler_params=_params(("parallel", "parallel")),
        name="mods",
    )(cc, w_mod, b_mod.reshape(depth, 1, n))


def _norm_mod_kernel(x_ref, g_ref, sh_ref, sc_ref, *rest, tpb, nb, grow, groups):
    i = pl.program_id(0)
    sh = _mod_row(sh_ref, i, tpb, nb)
    sc = _mod_row(sc_ref, i, tpb, nb)
    h = _rms(x_ref[...], g_ref[grow:grow + 1, :]) * (1.0 + sc) + sh
    hb = h.astype(BF16)
    if groups == 0:
        (o_ref,) = rest
        o_ref[...] = hb
    else:
        wcs_ref, a_ref, b_ref = rest
        gd = hb.shape[1] // groups
        for g in range(groups):
            r = jnp.dot(hb[:, g * gd:(g + 1) * gd], wcs_ref[...], preferred_element_type=F32)
            a_ref[:, g * gd:(g + 1) * gd] = r[:, :gd].astype(BF16)
            b_ref[:, g * gd:(g + 1) * gd] = r[:, gd:].astype(BF16)


def _norm_mod(x, norm_g, mods, layer, grow, col, nrows, seq, nb, wcs=None):
    d = x.shape[1]
    tm = 256
    tpb = seq // tm
    groups = 0 if wcs is None else FNET_GROUPS
    in_specs = [
        pl.BlockSpec((tm, d), lambda i: (i, 0)),
        pl.BlockSpec((None, 4, d), lambda i: (layer, 0, 0)),
        pl.BlockSpec((None, MOD_ROWS, d), lambda i: (layer, 0, col)),
        pl.BlockSpec((None, MOD_ROWS, d), lambda i: (layer, 0, col + 1)),
    ]
    args = [x, norm_g, mods, mods]
    out_spec = pl.BlockSpec((tm, d), lambda i: (i, 0))
    out_shape = jax.ShapeDtypeStruct((nrows, d), BF16)
    if wcs is not None:
        in_specs.append(pl.BlockSpec(wcs.shape, lambda i: (0, 0)))
        args.append(wcs)
        out_spec = [out_spec, out_spec]
        out_shape = [out_shape, out_shape]
    return pl.pallas_call(
        functools.partial(_norm_mod_kernel, tpb=tpb, nb=nb, grow=grow, groups=groups),
        grid=(nrows // tm,),
        in_specs=in_specs,
        out_specs=out_spec,
        out_shape=out_shape,
        compiler_params=_params(("parallel",)),
        name="norm_mod",
    )(*args)


def _mm_kernel(a_ref, w_ref, *rest, rope_tiles, q_tiles, q_scale):
    if rope_tiles:
        cos_ref, sin_ref, o_ref, wb_ref = rest
    else:
        o_ref, wb_ref = rest
    j = pl.program_id(0)

    @pl.when(pl.program_id(1) == 0)
    def _():
        wb_ref[...] = w_ref[...].astype(BF16)

    acc = jnp.dot(a_ref[...], wb_ref[...], preferred_element_type=F32)
    if not rope_tiles:
        o_ref[...] = acc.astype(o_ref.dtype)
        return

    @pl.when(j < rope_tiles)
    def _():
        cos = cos_ref[...]
        sin = sin_ref[...]
        scale = jnp.where(j < q_tiles, q_scale, 1.0).astype(F32)
        hd = cos.shape[1]
        for c in range(acc.shape[1] // hd):
            blk = acc[:, c * hd:(c + 1) * hd]
            rot = pltpu.roll(blk, hd // 2, axis=1)
            o_ref[:, c * hd:(c + 1) * hd] = ((blk * cos + rot * sin) * scale).astype(o_ref.dtype)

    @pl.when(j >= rope_tiles)
    def _():
        o_ref[...] = acc.astype(o_ref.dtype)


def _mm(a, w, col0, ncols, out_dtype, rope=None, name="mm"):
    m, k = a.shape
    tm = min(1024, m)
    tn = 1024
    assert m % tm == 0 and ncols % tn == 0 and col0 % tn == 0
    off = col0 // tn
    in_specs = [
        pl.BlockSpec((tm, k), lambda j, i: (i, 0)),
        pl.BlockSpec((k, tn), lambda j, i: (0, j + off)),
    ]
    args = [a, w]
    rope_tiles = q_tiles = 0
    q_scale = 1.0
    if rope is not None:
        cos2, sin2, rope_cols, q_cols, q_scale = rope
        hd = cos2.shape[1]
        in_specs += [pl.BlockSpec((tm, hd), lambda j, i: (i, 0))] * 2
        args += [cos2, sin2]
        rope_tiles, q_tiles = rope_cols // tn, q_cols // tn
    return pl.pallas_call(
        functools.partial(_mm_kernel, rope_tiles=rope_tiles, q_tiles=q_tiles, q_scale=q_scale),
        grid=(ncols // tn, m // tm),
        in_specs=in_specs,
        out_specs=pl.BlockSpec((tm, tn), lambda j, i: (i, j)),
        out_shape=jax.ShapeDtypeStruct((m, ncols), out_dtype),
        scratch_shapes=[pltpu.VMEM((k, tn), BF16)],
        compiler_params=_params(("parallel", "arbitrary")),
        name=name,
    )(*args)


def _mm_res_kernel(a_ref, w_ref, x_ref, g_ref, gate_ref, o_ref, *, tpb, nb, grow):
    y = jnp.dot(a_ref[...], w_ref[...], preferred_element_type=F32)
    gate = _mod_row(gate_ref, pl.program_id(0), tpb, nb)
    o_ref[...] = x_ref[...] + gate * _rms(y, g_ref[grow:grow + 1, :])


def _mm_res(a, w, x, norm_g, mods, layer, grow, gate_col, nrows, seq, nb):
    kdim, d = w.shape
    tm = 512
    tpb = seq // tm
    return pl.pallas_call(
        functools.partial(_mm_res_kernel, tpb=tpb, nb=nb, grow=grow),
        grid=(nrows // tm,),
        in_specs=[
            pl.BlockSpec((tm, kdim), lambda i: (i, 0)),
            pl.BlockSpec((kdim, d), lambda i: (0, 0)),
            pl.BlockSpec((tm, d), lambda i: (i, 0)),
            pl.BlockSpec((None, 4, d), lambda i: (layer, 0, 0)),
            pl.BlockSpec((None, MOD_ROWS, d), lambda i: (layer, 0, gate_col)),
        ],
        out_specs=pl.BlockSpec((tm, d), lambda i: (i, 0)),
        out_shape=jax.ShapeDtypeStruct((nrows, d), F32),
        compiler_params=_params(("parallel",)),
        name="mm_res",
    )(a, w.astype(BF16), x, norm_g, mods)


def _mlp_kernel(x_ref, g_ref, sh_ref, sc_ref, gate_ref, w1_ref, w2_ref, o_ref, h_ref, *, tpb, nb):
    i = pl.program_id(0)
    k = pl.program_id(1)

    @pl.when(k == 0)
    def _():
        sh = _mod_row(sh_ref, i, tpb, nb)
        sc = _mod_row(sc_ref, i, tpb, nb)
        h = _rms(x_ref[...], g_ref[2:3, :]) * (1.0 + sc) + sh
        h_ref[...] = h.astype(BF16)
        o_ref[...] = jnp.zeros_like(o_ref)

    u = jnp.dot(h_ref[...], w1_ref[...].astype(BF16), preferred_element_type=F32)
    u = jnp.square(jnp.maximum(u, 0.0)).astype(BF16)
    cn = MLP_OUT_CHUNK
    for c in range(o_ref.shape[1] // cn):
        cols = slice(c * cn, (c + 1) * cn)
        o_ref[:, cols] += jnp.dot(u, w2_ref[:, cols].astype(BF16), preferred_element_type=F32)

    @pl.when(k == pl.num_programs(1) - 1)
    def _():
        gate = _mod_row(gate_ref, i, tpb, nb)
        o_ref[...] = x_ref[...] + gate * _rms(o_ref[...], g_ref[3:4, :])


def _mlp(x, w1, w2, norm_g, mods, layer, nrows, seq, nb):
    d, hid = w1.shape[1], w1.shape[2]
    tm = 1024
    th = 512
    tpb = seq // tm
    mod_spec = lambda col: pl.BlockSpec((None, MOD_ROWS, d), lambda i, k: (layer, 0, col))
    row_spec = pl.BlockSpec((tm, d), lambda i, k: (i, 0), pipeline_mode=pl.Buffered(1))
    return pl.pallas_call(
        functools.partial(_mlp_kernel, tpb=tpb, nb=nb),
        grid=(nrows // tm, hid // th),
        in_specs=[
            row_spec,
            pl.BlockSpec((None, 4, d), lambda i, k: (layer, 0, 0)),
            mod_spec(3), mod_spec(4), mod_spec(5),
            pl.BlockSpec((None, d, th), lambda i, k: (layer, 0, k)),
            pl.BlockSpec((None, th, d), lambda i, k: (layer, k, 0)),
        ],
        out_specs=row_spec,
        out_shape=jax.ShapeDtypeStruct((nrows, d), F32),
        scratch_shapes=[pltpu.VMEM((tm, d), BF16)],
        compiler_params=_params(("parallel", "arbitrary")),
        name="mlp",
    )(x, norm_g, mods, mods, mods, w1, w2)


def _dft_tables(n):
    idx = jnp.arange(n, dtype=jnp.int32)
    ang = ((idx[:, None] * idx[None, :]) % n).astype(F32) * (2.0 * math.pi / n)
    return jnp.cos(ang), jnp.sin(ang)


def _posdft_kernel(c_ref, s_ref, a_ref, b_ref, *rest):
    o_ref = rest[-1]
    acc = jnp.dot(c_ref[...], a_ref[...], preferred_element_type=F32)
    acc += jnp.dot(s_ref[...], b_ref[...], preferred_element_type=F32)
    o_ref[...] = acc.astype(o_ref.dtype)


def _posdft(a, bn, nbatch, seq, row0, out_rows, prev=None):
    d = a.shape[1]
    cos, sin = _dft_tables(seq)
    cos, sin = cos.astype(BF16), sin.astype(BF16)
    tm = min(512, seq)
    tn = 512
    assert row0 % seq == 0
    rb0 = row0 // seq
    mb0 = row0 // tm
    in_specs = [
        pl.BlockSpec((tm, seq), lambda b, j, m: (m, 0)),
        pl.BlockSpec((tm, seq), lambda b, j, m: (m, 0)),
        pl.BlockSpec((seq, tn), lambda b, j, m: (rb0 + b, j)),
        pl.BlockSpec((seq, tn), lambda b, j, m: (rb0 + b, j)),
    ]
    args = [cos, sin, a, bn]
    aliases = {}
    if prev is not None:
        in_specs.append(pl.BlockSpec(memory_space=pl.ANY))
        args.append(prev)
        aliases = {4: 0}
    return pl.pallas_call(
        _posdft_kernel,
        grid=(nbatch, d // tn, seq // tm),
        in_specs=in_specs,
        out_specs=pl.BlockSpec((tm, tn), lambda b, j, m: (mb0 + b * (seq // tm) + m, j)),
        out_shape=jax.ShapeDtypeStruct((out_rows, d), BF16),
        input_output_aliases=aliases,
        compiler_params=_params(("parallel", "parallel", "arbitrary")),
        name="posdft",
    )(*args)


def _split3_dot(mask_bf16, v):
    hi = v.astype(BF16)
    r1 = v - hi.astype(F32)
    mid = r1.astype(BF16)
    lo = (r1 - mid.astype(F32)).astype(BF16)
    out = jnp.dot(mask_bf16, hi, preferred_element_type=F32)
    out += jnp.dot(mask_bf16, mid, preferred_element_type=F32)
    out += jnp.dot(mask_bf16, lo, preferred_element_type=F32)
    return out


def _gla_segment(fwd, q_ref, z_ref, v_ref, lb, st_ref, o_ref):
    seg_rows = q_ref.shape[0]
    nsub = seg_rows // GLA_SUB
    hs = GLA_SUB // 2

    z = z_ref[...]
    ez = jnp.exp(-jnp.abs(z))
    r = 1.0 / (1.0 + ez)
    pos = z >= 0.0
    sig = jnp.where(pos, r, ez * r)
    nsig = jnp.where(pos, ez * r, r)
    logf = jnp.log(lb + (1.0 - lb) * sig)
    kk = (1.0 - lb) * nsig
    q = q_ref[...].astype(F32)
    v = v_ref[...].astype(F32)

    half = 128
    ri = lax.broadcasted_iota(jnp.int32, (half, half), 0)
    ci = lax.broadcasted_iota(jnp.int32, (half, half), 1)
    same = (ri // GLA_SUB) == (ci // GLA_SUB)
    ahead = (ri - ci) if fwd else (ci - ri)
    m_incl = jnp.where(same & (ahead >= 0), 1.0, 0.0).astype(BF16)
    m_excl = jnp.where(same & (ahead < 0), 1.0, 0.0).astype(BF16)
    bcs, qds, kds = [], [], []
    for hseg in range(seg_rows // half):
        rows = slice(hseg * half, (hseg + 1) * half)
        bc = _split3_dot(m_incl, logf[rows])
        rest = _split3_dot(m_excl, logf[rows])
        bcs.append(bc)
        qds.append((q[rows] * jnp.exp(bc)).astype(BF16))
        kds.append((kk[rows] * jnp.exp(rest)).astype(BF16))
    bc_all = jnp.concatenate(bcs, axis=0)
    qd_all = jnp.concatenate(qds, axis=0)
    kd_all = jnp.concatenate(kds, axis=0)

    rowh = lax.broadcasted_iota(jnp.int32, (hs, HGRN_DK), 0)
    st = st_ref[...]
    for j in range(nsub):
        i = j if fwd else nsub - 1 - j
        r0 = i * GLA_SUB
        o = lax.dot_general(qd_all[r0:r0 + GLA_SUB], st.astype(BF16), (((1,), (1,)), ((), ())),
                            preferred_element_type=F32)
        acc = [o[:hs], o[hs:]]
        for s in range(GLA_SUB):
            sr = r0 + s
            b_s, k_s, v_s = bc_all[sr:sr + 1], kk[sr:sr + 1], v[sr:sr + 1]
            for hh in range(2):
                lo_row = hh * hs
                reach_all = (lo_row > s) if fwd else (lo_row + hs - 1 < s)
                reach_none = (lo_row + hs - 1 < s) if fwd else (lo_row > s)
                if reach_none:
                    continue
                rr = slice(r0 + lo_row, r0 + lo_row + hs)
                diff = bc_all[rr] - b_s
                if not reach_all:
                    keep = (rowh + lo_row >= s) if fwd else (rowh + lo_row <= s)
                    diff = jnp.where(keep, diff, -1e30)
                col = jnp.sum(q[rr] * jnp.exp(diff) * k_s, axis=1, keepdims=True)
                acc[hh] = acc[hh] + col * v_s
        o_ref[r0:r0 + hs, :] = acc[0]
        o_ref[r0 + hs:r0 + GLA_SUB, :] = acc[1]
        last = r0 + GLA_SUB - 1 if fwd else r0
        upd = lax.dot_general(v_ref[r0:r0 + GLA_SUB, :], kd_all[r0:r0 + GLA_SUB],
                              (((0,), (0,)), ((), ())), preferred_element_type=F32)
        st = st * jnp.exp(bc_all[last:last + 1]) + upd
    st_ref[...] = st


def _gla_kernel(q_ref, z_ref, v_ref, g_ref, lbp_ref, gn_ref, y_ref, of_ref, st_ref, o_ref,
                *, layer, nseg):
    t = pl.program_id(2)
    fwd = t < nseg
    seg = t % nseg
    seg_rows = q_ref.shape[0]

    @pl.when(seg == 0)
    def _():
        st_ref[...] = jnp.zeros_like(st_ref)

    p = lbp_ref[...]
    e = jnp.exp(p - jnp.max(p, axis=0, keepdims=True))
    sm = e / jnp.sum(e, axis=0, keepdims=True)
    lb2 = jnp.zeros_like(sm[0])
    for j in range(1, layer + 1):
        lb2 = lb2 + sm[j]

    @pl.when(fwd)
    def _():
        _gla_segment(True, q_ref, z_ref, v_ref, lb2[0:1, :], st_ref, o_ref)
        of_ref[pl.ds(pl.multiple_of(seg * seg_rows, seg_rows), seg_rows), :] = o_ref[...]

    @pl.when(jnp.logical_not(fwd))
    def _():
        _gla_segment(False, q_ref, z_ref, v_ref, lb2[1:2, :], st_ref, o_ref)
        fseg = jnp.where(seg == 0, 0, nseg - seg)
        o = o_ref[...] + of_ref[pl.ds(pl.multiple_of(fseg * seg_rows, seg_rows), seg_rows), :]
        y = _rms(o, gn_ref[...]) * _silu(g_ref[...].astype(F32))
        y_ref[...] = y.astype(y_ref.dtype)


def _gla(pq, pz, pvg, hgrn_lb, gnorm, layer, nb, seq, ctx_len):
    t_rows, f = pq.shape
    heads = f // HGRN_DK
    sr = GLA_SEG
    assert ctx_len == sr and seq % sr == 0
    nlat = seq // sr
    nseg = nlat + 1
    lat_blocks = nb * nlat
    depth = hgrn_lb.shape[0]

    def rb(b, t):
        seg = t % nseg
        lat = b * nlat + jnp.where(t < nseg, seg - 1, nlat - seg)
        return jnp.where(seg == 0, lat_blocks + b, lat)

    blk = lambda colf: pl.BlockSpec((sr, HGRN_DK), lambda b, h, t: (rb(b, t), colf(h, t)))
    return pl.pallas_call(
        functools.partial(_gla_kernel, layer=layer, nseg=nseg),
        grid=(nb, heads, 2 * nseg),
        in_specs=[
            blk(lambda h, t: h),
            blk(lambda h, t: jnp.where(t < nseg, 0, heads) + h),
            blk(lambda h, t: h),
            blk(lambda h, t: heads + h),
            pl.BlockSpec((depth, 2, HGRN_DK), lambda b, h, t: (0, 0, h)),
            pl.BlockSpec((1, HGRN_DK), lambda b, h, t: (0, 0)),
        ],
        out_specs=pl.BlockSpec((sr, HGRN_DK), lambda b, h, t: (rb(b, jnp.maximum(t, nseg)), h)),
        out_shape=jax.ShapeDtypeStruct((t_rows, f), BF16),
        scratch_shapes=[
            pltpu.VMEM((nseg * sr, HGRN_DK), F32),
            pltpu.VMEM((HGRN_DK, HGRN_DK), F32),
            pltpu.VMEM((sr, HGRN_DK), F32),
        ],
        compiler_params=_params(("parallel", "parallel", "arbitrary")),
        name="gla",
    )(pq, pz, pvg, pvg, hgrn_lb, gnorm.reshape(1, HGRN_DK))


def _dot_nt(a, b):
    return lax.dot_general(a, b, (((1,), (1,)), ((), ())), preferred_element_type=F32)


def _attn_kernel(q1_ref, q2_ref, k1l_ref, k2l_ref, k1c_ref, k2c_ref, vl_ref, vc_ref,
                 lam_ref, sub_ref, o_ref, *, lambda_init, nq):
    qi = pl.program_id(2)
    lp = lam_ref[...]
    lam = (jnp.exp(jnp.sum(lp[0:1, :] * lp[1:2, :], keepdims=True))
           - jnp.exp(jnp.sum(lp[2:3, :] * lp[3:4, :], keepdims=True)) + lambda_init)

    def probs(q_ref, kc_ref, kl_ref):
        sc = _dot_nt(q_ref[...], kc_ref[...])
        m = jnp.max(sc, axis=-1, keepdims=True)
        if kl_ref is not None:
            sl = _dot_nt(q_ref[...], kl_ref[...])
            m = jnp.maximum(m, jnp.max(sl, axis=-1, keepdims=True))
        pc = jnp.exp2(sc - m)
        den = jnp.sum(pc, axis=-1, keepdims=True)
        pl_ = None
        if kl_ref is not None:
            pl_ = jnp.exp2(sl - m)
            den = den + jnp.sum(pl_, axis=-1, keepdims=True)
            pl_ = pl_.astype(BF16)
        return pc.astype(BF16), pl_, 1.0 / den

    def finish(o):
        y = _rms(o, sub_ref[...]) * (1.0 - lambda_init)
        o_ref[...] = y.astype(o_ref.dtype)

    def pv(pc, pl_):
        o = jnp.dot(pc, vc_ref[...], preferred_element_type=F32)
        if pl_ is not None:
            o += jnp.dot(pl_, vl_ref[...], preferred_element_type=F32)
        return o

    @pl.when(qi < nq)
    def _():
        p1c, p1l, i1 = probs(q1_ref, k1c_ref, k1l_ref)
        o1 = pv(p1c, p1l)
        p2c, p2l, i2 = probs(q2_ref, k2c_ref, k2l_ref)
        o2 = pv(p2c, p2l)
        finish(o1 * i1 - o2 * (lam * i2))

    @pl.when(qi == nq)
    def _():
        p1c, _, i1 = probs(q1_ref, k1c_ref, None)
        p2c, _, i2 = probs(q2_ref, k2c_ref, None)
        finish(pv(p1c, None) * i1 - pv(p2c, None) * (lam * i2))


def _attn(qkv, lam_p, subln, lambda_init, nb, seq, ctx_len):
    t_rows, d3 = qkv.shape
    d = d3 // 3
    hd = DIFF_HEAD_DIM
    heads = d // (2 * hd)
    tq = ctx_len
    nq = seq // tq
    lat_qblocks = nb * nq
    kcol = d // hd
    vcol = 2 * d // (2 * hd)
    qrow = lambda b, qi: jnp.where(qi == nq, lat_qblocks + b, b * nq + qi)
    q_spec = lambda m: pl.BlockSpec((tq, hd), lambda b, h, qi: (qrow(b, qi), 2 * h + m))
    kl_spec = lambda m: pl.BlockSpec((seq, hd), lambda b, h, qi: (b, kcol + 2 * h + m))
    kc_spec = lambda m: pl.BlockSpec((ctx_len, hd), lambda b, h, qi: (lat_qblocks + b, kcol + 2 * h + m))
    return pl.pallas_call(
        functools.partial(_attn_kernel, lambda_init=lambda_init, nq=nq),
        grid=(nb, heads, nq + 1),
        in_specs=[
            q_spec(0), q_spec(1), kl_spec(0), kl_spec(1), kc_spec(0), kc_spec(1),
            pl.BlockSpec((seq, 2 * hd), lambda b, h, qi: (b, vcol + h)),
            pl.BlockSpec((ctx_len, 2 * hd), lambda b, h, qi: (lat_qblocks + b, vcol + h)),
            pl.BlockSpec((4, hd), lambda b, h, qi: (0, 0)),
            pl.BlockSpec((1, 2 * hd), lambda b, h, qi: (0, 0)),
        ],
        out_specs=pl.BlockSpec((tq, 2 * hd), lambda b, h, qi: (qrow(b, qi), h)),
        out_shape=jax.ShapeDtypeStruct((t_rows, d), BF16),
        compiler_params=_params(("parallel", "parallel", "arbitrary")),
        name="diff_attn",
    )(qkv, qkv, qkv, qkv, qkv, qkv, qkv, qkv, lam_p, subln.reshape(1, 2 * hd))


def _rope_tables(seq, nb, ctx_rows):
    rows = seq // GRID_W
    row = jnp.repeat(jnp.arange(rows, dtype=F32), GRID_W)
    col = jnp.tile(jnp.arange(GRID_W, dtype=F32), rows)
    n_freq = DIFF_HEAD_DIM // 4
    inv = ROPE_BASE ** (-jnp.arange(n_freq, dtype=F32) / n_freq)
    ang = jnp.concatenate([row[:, None] * inv, col[:, None] * inv], axis=-1)
    cos, sin = jnp.cos(ang), jnp.sin(ang)
    cos2 = jnp.concatenate([cos, cos], axis=-1)
    sin2 = jnp.concatenate([-sin, sin], axis=-1)
    cos2 = jnp.concatenate([jnp.tile(cos2, (nb, 1)), jnp.ones((ctx_rows, DIFF_HEAD_DIM), F32)], axis=0)
    sin2 = jnp.concatenate([jnp.tile(sin2, (nb, 1)), jnp.zeros((ctx_rows, DIFF_HEAD_DIM), F32)], axis=0)
    return cos2, sin2


def kernel(x, c, ctx, c_ctx, w_mod, b_mod, norm_g, w_mlp_in, w_mlp_out, fnet_w_out, hgrn_w_in,
           hgrn_lb, hgrn_gnorm, hgrn_w_out, diff_w_qkv, diff_lambda, diff_subln, diff_w_out):
    nb, seq, d = x.shape
    ctx_len = ctx.shape[1]
    depth = w_mod.shape[0]
    lat_rows, ctx_rows = nb * seq, nb * ctx_len
    assert nb + 1 <= MOD_ROWS

    xs = jnp.concatenate([x.reshape(lat_rows, d), ctx.reshape(ctx_rows, d)], axis=0)
    cc = jnp.concatenate([c, c_ctx[None, :], jnp.zeros((MOD_ROWS - nb - 1, d), F32)], axis=0)
    mods = _mods(cc, w_mod, b_mod)

    for i in range(depth):
        mixer, slot = i % N_MIXERS, i // N_MIXERS
        ctx_out = i < depth - 1
        ctx_in = ctx_out or mixer != 0
        assert ctx_out or mixer == 0
        nrows = lat_rows + ctx_rows if ctx_in else lat_rows
        common = dict(nrows=nrows, seq=seq, nb=nb)
        if mixer == 0:
            gd = d // FNET_GROUPS
            cg, sg = _dft_tables(gd)
            wcs = jnp.concatenate([cg, -sg], axis=1).astype(BF16)
            a, bn = _norm_mod(xs, norm_g, mods, i, 0, 0, wcs=wcs, **common)
            y = _posdft(a, bn, nb, seq, 0, nrows)
            if ctx_in:
                y = _posdft(a, bn, nb, ctx_len, lat_rows, nrows, prev=y)
            w_out = fnet_w_out[slot]
        elif mixer == 1:
            h = _norm_mod(xs, norm_g, mods, i, 0, 0, **common)
            w_in = hgrn_w_in[slot]
            f = hgrn_lb.shape[-1]
            pq = _mm(h, w_in, 0, f, BF16, name="hgrn_q")
            pz = _mm(h, w_in, f, 2 * f, F32, name="hgrn_z")
            pvg = _mm(h, w_in, 3 * f, 2 * d, BF16, name="hgrn_vg")
            y = _gla(pq, pz, pvg, hgrn_lb, hgrn_gnorm[slot], i, nb, seq, ctx_len)
            w_out = hgrn_w_out[slot]
        else:
            h = _norm_mod(xs, norm_g, mods, i, 0, 0, **common)
            cos2, sin2 = _rope_tables(seq, nb, ctx_rows)
            rope = (cos2, sin2, 2 * d, d, DIFF_HEAD_DIM ** -0.5 * math.log2(math.e))
            qkv = _mm(h, diff_w_qkv[slot], 0, 3 * d, BF16, rope=rope, name="diff_qkv")
            lambda_init = 0.8 - 0.6 * math.exp(-0.3 * i)
            y = _attn(qkv, diff_lambda[slot], diff_subln[slot], lambda_init, nb, seq, ctx_len)
            w_out = diff_w_out[slot]
        xs = _mm_res(y, w_out, xs, norm_g, mods, i, 1, 2, **common)
        xs = _mlp(xs, w_mlp_in, w_mlp_out, norm_g, mods, i, **common)
    return xs[:lat_rows].reshape(nb, seq, d)
```

```python
import functools
import math

import jax
import jax.numpy as jnp
from jax import lax
from jax.experimental import pallas as pl
from jax.experimental.pallas import tpu as pltpu

F32 = jnp.float32
BF16 = jnp.bfloat16

NORM_EPS = 1e-6
ROPE_BASE = 10000.0
GRID_W = 64
N_MOD = 6
N_MIXERS = 3
FNET_GROUPS = 8
HGRN_DK = 128
DIFF_HEAD_DIM = 128
MOD_ROWS = 8

GLA_SUB = 16
GLA_CHUNK = 64
GLA_HEADS_PER_STEP = 4
GLA_FAST_MAX_DECAY = 150.0
GLA_SEG = 256
MLP_OUT_CHUNK = 512
V7X_VMEM_LIMIT = 56 * 1024 * 1024


def _params(sem, vmem=V7X_VMEM_LIMIT):
    return pltpu.CompilerParams(dimension_semantics=sem, vmem_limit_bytes=vmem)


def _silu(v):
    return v * jax.nn.sigmoid(v)


def _rms(y, gamma):
    return y * lax.rsqrt(jnp.mean(y * y, axis=-1, keepdims=True) + NORM_EPS) * gamma


def _mod_row(ref, tile, tiles_per_batch, n_batch):
    r = jnp.minimum(tile // tiles_per_batch, n_batch)
    return ref[pl.ds(r, 1), :]


def _mods_kernel(c_ref, w_ref, b_ref, o_ref):
    a = _silu(c_ref[...]).astype(BF16)
    o_ref[...] = jnp.dot(a, w_ref[...].astype(BF16), preferred_element_type=F32) + b_ref[...]


def _mods(cc, w_mod, b_mod):
    depth, d, n = w_mod.shape
    tn = 1024
    return pl.pallas_call(
        _mods_kernel,
        grid=(depth, n // tn),
        in_specs=[
            pl.BlockSpec((MOD_ROWS, d), lambda l, j: (0, 0)),
            pl.BlockSpec((None, d, tn), lambda l, j: (l, 0, j)),
            pl.BlockSpec((None, 1, tn), lambda l, j: (l, 0, j)),
        ],
        out_specs=pl.BlockSpec((None, MOD_ROWS, tn), lambda l, j: (l, 0, j)),
        out_shape=jax.ShapeDtypeStruct((depth, MOD_ROWS, n), F32),
        compiler_params=_params(("parallel", "parallel")),
        name="mods",
    )(cc, w_mod, b_mod.reshape(depth, 1, n))


def _norm_mod_kernel(x_ref, g_ref, sh_ref, sc_ref, *rest, tpb, nb, grow, groups):
    i = pl.program_id(0)
    sh = _mod_row(sh_ref, i, tpb, nb)
    sc = _mod_row(sc_ref, i, tpb, nb)
    h = _rms(x_ref[...], g_ref[grow:grow + 1, :]) * (1.0 + sc) + sh
    hb = h.astype(BF16)
    if groups == 0:
        (o_ref,) = rest
        o_ref[...] = hb
    else:
        wcs_ref, a_ref, b_ref = rest
        gd = hb.shape[1] // groups
        for g in range(groups):
            r = jnp.dot(hb[:, g * gd:(g + 1) * gd], wcs_ref[...], preferred_element_type=F32)
            a_ref[:, g * gd:(g + 1) * gd] = r[:, :gd].astype(BF16)
            b_ref[:, g * gd:(g + 1) * gd] = r[:, gd:].astype(BF16)


def _norm_mod(x, norm_g, mods, layer, grow, col, nrows, seq, nb, wcs=None):
    d = x.shape[1]
    tm = 256
    tpb = seq // tm
    groups = 0 if wcs is None else FNET_GROUPS
    in_specs = [
        pl.BlockSpec((tm, d), lambda i: (i, 0)),
        pl.BlockSpec((None, 4, d), lambda i: (layer, 0, 0)),
        pl.BlockSpec((None, MOD_ROWS, d), lambda i: (layer, 0, col)),
        pl.BlockSpec((None, MOD_ROWS, d), lambda i: (layer, 0, col + 1)),
    ]
    args = [x, norm_g, mods, mods]
    out_spec = pl.BlockSpec((tm, d), lambda i: (i, 0))
    out_shape = jax.ShapeDtypeStruct((nrows, d), BF16)
    if wcs is not None:
        in_specs.append(pl.BlockSpec(wcs.shape, lambda i: (0, 0)))
        args.append(wcs)
        out_spec = [out_spec, out_spec]
        out_shape = [out_shape, out_shape]
    return pl.pallas_call(
        functools.partial(_norm_mod_kernel, tpb=tpb, nb=nb, grow=grow, groups=groups),
        grid=(nrows // tm,),
        in_specs=in_specs,
        out_specs=out_spec,
        out_shape=out_shape,
        compiler_params=_params(("parallel",)),
        name="norm_mod",
    )(*args)


def _mm_kernel(a_ref, w_ref, *rest, rope_tiles, q_tiles, q_scale):
    if rope_tiles:
        cos_ref, sin_ref, o_ref, wb_ref = rest
    else:
        o_ref, wb_ref = rest
    j = pl.program_id(0)

    @pl.when(pl.program_id(1) == 0)
    def _():
        wb_ref[...] = w_ref[...].astype(BF16)

    acc = jnp.dot(a_ref[...], wb_ref[...], preferred_element_type=F32)
    if not rope_tiles:
        o_ref[...] = acc.astype(o_ref.dtype)
        return

    @pl.when(j < rope_tiles)
    def _():
        cos = cos_ref[...]
        sin = sin_ref[...]
        scale = jnp.where(j < q_tiles, q_scale, 1.0).astype(F32)
        hd = cos.shape[1]
        for c in range(acc.shape[1] // hd):
            blk = acc[:, c * hd:(c + 1) * hd]
            rot = pltpu.roll(blk, hd // 2, axis=1)
            o_ref[:, c * hd:(c + 1) * hd] = ((blk * cos + rot * sin) * scale).astype(o_ref.dtype)

    @pl.when(j >= rope_tiles)
    def _():
        o_ref[...] = acc.astype(o_ref.dtype)


def _mm(a, w, col0, ncols, out_dtype, rope=None, name="mm"):
    m, k = a.shape
    tm = min(1024, m)
    tn = 1024
    assert m % tm == 0 and ncols % tn == 0 and col0 % tn == 0
    off = col0 // tn
    in_specs = [
        pl.BlockSpec((tm, k), lambda j, i: (i, 0)),
        pl.BlockSpec((k, tn), lambda j, i: (0, j + off)),
    ]
    args = [a, w]
    rope_tiles = q_tiles = 0
    q_scale = 1.0
    if rope is not None:
        cos2, sin2, rope_cols, q_cols, q_scale = rope
        hd = cos2.shape[1]
        in_specs += [pl.BlockSpec((tm, hd), lambda j, i: (i, 0))] * 2
        args += [cos2, sin2]
        rope_tiles, q_tiles = rope_cols // tn, q_cols // tn
    return pl.pallas_call(
        functools.partial(_mm_kernel, rope_tiles=rope_tiles, q_tiles=q_tiles, q_scale=q_scale),
        grid=(ncols // tn, m // tm),
        in_specs=in_specs,
        out_specs=pl.BlockSpec((tm, tn), lambda j, i: (i, j)),
        out_shape=jax.ShapeDtypeStruct((m, ncols), out_dtype),
        scratch_shapes=[pltpu.VMEM((k, tn), BF16)],
        compiler_params=_params(("parallel", "arbitrary")),
        name=name,
    )(*args)


def _mm_res_kernel(a_ref, w_ref, x_ref, g_ref, gate_ref, o_ref, *, tpb, nb, grow):
    y = jnp.dot(a_ref[...], w_ref[...], preferred_element_type=F32)
    gate = _mod_row(gate_ref, pl.program_id(0), tpb, nb)
    o_ref[...] = x_ref[...] + gate * _rms(y, g_ref[grow:grow + 1, :])


def _mm_res(a, w, x, norm_g, mods, layer, grow, gate_col, nrows, seq, nb):
    kdim, d = w.shape
    tm = 512
    tpb = seq // tm
    return pl.pallas_call(
        functools.partial(_mm_res_kernel, tpb=tpb, nb=nb, grow=grow),
        grid=(nrows // tm,),
        in_specs=[
            pl.BlockSpec((tm, kdim), lambda i: (i, 0)),
            pl.BlockSpec((kdim, d), lambda i: (0, 0)),
            pl.BlockSpec((tm, d), lambda i: (i, 0)),
            pl.BlockSpec((None, 4, d), lambda i: (layer, 0, 0)),
            pl.BlockSpec((None, MOD_ROWS, d), lambda i: (layer, 0, gate_col)),
        ],
        out_specs=pl.BlockSpec((tm, d), lambda i: (i, 0)),
        out_shape=jax.ShapeDtypeStruct((nrows, d), F32),
        compiler_params=_params(("parallel",)),
        name="mm_res",
    )(a, w.astype(BF16), x, norm_g, mods)


def _mlp_kernel(x_ref, g_ref, sh_ref, sc_ref, gate_ref, w1_ref, w2_ref, o_ref, h_ref, *, tpb, nb):
    i = pl.program_id(0)
    k = pl.program_id(1)

    @pl.when(k == 0)
    def _():
        sh = _mod_row(sh_ref, i, tpb, nb)
        sc = _mod_row(sc_ref, i, tpb, nb)
        h = _rms(x_ref[...], g_ref[2:3, :]) * (1.0 + sc) + sh
        h_ref[...] = h.astype(BF16)
        o_ref[...] = jnp.zeros_like(o_ref)

    u = jnp.dot(h_ref[...], w1_ref[...].astype(BF16), preferred_element_type=F32)
    u = jnp.square(jnp.maximum(u, 0.0)).astype(BF16)
    cn = MLP_OUT_CHUNK
    for c in range(o_ref.shape[1] // cn):
        cols = slice(c * cn, (c + 1) * cn)
        o_ref[:, cols] += jnp.dot(u, w2_ref[:, cols].astype(BF16), preferred_element_type=F32)

    @pl.when(k == pl.num_programs(1) - 1)
    def _():
        gate = _mod_row(gate_ref, i, tpb, nb)
        o_ref[...] = x_ref[...] + gate * _rms(o_ref[...], g_ref[3:4, :])


def _mlp(x, w1, w2, norm_g, mods, layer, nrows, seq, nb):
    d, hid = w1.shape[1], w1.shape[2]
    tm = 1024
    th = 512
    tpb = seq // tm
    mod_spec = lambda col: pl.BlockSpec((None, MOD_ROWS, d), lambda i, k: (layer, 0, col))
    row_spec = pl.BlockSpec((tm, d), lambda i, k: (i, 0), pipeline_mode=pl.Buffered(1))
    return pl.pallas_call(
        functools.partial(_mlp_kernel, tpb=tpb, nb=nb),
        grid=(nrows // tm, hid // th),
        in_specs=[
            row_spec,
            pl.BlockSpec((None, 4, d), lambda i, k: (layer, 0, 0)),
            mod_spec(3), mod_spec(4), mod_spec(5),
            pl.BlockSpec((None, d, th), lambda i, k: (layer, 0, k)),
            pl.BlockSpec((None, th, d), lambda i, k: (layer, k, 0)),
        ],
        out_specs=row_spec,
        out_shape=jax.ShapeDtypeStruct((nrows, d), F32),
        scratch_shapes=[pltpu.VMEM((tm, d), BF16)],
        compiler_params=_params(("parallel", "arbitrary")),
        name="mlp",
    )(x, norm_g, mods, mods, mods, w1, w2)


def _dft_tables(n, w=32):
    assert n % w == 0
    k = jnp.arange(n, dtype=jnp.int32)[:, None]
    unit = 2.0 * math.pi / n
    ang_a = ((k * (w * jnp.arange(n // w, dtype=jnp.int32))[None, :]) % n).astype(F32) * unit
    ang_b = ((k * jnp.arange(w, dtype=jnp.int32)[None, :]) % n).astype(F32) * unit
    ca, sa = jnp.cos(ang_a)[:, :, None], jnp.sin(ang_a)[:, :, None]
    cb, sb = jnp.cos(ang_b)[:, None, :], jnp.sin(ang_b)[:, None, :]
    return (ca * cb - sa * sb).reshape(n, n), (sa * cb + ca * sb).reshape(n, n)


def _posdft_kernel(c_ref, s_ref, a_ref, b_ref, *rest, nm):
    def dft(cos_ref, sin_ref, x_ref, y_ref):
        acc = jnp.dot(cos_ref[...], x_ref[...], preferred_element_type=F32)
        acc += jnp.dot(sin_ref[...], y_ref[...], preferred_element_type=F32)
        return acc

    if len(rest) == 1:
        (o_ref,) = rest
        o_ref[...] = dft(c_ref, s_ref, a_ref, b_ref).astype(o_ref.dtype)
        return
    cc_ref, sc_ref, ac_ref, bc_ref, o_ref = rest
    m = pl.program_id(2)

    @pl.when(m < nm)
    def _():
        o_ref[...] = dft(c_ref, s_ref, a_ref, b_ref).astype(o_ref.dtype)

    @pl.when(m == nm)
    def _():
        o_ref[...] = dft(cc_ref, sc_ref, ac_ref, bc_ref).astype(o_ref.dtype)


def _posdft(a, bn, nbatch, seq, ctx_len):
    d = a.shape[1]
    tm = 256
    tn = 1024
    nm = seq // tm
    to_bf16 = lambda cs: tuple(t.astype(BF16) for t in cs)
    cos, sin = to_bf16(_dft_tables(seq))
    in_specs = [
        pl.BlockSpec((tm, seq), lambda b, j, m: (jnp.minimum(m, nm - 1), 0)),
        pl.BlockSpec((tm, seq), lambda b, j, m: (jnp.minimum(m, nm - 1), 0)),
        pl.BlockSpec((seq, tn), lambda b, j, m: (b, j)),
        pl.BlockSpec((seq, tn), lambda b, j, m: (b, j)),
    ]
    args = [cos, sin, a, bn]
    out_rows = nbatch * seq
    out_row = lambda b, m: b * nm + m
    if ctx_len:
        assert ctx_len == tm
        ctx0 = nbatch * seq // ctx_len
        cosc, sinc = to_bf16(_dft_tables(ctx_len))
        in_specs += [
            pl.BlockSpec((ctx_len, ctx_len), lambda b, j, m: (0, 0)),
            pl.BlockSpec((ctx_len, ctx_len), lambda b, j, m: (0, 0)),
            pl.BlockSpec((ctx_len, tn), lambda b, j, m: (ctx0 + b, j)),
            pl.BlockSpec((ctx_len, tn), lambda b, j, m: (ctx0 + b, j)),
        ]
        args += [cosc, sinc, a, bn]
        out_rows += nbatch * ctx_len
        out_row = lambda b, m: jnp.where(m == nm, ctx0 + b, b * nm + m)
    return pl.pallas_call(
        functools.partial(_posdft_kernel, nm=nm),
        grid=(nbatch, d // tn, nm + (1 if ctx_len else 0)),
        in_specs=in_specs,
        out_specs=pl.BlockSpec((tm, tn), lambda b, j, m: (out_row(b, m), j)),
        out_shape=jax.ShapeDtypeStruct((out_rows, d), BF16),
        compiler_params=_params(("parallel", "parallel", "arbitrary")),
        name="posdft",
    )(*args)


def _split3_dot(mask_bf16, v):
    hi = v.astype(BF16)
    r1 = v - hi.astype(F32)
    mid = r1.astype(BF16)
    lo = (r1 - mid.astype(F32)).astype(BF16)
    out = jnp.dot(mask_bf16, hi, preferred_element_type=F32)
    out += jnp.dot(mask_bf16, mid, preferred_element_type=F32)
    out += jnp.dot(mask_bf16, lo, preferred_element_type=F32)
    return out


def _dot_nt(a, b):
    return lax.dot_general(a, b, (((1,), (1,)), ((), ())), preferred_element_type=F32)


def _dot_tn(a, b):
    return lax.dot_general(a, b, (((0,), (0,)), ((), ())), preferred_element_type=F32)


_GLA_HALF = 128


def _block_sum_masks(block, fwd):
    ri = lax.broadcasted_iota(jnp.int32, (_GLA_HALF, _GLA_HALF), 0)
    ci = lax.broadcasted_iota(jnp.int32, (_GLA_HALF, _GLA_HALF), 1)
    same = (ri // block) == (ci // block)
    ahead = (ri - ci) if fwd else (ci - ri)
    m_incl = jnp.where(same & (ahead >= 0), 1.0, 0.0).astype(BF16)
    m_excl = jnp.where(same & (ahead < 0), 1.0, 0.0).astype(BF16)
    return m_incl, m_excl


def _block_sums(mask, x):
    return jnp.concatenate([_split3_dot(mask, x[r0:r0 + _GLA_HALF])
                            for r0 in range(0, x.shape[0], _GLA_HALF)], axis=0)


def _gla_fast(fwd, q, kk, bc, v_ref, st_ref, o_ref):
    cs = GLA_CHUNK
    nchunk = q.shape[0] // cs
    nhead = st_ref.shape[0]
    ri = lax.broadcasted_iota(jnp.int32, (cs, cs), 0)
    ci = lax.broadcasted_iota(jnp.int32, (cs, cs), 1)
    keep = (ci <= ri) if fwd else (ci >= ri)
    sts = [st_ref[h] for h in range(nhead)]
    for j in range(nchunk):
        c = j if fwd else nchunk - 1 - j
        rows = slice(c * cs, (c + 1) * cs)
        b = bc[rows]
        tot = b[cs - 1:cs] if fwd else b[0:1]
        mid = 0.5 * tot
        qm = (q[rows] * jnp.exp(b - mid)).astype(BF16)
        km = (kk[rows] * jnp.exp(mid - b)).astype(BF16)
        carry_in = jnp.exp(mid)
        carry_out = jnp.exp(tot)
        new_out = jnp.exp(tot - mid)
        for h in range(nhead):
            hc = slice(h * HGRN_DK, (h + 1) * HGRN_DK)
            vb = v_ref[rows, hc]
            a = jnp.where(keep, _dot_nt(qm[:, hc], km[:, hc]), 0.0).astype(BF16)
            o = jnp.dot(a, vb, preferred_element_type=F32)
            o += _dot_nt(qm[:, hc], (sts[h] * carry_in[:, hc]).astype(BF16))
            o_ref[rows, hc] = o
            sts[h] = sts[h] * carry_out[:, hc] + _dot_tn(vb, km[:, hc]) * new_out[:, hc]
    for h in range(nhead):
        st_ref[h] = sts[h]


def _gla_exact(fwd, q, kk, logf, v_ref, st_ref, o_ref):
    seg_rows = q.shape[0]
    nsub = seg_rows // GLA_SUB
    hs = GLA_SUB // 2
    v = v_ref[...].astype(F32)

    m_incl, m_excl = _block_sum_masks(GLA_SUB, fwd)
    bc_all = _block_sums(m_incl, logf)
    qd_all = (q * jnp.exp(bc_all)).astype(BF16)
    kd_all = (kk * jnp.exp(_block_sums(m_excl, logf))).astype(BF16)

    rowh = lax.broadcasted_iota(jnp.int32, (hs, HGRN_DK), 0)
    for h in range(st_ref.shape[0]):
        hc = slice(h * HGRN_DK, (h + 1) * HGRN_DK)
        st = st_ref[h]
        for j in range(nsub):
            i = j if fwd else nsub - 1 - j
            r0 = i * GLA_SUB
            o = _dot_nt(qd_all[r0:r0 + GLA_SUB, hc], st.astype(BF16))
            acc = [o[:hs], o[hs:]]
            for s in range(GLA_SUB):
                sr = r0 + s
                b_s, k_s, v_s = bc_all[sr:sr + 1, hc], kk[sr:sr + 1, hc], v[sr:sr + 1, hc]
                for hh in range(2):
                    lo_row = hh * hs
                    reach_all = (lo_row > s) if fwd else (lo_row + hs - 1 < s)
                    reach_none = (lo_row + hs - 1 < s) if fwd else (lo_row > s)
                    if reach_none:
                        continue
                    rr = slice(r0 + lo_row, r0 + lo_row + hs)
                    diff = bc_all[rr, hc] - b_s
                    if not reach_all:
                        keep = (rowh + lo_row >= s) if fwd else (rowh + lo_row <= s)
                        diff = jnp.where(keep, diff, -1e30)
                    col = jnp.sum(q[rr, hc] * jnp.exp(diff) * k_s, axis=1, keepdims=True)
                    acc[hh] = acc[hh] + col * v_s
            o_ref[r0:r0 + hs, hc] = acc[0]
            o_ref[r0 + hs:r0 + GLA_SUB, hc] = acc[1]
            last = r0 + GLA_SUB - 1 if fwd else r0
            upd = _dot_tn(v_ref[r0:r0 + GLA_SUB, hc], kd_all[r0:r0 + GLA_SUB, hc])
            st = st * jnp.exp(bc_all[last:last + 1, hc]) + upd
        st_ref[h] = st


def _gla_segment(fwd, q_ref, z_ref, v_ref, lb, st_ref, o_ref):
    z = z_ref[...]
    ez = jnp.exp(-jnp.abs(z))
    r = 1.0 / (1.0 + ez)
    pos = z >= 0.0
    sig = jnp.where(pos, r, ez * r)
    nsig = jnp.where(pos, ez * r, r)
    logf = jnp.log(lb + (1.0 - lb) * sig)
    kk = (1.0 - lb) * nsig
    q = q_ref[...].astype(F32)

    bc = _block_sums(_block_sum_masks(GLA_CHUNK, fwd)[0], logf)
    edge = GLA_CHUNK - 1 if fwd else 0
    tot = bc[edge:edge + 1]
    for c in range(1, q.shape[0] // GLA_CHUNK):
        tot = jnp.minimum(tot, bc[c * GLA_CHUNK + edge:c * GLA_CHUNK + edge + 1])
    mild = jnp.max(-tot) <= GLA_FAST_MAX_DECAY

    @pl.when(mild)
    def _():
        _gla_fast(fwd, q, kk, bc, v_ref, st_ref, o_ref)

    @pl.when(jnp.logical_not(mild))
    def _():
        _gla_exact(fwd, q, kk, logf, v_ref, st_ref, o_ref)


def _gla_kernel(q_ref, z_ref, v_ref, g_ref, lbp_ref, gn_ref, y_ref, of_ref, st_ref, o_ref,
                *, layer, nseg):
    t = pl.program_id(2)
    fwd = t < nseg
    seg = t % nseg
    seg_rows = q_ref.shape[0]

    @pl.when(seg == 0)
    def _():
        st_ref[...] = jnp.zeros_like(st_ref)

    p = lbp_ref[...]
    e = jnp.exp(p - jnp.max(p, axis=0, keepdims=True))
    sm = e / jnp.sum(e, axis=0, keepdims=True)
    lb2 = jnp.zeros_like(sm[0])
    for j in range(1, layer + 1):
        lb2 = lb2 + sm[j]

    @pl.when(fwd)
    def _():
        _gla_segment(True, q_ref, z_ref, v_ref, lb2[0:1, :], st_ref, o_ref)
        of_ref[pl.ds(pl.multiple_of(seg * seg_rows, seg_rows), seg_rows), :] = o_ref[...]

    @pl.when(jnp.logical_not(fwd))
    def _():
        _gla_segment(False, q_ref, z_ref, v_ref, lb2[1:2, :], st_ref, o_ref)
        fseg = jnp.where(seg == 0, 0, nseg - seg)
        o = o_ref[...] + of_ref[pl.ds(pl.multiple_of(fseg * seg_rows, seg_rows), seg_rows), :]
        gate = _silu(g_ref[...].astype(F32))
        for h in range(st_ref.shape[0]):
            hc = slice(h * HGRN_DK, (h + 1) * HGRN_DK)
            y_ref[:, hc] = (_rms(o[:, hc], gn_ref[...]) * gate[:, hc]).astype(y_ref.dtype)


def _gla(pq, pz, pvg, hgrn_lb, gnorm, layer, nb, seq, ctx_len):
    t_rows, f = pq.shape
    heads = f // HGRN_DK
    sr = GLA_SEG
    assert ctx_len == sr and seq % sr == 0
    nlat = seq // sr
    nseg = nlat + 1
    lat_blocks = nb * nlat
    depth = hgrn_lb.shape[0]

    def rb(b, t):
        seg = t % nseg
        lat = b * nlat + jnp.where(t < nseg, seg - 1, nlat - seg)
        return jnp.where(seg == 0, lat_blocks + b, lat)

    hp = GLA_HEADS_PER_STEP
    assert heads % hp == 0
    hgroups = heads // hp
    wide = hp * HGRN_DK
    blk = lambda colf: pl.BlockSpec((sr, wide), lambda b, h, t: (rb(b, t), colf(h, t)))
    return pl.pallas_call(
        functools.partial(_gla_kernel, layer=layer, nseg=nseg),
        grid=(nb, hgroups, 2 * nseg),
        in_specs=[
            blk(lambda h, t: h),
            blk(lambda h, t: jnp.where(t < nseg, 0, hgroups) + h),
            blk(lambda h, t: h),
            blk(lambda h, t: hgroups + h),
            pl.BlockSpec((depth, 2, wide), lambda b, h, t: (0, 0, h)),
            pl.BlockSpec((1, HGRN_DK), lambda b, h, t: (0, 0)),
        ],
        out_specs=pl.BlockSpec((sr, wide), lambda b, h, t: (rb(b, jnp.maximum(t, nseg)), h)),
        out_shape=jax.ShapeDtypeStruct((t_rows, f), BF16),
        scratch_shapes=[
            pltpu.VMEM((nseg * sr, wide), F32),
            pltpu.VMEM((hp, HGRN_DK, HGRN_DK), F32),
            pltpu.VMEM((sr, wide), F32),
        ],
        compiler_params=_params(("parallel", "parallel", "arbitrary")),
        name="gla",
    )(pq, pz, pvg, pvg, hgrn_lb, gnorm.reshape(1, HGRN_DK))


def _attn_kernel(q1_ref, q2_ref, k1l_ref, k2l_ref, k1c_ref, k2c_ref, vl_ref, vc_ref,
                 lam_ref, sub_ref, o_ref, *, lambda_init, nq):
    qi = pl.program_id(2)
    lp = lam_ref[...]
    lam = (jnp.exp(jnp.sum(lp[0:1, :] * lp[1:2, :], keepdims=True))
           - jnp.exp(jnp.sum(lp[2:3, :] * lp[3:4, :], keepdims=True)) + lambda_init)

    def scores(q_ref, kc_ref, kl_ref):
        sc = _dot_nt(q_ref[...], kc_ref[...])
        sl = None if kl_ref is None else _dot_nt(q_ref[...], kl_ref[...])
        return sc, sl

    def probs(sc, sl):
        m = jnp.max(sc, axis=-1, keepdims=True)
        if sl is not None:
            m = jnp.maximum(m, jnp.max(sl, axis=-1, keepdims=True))
        pc = jnp.exp2(sc - m)
        den = jnp.sum(pc, axis=-1, keepdims=True)
        pl_ = None
        if sl is not None:
            pl_ = jnp.exp2(sl - m)
            den = den + jnp.sum(pl_, axis=-1, keepdims=True)
            pl_ = pl_.astype(BF16)
        return pc.astype(BF16), pl_, 1.0 / den

    def finish(o):
        y = _rms(o, sub_ref[...]) * (1.0 - lambda_init)
        o_ref[...] = y.astype(o_ref.dtype)

    def pv(pc, pl_):
        o = jnp.dot(pc, vc_ref[...], preferred_element_type=F32)
        if pl_ is not None:
            o += jnp.dot(pl_, vl_ref[...], preferred_element_type=F32)
        return o

    @pl.when(qi < nq)
    def _():
        s1 = scores(q1_ref, k1c_ref, k1l_ref)
        s2 = scores(q2_ref, k2c_ref, k2l_ref)
        p1c, p1l, i1 = probs(*s1)
        p2c, p2l, i2 = probs(*s2)
        o1 = pv(p1c, p1l)
        o2 = pv(p2c, p2l)
        finish(o1 * i1 - o2 * (lam * i2))

    @pl.when(qi == nq)
    def _():
        p1c, _, i1 = probs(*scores(q1_ref, k1c_ref, None))
        p2c, _, i2 = probs(*scores(q2_ref, k2c_ref, None))
        finish(pv(p1c, None) * i1 - pv(p2c, None) * (lam * i2))


def _attn(qkv, lam_p, subln, lambda_init, nb, seq, ctx_len):
    t_rows, d3 = qkv.shape
    d = d3 // 3
    hd = DIFF_HEAD_DIM
    heads = d // (2 * hd)
    tq = ctx_len
    nq = seq // tq
    lat_qblocks = nb * nq
    kcol = d // hd
    vcol = 2 * d // (2 * hd)
    qrow = lambda b, qi: jnp.where(qi == nq, lat_qblocks + b, b * nq + qi)
    q_spec = lambda m: pl.BlockSpec((tq, hd), lambda b, h, qi: (qrow(b, qi), 2 * h + m))
    kl_spec = lambda m: pl.BlockSpec((seq, hd), lambda b, h, qi: (b, kcol + 2 * h + m))
    kc_spec = lambda m: pl.BlockSpec((ctx_len, hd), lambda b, h, qi: (lat_qblocks + b, kcol + 2 * h + m))
    return pl.pallas_call(
        functools.partial(_attn_kernel, lambda_init=lambda_init, nq=nq),
        grid=(nb, heads, nq + 1),
        in_specs=[
            q_spec(0), q_spec(1), kl_spec(0), kl_spec(1), kc_spec(0), kc_spec(1),
            pl.BlockSpec((seq, 2 * hd), lambda b, h, qi: (b, vcol + h)),
            pl.BlockSpec((ctx_len, 2 * hd), lambda b, h, qi: (lat_qblocks + b, vcol + h)),
            pl.BlockSpec((4, hd), lambda b, h, qi: (0, 0)),
            pl.BlockSpec((1, 2 * hd), lambda b, h, qi: (0, 0)),
        ],
        out_specs=pl.BlockSpec((tq, 2 * hd), lambda b, h, qi: (qrow(b, qi), h)),
        out_shape=jax.ShapeDtypeStruct((t_rows, d), BF16),
        compiler_params=_params(("parallel", "parallel", "arbitrary")),
        name="diff_attn",
    )(qkv, qkv, qkv, qkv, qkv, qkv, qkv, qkv, lam_p, subln.reshape(1, 2 * hd))


def _rope_tables(seq, nb, ctx_rows):
    rows = seq // GRID_W
    row = jnp.repeat(jnp.arange(rows, dtype=F32), GRID_W)
    col = jnp.tile(jnp.arange(GRID_W, dtype=F32), rows)
    n_freq = DIFF_HEAD_DIM // 4
    inv = ROPE_BASE ** (-jnp.arange(n_freq, dtype=F32) / n_freq)
    ang = jnp.concatenate([row[:, None] * inv, col[:, None] * inv], axis=-1)
    cos, sin = jnp.cos(ang), jnp.sin(ang)
    cos2 = jnp.concatenate([cos, cos], axis=-1)
    sin2 = jnp.concatenate([-sin, sin], axis=-1)
    cos2 = jnp.concatenate([jnp.tile(cos2, (nb, 1)), jnp.ones((ctx_rows, DIFF_HEAD_DIM), F32)], axis=0)
    sin2 = jnp.concatenate([jnp.tile(sin2, (nb, 1)), jnp.zeros((ctx_rows, DIFF_HEAD_DIM), F32)], axis=0)
    return cos2, sin2


def kernel(x, c, ctx, c_ctx, w_mod, b_mod, norm_g, w_mlp_in, w_mlp_out, fnet_w_out, hgrn_w_in,
           hgrn_lb, hgrn_gnorm, hgrn_w_out, diff_w_qkv, diff_lambda, diff_subln, diff_w_out):
    nb, seq, d = x.shape
    ctx_len = ctx.shape[1]
    depth = w_mod.shape[0]
    lat_rows, ctx_rows = nb * seq, nb * ctx_len
    assert nb + 1 <= MOD_ROWS

    xs = jnp.concatenate([x.reshape(lat_rows, d), ctx.reshape(ctx_rows, d)], axis=0)
    cc = jnp.concatenate([c, c_ctx[None, :], jnp.zeros((MOD_ROWS - nb - 1, d), F32)], axis=0)
    mods = _mods(cc, w_mod, b_mod)

    for i in range(depth):
        mixer, slot = i % N_MIXERS, i // N_MIXERS
        ctx_out = i < depth - 1
        ctx_in = ctx_out or mixer != 0
        assert ctx_out or mixer == 0
        nrows = lat_rows + ctx_rows if ctx_in else lat_rows
        common = dict(nrows=nrows, seq=seq, nb=nb)
        if mixer == 0:
            gd = d // FNET_GROUPS
            cg, sg = _dft_tables(gd)
            wcs = jnp.concatenate([cg, -sg], axis=1).astype(BF16)
            a, bn = _norm_mod(xs, norm_g, mods, i, 0, 0, wcs=wcs, **common)
            y = _posdft(a, bn, nb, seq, ctx_len if ctx_in else 0)
            w_out = fnet_w_out[slot]
        elif mixer == 1:
            h = _norm_mod(xs, norm_g, mods, i, 0, 0, **common)
            w_in = hgrn_w_in[slot]
            f = hgrn_lb.shape[-1]
            pq = _mm(h, w_in, 0, f, BF16, name="hgrn_q")
            pz = _mm(h, w_in, f, 2 * f, F32, name="hgrn_z")
            pvg = _mm(h, w_in, 3 * f, 2 * d, BF16, name="hgrn_vg")
            y = _gla(pq, pz, pvg, hgrn_lb, hgrn_gnorm[slot], i, nb, seq, ctx_len)
            w_out = hgrn_w_out[slot]
        else:
            h = _norm_mod(xs, norm_g, mods, i, 0, 0, **common)
            cos2, sin2 = _rope_tables(seq, nb, ctx_rows)
            rope = (cos2, sin2, 2 * d, d, DIFF_HEAD_DIM ** -0.5 * math.log2(math.e))
            qkv = _mm(h, diff_w_qkv[slot], 0, 3 * d, BF16, rope=rope, name="diff_qkv")
            lambda_init = 0.8 - 0.6 * math.exp(-0.3 * i)
            y = _attn(qkv, diff_lambda[slot], diff_subln[slot], lambda_init, nb, seq, ctx_len)
            w_out = diff_w_out[slot]
        xs = _mm_res(y, w_out, xs, norm_g, mods, i, 1, 2, **common)
        xs = _mlp(xs, w_mlp_in, w_mlp_out, norm_g, mods, i, **common)
    return xs[:lat_rows].reshape(nb, seq, d)
```

```python
import functools
import math

import jax
import jax.numpy as jnp
from jax import lax
from jax.experimental import pallas as pl
from jax.experimental.pallas import tpu as pltpu

F32 = jnp.float32
BF16 = jnp.bfloat16

NORM_EPS = 1e-6
ROPE_BASE = 10000.0
GRID_W = 64
N_MOD = 6
N_MIXERS = 3
FNET_GROUPS = 8
HGRN_DK = 128
DIFF_HEAD_DIM = 128
MOD_ROWS = 8

GLA_SUB = 16
GLA_CHUNK = 64
GLA_HEADS_PER_STEP = 4
ATTN_HEADS_PER_STEP = 2
GLA_FAST_MAX_DECAY = 150.0
GLA_SEG = 256
MLP_OUT_CHUNK = 512
ROW_CHUNK = 32
V7X_VMEM_LIMIT = 56 * 1024 * 1024


def _params(sem, vmem=V7X_VMEM_LIMIT):
    return pltpu.CompilerParams(dimension_semantics=sem, vmem_limit_bytes=vmem)


def _silu(v):
    return v * jax.nn.sigmoid(v)


def _rms(y, gamma):
    return y * lax.rsqrt(jnp.mean(y * y, axis=-1, keepdims=True) + NORM_EPS) * gamma


def _mod_row(ref, tile, tiles_per_batch, n_batch):
    r = jnp.minimum(tile // tiles_per_batch, n_batch)
    return ref[pl.ds(r, 1), :]


def _mods_kernel(c_ref, w_ref, b_ref, o_ref):
    a = _silu(c_ref[...]).astype(BF16)
    o_ref[...] = jnp.dot(a, w_ref[...].astype(BF16), preferred_element_type=F32) + b_ref[...]


def _mods(cc, w_mod, b_mod):
    depth, d, n = w_mod.shape
    tn = 1024
    return pl.pallas_call(
        _mods_kernel,
        grid=(depth, n // tn),
        in_specs=[
            pl.BlockSpec((MOD_ROWS, d), lambda l, j: (0, 0)),
            pl.BlockSpec((None, d, tn), lambda l, j: (l, 0, j)),
            pl.BlockSpec((None, 1, tn), lambda l, j: (l, 0, j)),
        ],
        out_specs=pl.BlockSpec((None, MOD_ROWS, tn), lambda l, j: (l, 0, j)),
        out_shape=jax.ShapeDtypeStruct((depth, MOD_ROWS, n), F32),
        compiler_params=_params(("parallel", "parallel")),
        name="mods",
    )(cc, w_mod, b_mod.reshape(depth, 1, n))


def _norm_mod_kernel(x_ref, g_ref, sh_ref, sc_ref, *rest, tpb, nb, grow, groups):
    i = pl.program_id(0)
    sh = _mod_row(sh_ref, i, tpb, nb)
    gmul = g_ref[grow:grow + 1, :] * (1.0 + _mod_row(sc_ref, i, tpb, nb))
    chunks = [slice(r, r + ROW_CHUNK) for r in range(0, x_ref.shape[0], ROW_CHUNK)]
    hbs = [(_rms(x_ref[rows, :], gmul) + sh).astype(BF16) for rows in chunks]
    if groups == 0:
        (o_ref,) = rest
        for rows, hb in zip(chunks, hbs):
            o_ref[rows, :] = hb
    else:
        wcs_ref, a_ref, b_ref = rest
        hb = jnp.concatenate(hbs, axis=0)
        gd = hb.shape[1] // groups
        for g in range(groups):
            r = jnp.dot(hb[:, g * gd:(g + 1) * gd], wcs_ref[...], preferred_element_type=F32)
            a_ref[:, g * gd:(g + 1) * gd] = r[:, :gd].astype(BF16)
            b_ref[:, g * gd:(g + 1) * gd] = r[:, gd:].astype(BF16)


def _norm_mod(x, norm_g, mods, layer, grow, col, nrows, seq, nb, wcs=None):
    d = x.shape[1]
    tm = 512
    tpb = seq // tm
    groups = 0 if wcs is None else FNET_GROUPS
    in_specs = [
        pl.BlockSpec((tm, d), lambda i: (i, 0)),
        pl.BlockSpec((None, 4, d), lambda i: (layer, 0, 0)),
        pl.BlockSpec((None, MOD_ROWS, d), lambda i: (layer, 0, col)),
        pl.BlockSpec((None, MOD_ROWS, d), lambda i: (layer, 0, col + 1)),
    ]
    args = [x, norm_g, mods, mods]
    out_spec = pl.BlockSpec((tm, d), lambda i: (i, 0))
    out_shape = jax.ShapeDtypeStruct((nrows, d), BF16)
    if wcs is not None:
        in_specs.append(pl.BlockSpec(wcs.shape, lambda i: (0, 0)))
        args.append(wcs)
        out_spec = [out_spec, out_spec]
        out_shape = [out_shape, out_shape]
    return pl.pallas_call(
        functools.partial(_norm_mod_kernel, tpb=tpb, nb=nb, grow=grow, groups=groups),
        grid=(nrows // tm,),
        in_specs=in_specs,
        out_specs=out_spec,
        out_shape=out_shape,
        compiler_params=_params(("parallel",)),
        name="norm_mod",
    )(*args)


def _mm_kernel(a_ref, w_ref, *rest, rope_tiles, q_tiles, q_scale):
    if rope_tiles:
        cos_ref, sin_ref, o_ref, wb_ref = rest
    else:
        o_ref, wb_ref = rest
    j = pl.program_id(0)

    @pl.when(pl.program_id(1) == 0)
    def _():
        wb_ref[...] = w_ref[...].astype(BF16)

    acc = jnp.dot(a_ref[...], wb_ref[...], preferred_element_type=F32)
    if not rope_tiles:
        o_ref[...] = acc.astype(o_ref.dtype)
        return

    @pl.when(j < rope_tiles)
    def _():
        cos = cos_ref[...]
        sin = sin_ref[...]
        scale = jnp.where(j < q_tiles, q_scale, 1.0).astype(F32)
        hd = cos.shape[1]
        for c in range(acc.shape[1] // hd):
            blk = acc[:, c * hd:(c + 1) * hd]
            rot = pltpu.roll(blk, hd // 2, axis=1)
            o_ref[:, c * hd:(c + 1) * hd] = ((blk * cos + rot * sin) * scale).astype(o_ref.dtype)

    @pl.when(j >= rope_tiles)
    def _():
        o_ref[...] = acc.astype(o_ref.dtype)


def _mm(a, w, col0, ncols, out_dtype, rope=None, name="mm"):
    m, k = a.shape
    tm = min(1024, m)
    tn = 1024
    assert m % tm == 0 and ncols % tn == 0 and col0 % tn == 0
    off = col0 // tn
    in_specs = [
        pl.BlockSpec((tm, k), lambda j, i: (i, 0)),
        pl.BlockSpec((k, tn), lambda j, i: (0, j + off)),
    ]
    args = [a, w]
    rope_tiles = q_tiles = 0
    q_scale = 1.0
    if rope is not None:
        cos2, sin2, rope_cols, q_cols, q_scale = rope
        hd = cos2.shape[1]
        in_specs += [pl.BlockSpec((tm, hd), lambda j, i: (i, 0))] * 2
        args += [cos2, sin2]
        rope_tiles, q_tiles = rope_cols // tn, q_cols // tn
    return pl.pallas_call(
        functools.partial(_mm_kernel, rope_tiles=rope_tiles, q_tiles=q_tiles, q_scale=q_scale),
        grid=(ncols // tn, m // tm),
        in_specs=in_specs,
        out_specs=pl.BlockSpec((tm, tn), lambda j, i: (i, j)),
        out_shape=jax.ShapeDtypeStruct((m, ncols), out_dtype),
        scratch_shapes=[pltpu.VMEM((k, tn), BF16)],
        compiler_params=_params(("parallel", "arbitrary")),
        name=name,
    )(*args)


def _mm_res_kernel(a_ref, w_ref, x_ref, g_ref, gate_ref, o_ref, *, tpb, nb, grow):
    y = jnp.dot(a_ref[...], w_ref[...], preferred_element_type=F32)
    gout = g_ref[grow:grow + 1, :] * _mod_row(gate_ref, pl.program_id(0), tpb, nb)
    for r in range(0, y.shape[0], ROW_CHUNK):
        rows = slice(r, r + ROW_CHUNK)
        o_ref[rows, :] = x_ref[rows, :] + _rms(y[rows], gout)


def _mm_res(a, w, x, norm_g, mods, layer, grow, gate_col, nrows, seq, nb):
    kdim, d = w.shape
    tm = 512
    tpb = seq // tm
    return pl.pallas_call(
        functools.partial(_mm_res_kernel, tpb=tpb, nb=nb, grow=grow),
        grid=(nrows // tm,),
        in_specs=[
            pl.BlockSpec((tm, kdim), lambda i: (i, 0)),
            pl.BlockSpec((kdim, d), lambda i: (0, 0)),
            pl.BlockSpec((tm, d), lambda i: (i, 0)),
            pl.BlockSpec((None, 4, d), lambda i: (layer, 0, 0)),
            pl.BlockSpec((None, MOD_ROWS, d), lambda i: (layer, 0, gate_col)),
        ],
        out_specs=pl.BlockSpec((tm, d), lambda i: (i, 0)),
        out_shape=jax.ShapeDtypeStruct((nrows, d), F32),
        compiler_params=_params(("parallel",)),
        name="mm_res",
    )(a, w.astype(BF16), x, norm_g, mods)


def _mlp_kernel(x_ref, g_ref, sh_ref, sc_ref, gate_ref, w1_ref, w2_ref, o_ref, h_ref, *, tpb, nb):
    i = pl.program_id(0)
    k = pl.program_id(1)

    chunks = [slice(r, r + ROW_CHUNK) for r in range(0, x_ref.shape[0], ROW_CHUNK)]

    @pl.when(k == 0)
    def _():
        gmul = g_ref[2:3, :] * (1.0 + _mod_row(sc_ref, i, tpb, nb))
        sh = _mod_row(sh_ref, i, tpb, nb)
        for rows in chunks:
            h_ref[rows, :] = (_rms(x_ref[rows, :], gmul) + sh).astype(BF16)
        o_ref[...] = jnp.zeros_like(o_ref)

    u = jnp.dot(h_ref[...], w1_ref[...].astype(BF16), preferred_element_type=F32)
    u = jnp.square(jnp.maximum(u, 0.0)).astype(BF16)
    cn = MLP_OUT_CHUNK
    for c in range(o_ref.shape[1] // cn):
        cols = slice(c * cn, (c + 1) * cn)
        o_ref[:, cols] += jnp.dot(u, w2_ref[:, cols].astype(BF16), preferred_element_type=F32)

    @pl.when(k == pl.num_programs(1) - 1)
    def _():
        gout = g_ref[3:4, :] * _mod_row(gate_ref, i, tpb, nb)
        for rows in chunks:
            o_ref[rows, :] = x_ref[rows, :] + _rms(o_ref[rows, :], gout)


def _mlp(x, w1, w2, norm_g, mods, layer, nrows, seq, nb):
    d, hid = w1.shape[1], w1.shape[2]
    tm = 1024
    th = 512
    tpb = seq // tm
    mod_spec = lambda col: pl.BlockSpec((None, MOD_ROWS, d), lambda i, k: (layer, 0, col))
    row_spec = pl.BlockSpec((tm, d), lambda i, k: (i, 0), pipeline_mode=pl.Buffered(1))
    return pl.pallas_call(
        functools.partial(_mlp_kernel, tpb=tpb, nb=nb),
        grid=(nrows // tm, hid // th),
        in_specs=[
            pl.BlockSpec((tm, d), lambda i, k: (i, 0)),
            pl.BlockSpec((None, 4, d), lambda i, k: (layer, 0, 0)),
            mod_spec(3), mod_spec(4), mod_spec(5),
            pl.BlockSpec((None, d, th), lambda i, k: (layer, 0, k)),
            pl.BlockSpec((None, th, d), lambda i, k: (layer, k, 0)),
        ],
        out_specs=row_spec,
        out_shape=jax.ShapeDtypeStruct((nrows, d), F32),
        scratch_shapes=[pltpu.VMEM((tm, d), BF16)],
        compiler_params=_params(("parallel", "arbitrary")),
        name="mlp",
    )(x, norm_g, mods, mods, mods, w1, w2)


def _dft_tables(n, w=32):
    assert n % w == 0
    k = jnp.arange(n, dtype=jnp.int32)[:, None]
    unit = 2.0 * math.pi / n
    ang_a = ((k * (w * jnp.arange(n // w, dtype=jnp.int32))[None, :]) % n).astype(F32) * unit
    ang_b = ((k * jnp.arange(w, dtype=jnp.int32)[None, :]) % n).astype(F32) * unit
    ca, sa = jnp.cos(ang_a)[:, :, None], jnp.sin(ang_a)[:, :, None]
    cb, sb = jnp.cos(ang_b)[:, None, :], jnp.sin(ang_b)[:, None, :]
    return (ca * cb - sa * sb).reshape(n, n), (sa * cb + ca * sb).reshape(n, n)


def _posdft_kernel(c_ref, s_ref, a_ref, b_ref, *rest, nm):
    def dft(cos_ref, sin_ref, x_ref, y_ref):
        acc = jnp.dot(cos_ref[...], x_ref[...], preferred_element_type=F32)
        acc += jnp.dot(sin_ref[...], y_ref[...], preferred_element_type=F32)
        return acc

    if len(rest) == 1:
        (o_ref,) = rest
        o_ref[...] = dft(c_ref, s_ref, a_ref, b_ref).astype(o_ref.dtype)
        return
    cc_ref, sc_ref, ac_ref, bc_ref, o_ref = rest
    m = pl.program_id(2)

    @pl.when(m < nm)
    def _():
        o_ref[...] = dft(c_ref, s_ref, a_ref, b_ref).astype(o_ref.dtype)

    @pl.when(m == nm)
    def _():
        o_ref[...] = dft(cc_ref, sc_ref, ac_ref, bc_ref).astype(o_ref.dtype)


def _posdft(a, bn, nbatch, seq, ctx_len):
    d = a.shape[1]
    tm = 256
    tn = 1024
    nm = seq // tm
    to_bf16 = lambda cs: tuple(t.astype(BF16) for t in cs)
    cos, sin = to_bf16(_dft_tables(seq))
    in_specs = [
        pl.BlockSpec((tm, seq), lambda b, j, m: (jnp.minimum(m, nm - 1), 0)),
        pl.BlockSpec((tm, seq), lambda b, j, m: (jnp.minimum(m, nm - 1), 0)),
        pl.BlockSpec((seq, tn), lambda b, j, m: (b, j)),
        pl.BlockSpec((seq, tn), lambda b, j, m: (b, j)),
    ]
    args = [cos, sin, a, bn]
    out_rows = nbatch * seq
    out_row = lambda b, m: b * nm + m
    if ctx_len:
        assert ctx_len == tm
        ctx0 = nbatch * seq // ctx_len
        cosc, sinc = to_bf16(_dft_tables(ctx_len))
        in_specs += [
            pl.BlockSpec((ctx_len, ctx_len), lambda b, j, m: (0, 0)),
            pl.BlockSpec((ctx_len, ctx_len), lambda b, j, m: (0, 0)),
            pl.BlockSpec((ctx_len, tn), lambda b, j, m: (ctx0 + b, j)),
            pl.BlockSpec((ctx_len, tn), lambda b, j, m: (ctx0 + b, j)),
        ]
        args += [cosc, sinc, a, bn]
        out_rows += nbatch * ctx_len
        out_row = lambda b, m: jnp.where(m == nm, ctx0 + b, b * nm + m)
    return pl.pallas_call(
        functools.partial(_posdft_kernel, nm=nm),
        grid=(nbatch, d // tn, nm + (1 if ctx_len else 0)),
        in_specs=in_specs,
        out_specs=pl.BlockSpec((tm, tn), lambda b, j, m: (out_row(b, m), j)),
        out_shape=jax.ShapeDtypeStruct((out_rows, d), BF16),
        compiler_params=_params(("parallel", "parallel", "arbitrary")),
        name="posdft",
    )(*args)


def _split3_dot(mask_bf16, v):
    hi = v.astype(BF16)
    r1 = v - hi.astype(F32)
    mid = r1.astype(BF16)
    lo = (r1 - mid.astype(F32)).astype(BF16)
    out = jnp.dot(mask_bf16, hi, preferred_element_type=F32)
    out += jnp.dot(mask_bf16, mid, preferred_element_type=F32)
    out += jnp.dot(mask_bf16, lo, preferred_element_type=F32)
    return out


def _dot_nt(a, b):
    return lax.dot_general(a, b, (((1,), (1,)), ((), ())), preferred_element_type=F32)


def _dot_tn(a, b):
    return lax.dot_general(a, b, (((0,), (0,)), ((), ())), preferred_element_type=F32)


_GLA_HALF = 128


def _block_sum_masks(block, fwd):
    ri = lax.broadcasted_iota(jnp.int32, (_GLA_HALF, _GLA_HALF), 0)
    ci = lax.broadcasted_iota(jnp.int32, (_GLA_HALF, _GLA_HALF), 1)
    same = (ri // block) == (ci // block)
    ahead = (ri - ci) if fwd else (ci - ri)
    m_incl = jnp.where(same & (ahead >= 0), 1.0, 0.0).astype(BF16)
    m_excl = jnp.where(same & (ahead < 0), 1.0, 0.0).astype(BF16)
    return m_incl, m_excl


def _block_sums(mask, x):
    return jnp.concatenate([_split3_dot(mask, x[r0:r0 + _GLA_HALF])
                            for r0 in range(0, x.shape[0], _GLA_HALF)], axis=0)


def _gla_fast(fwd, q, kk, bc, v_ref, st_ref, o_ref):
    cs = GLA_CHUNK
    nchunk = q.shape[0] // cs
    nhead = st_ref.shape[0]
    ri = lax.broadcasted_iota(jnp.int32, (cs, cs), 0)
    ci = lax.broadcasted_iota(jnp.int32, (cs, cs), 1)
    keep = (ci <= ri) if fwd else (ci >= ri)
    sts = [st_ref[h] for h in range(nhead)]
    for j in range(nchunk):
        c = j if fwd else nchunk - 1 - j
        rows = slice(c * cs, (c + 1) * cs)
        b = bc[rows]
        tot = b[cs - 1:cs] if fwd else b[0:1]
        mid = 0.5 * tot
        qm = (q[rows] * jnp.exp(b - mid)).astype(BF16)
        km = (kk[rows] * jnp.exp(mid - b)).astype(BF16)
        carry_in = jnp.exp(mid)
        carry_out = jnp.exp(tot)
        new_out = jnp.exp(tot - mid)
        for h in range(nhead):
            hc = slice(h * HGRN_DK, (h + 1) * HGRN_DK)
            vb = v_ref[rows, hc]
            a = jnp.where(keep, _dot_nt(qm[:, hc], km[:, hc]), 0.0).astype(BF16)
            o = jnp.dot(a, vb, preferred_element_type=F32)
            o += _dot_nt(qm[:, hc], (sts[h] * carry_in[:, hc]).astype(BF16))
            o_ref[rows, hc] = o
            sts[h] = sts[h] * carry_out[:, hc] + _dot_tn(vb, km[:, hc]) * new_out[:, hc]
    for h in range(nhead):
        st_ref[h] = sts[h]


def _gla_exact(fwd, q, kk, logf, v_ref, st_ref, o_ref):
    seg_rows = q.shape[0]
    nsub = seg_rows // GLA_SUB
    hs = GLA_SUB // 2
    v = v_ref[...].astype(F32)

    m_incl, m_excl = _block_sum_masks(GLA_SUB, fwd)
    bc_all = _block_sums(m_incl, logf)
    qd_all = (q * jnp.exp(bc_all)).astype(BF16)
    kd_all = (kk * jnp.exp(_block_sums(m_excl, logf))).astype(BF16)

    rowh = lax.broadcasted_iota(jnp.int32, (hs, HGRN_DK), 0)
    for h in range(st_ref.shape[0]):
        hc = slice(h * HGRN_DK, (h + 1) * HGRN_DK)
        st = st_ref[h]
        for j in range(nsub):
            i = j if fwd else nsub - 1 - j
            r0 = i * GLA_SUB
            o = _dot_nt(qd_all[r0:r0 + GLA_SUB, hc], st.astype(BF16))
            acc = [o[:hs], o[hs:]]
            for s in range(GLA_SUB):
                sr = r0 + s
                b_s, k_s, v_s = bc_all[sr:sr + 1, hc], kk[sr:sr + 1, hc], v[sr:sr + 1, hc]
                for hh in range(2):
                    lo_row = hh * hs
                    reach_all = (lo_row > s) if fwd else (lo_row + hs - 1 < s)
                    reach_none = (lo_row + hs - 1 < s) if fwd else (lo_row > s)
                    if reach_none:
                        continue
                    rr = slice(r0 + lo_row, r0 + lo_row + hs)
                    diff = bc_all[rr, hc] - b_s
                    if not reach_all:
                        keep = (rowh + lo_row >= s) if fwd else (rowh + lo_row <= s)
                        diff = jnp.where(keep, diff, -1e30)
                    col = jnp.sum(q[rr, hc] * jnp.exp(diff) * k_s, axis=1, keepdims=True)
                    acc[hh] = acc[hh] + col * v_s
            o_ref[r0:r0 + hs, hc] = acc[0]
            o_ref[r0 + hs:r0 + GLA_SUB, hc] = acc[1]
            last = r0 + GLA_SUB - 1 if fwd else r0
            upd = _dot_tn(v_ref[r0:r0 + GLA_SUB, hc], kd_all[r0:r0 + GLA_SUB, hc])
            st = st * jnp.exp(bc_all[last:last + 1, hc]) + upd
        st_ref[h] = st


def _gla_segment(fwd, q_ref, z_ref, v_ref, lb, st_ref, o_ref):
    z = z_ref[...]
    ez = jnp.exp(-jnp.abs(z))
    r = 1.0 / (1.0 + ez)
    pos = z >= 0.0
    sig = jnp.where(pos, r, ez * r)
    nsig = jnp.where(pos, ez * r, r)
    logf = jnp.log(lb + (1.0 - lb) * sig)
    kk = (1.0 - lb) * nsig
    q = q_ref[...].astype(F32)

    bc = _block_sums(_block_sum_masks(GLA_CHUNK, fwd)[0], logf)
    edge = GLA_CHUNK - 1 if fwd else 0
    tot = bc[edge:edge + 1]
    for c in range(1, q.shape[0] // GLA_CHUNK):
        tot = jnp.minimum(tot, bc[c * GLA_CHUNK + edge:c * GLA_CHUNK + edge + 1])
    mild = jnp.max(-tot) <= GLA_FAST_MAX_DECAY

    @pl.when(mild)
    def _():
        _gla_fast(fwd, q, kk, bc, v_ref, st_ref, o_ref)

    @pl.when(jnp.logical_not(mild))
    def _():
        _gla_exact(fwd, q, kk, logf, v_ref, st_ref, o_ref)


def _gla_kernel(q_ref, z_ref, v_ref, g_ref, lbp_ref, gn_ref, y_ref, of_ref, st_ref, o_ref,
                *, layer, nseg):
    t = pl.program_id(2)
    fwd = t < nseg
    seg = t % nseg
    seg_rows = q_ref.shape[0]

    @pl.when(seg == 0)
    def _():
        st_ref[...] = jnp.zeros_like(st_ref)

    p = lbp_ref[...]
    e = jnp.exp(p - jnp.max(p, axis=0, keepdims=True))
    sm = e / jnp.sum(e, axis=0, keepdims=True)
    lb2 = jnp.zeros_like(sm[0])
    for j in range(1, layer + 1):
        lb2 = lb2 + sm[j]

    @pl.when(fwd)
    def _():
        _gla_segment(True, q_ref, z_ref, v_ref, lb2[0:1, :], st_ref, o_ref)
        of_ref[pl.ds(pl.multiple_of(seg * seg_rows, seg_rows), seg_rows), :] = o_ref[...]

    @pl.when(jnp.logical_not(fwd))
    def _():
        _gla_segment(False, q_ref, z_ref, v_ref, lb2[1:2, :], st_ref, o_ref)
        fseg = jnp.where(seg == 0, 0, nseg - seg)
        o = o_ref[...] + of_ref[pl.ds(pl.multiple_of(fseg * seg_rows, seg_rows), seg_rows), :]
        gate = _silu(g_ref[...].astype(F32))
        for h in range(st_ref.shape[0]):
            hc = slice(h * HGRN_DK, (h + 1) * HGRN_DK)
            y_ref[:, hc] = (_rms(o[:, hc], gn_ref[...]) * gate[:, hc]).astype(y_ref.dtype)


def _gla(pq, pz, pvg, hgrn_lb, gnorm, layer, nb, seq, ctx_len):
    t_rows, f = pq.shape
    heads = f // HGRN_DK
    sr = GLA_SEG
    assert ctx_len == sr and seq % sr == 0
    nlat = seq // sr
    nseg = nlat + 1
    lat_blocks = nb * nlat
    depth = hgrn_lb.shape[0]

    def rb(b, t):
        seg = t % nseg
        lat = b * nlat + jnp.where(t < nseg, seg - 1, nlat - seg)
        return jnp.where(seg == 0, lat_blocks + b, lat)

    hp = GLA_HEADS_PER_STEP
    assert heads % hp == 0
    hgroups = heads // hp
    wide = hp * HGRN_DK
    blk = lambda colf: pl.BlockSpec((sr, wide), lambda b, h, t: (rb(b, t), colf(h, t)))
    return pl.pallas_call(
        functools.partial(_gla_kernel, layer=layer, nseg=nseg),
        grid=(nb, hgroups, 2 * nseg),
        in_specs=[
            blk(lambda h, t: h),
            blk(lambda h, t: jnp.where(t < nseg, 0, hgroups) + h),
            blk(lambda h, t: h),
            blk(lambda h, t: hgroups + h),
            pl.BlockSpec((depth, 2, wide), lambda b, h, t: (0, 0, h)),
            pl.BlockSpec((1, HGRN_DK), lambda b, h, t: (0, 0)),
        ],
        out_specs=pl.BlockSpec((sr, wide), lambda b, h, t: (rb(b, jnp.maximum(t, nseg)), h)),
        out_shape=jax.ShapeDtypeStruct((t_rows, f), BF16),
        scratch_shapes=[
            pltpu.VMEM((nseg * sr, wide), F32),
            pltpu.VMEM((hp, HGRN_DK, HGRN_DK), F32),
            pltpu.VMEM((sr, wide), F32),
        ],
        compiler_params=_params(("parallel", "parallel", "arbitrary")),
        name="gla",
    )(pq, pz, pvg, pvg, hgrn_lb, gnorm.reshape(1, HGRN_DK))


def _attn_kernel(q_ref, kl_ref, kc_ref, vl_ref, vc_ref, lam_ref, sub_ref, o_ref, *, lambda_init, nq):
    qi = pl.program_id(2)
    hd = DIFF_HEAD_DIM
    nhead = o_ref.shape[1] // (2 * hd)
    lp = lam_ref[...]
    lam = (jnp.exp(jnp.sum(lp[0:1, :] * lp[1:2, :], keepdims=True))
           - jnp.exp(jnp.sum(lp[2:3, :] * lp[3:4, :], keepdims=True)) + lambda_init)

    def probs(sc, sl):
        m = jnp.max(sc, axis=-1, keepdims=True)
        if sl is not None:
            m = jnp.maximum(m, jnp.max(sl, axis=-1, keepdims=True))
        pc = jnp.exp2(sc - m)
        den = jnp.sum(pc, axis=-1, keepdims=True)
        pl_ = None
        if sl is not None:
            pl_ = jnp.exp2(sl - m)
            den = den + jnp.sum(pl_, axis=-1, keepdims=True)
            pl_ = pl_.astype(BF16)
        return pc.astype(BF16), pl_, 1.0 / den

    def attend(with_latent_keys):
        cols = [slice(c * hd, (c + 1) * hd) for c in range(2 * nhead)]
        sc = [_dot_nt(q_ref[:, c], kc_ref[:, c]) for c in cols]
        sl = [_dot_nt(q_ref[:, c], kl_ref[:, c]) if with_latent_keys else None for c in cols]
        for h in range(nhead):
            vcols = slice(2 * h * hd, 2 * (h + 1) * hd)
            outs = []
            for mp in range(2):
                pc, pl_, inv = probs(sc[2 * h + mp], sl[2 * h + mp])
                o = jnp.dot(pc, vc_ref[:, vcols], preferred_element_type=F32)
                if pl_ is not None:
                    o += jnp.dot(pl_, vl_ref[:, vcols], preferred_element_type=F32)
                outs.append(o * (inv if mp == 0 else lam * inv))
            y = _rms(outs[0] - outs[1], sub_ref[...]) * (1.0 - lambda_init)
            o_ref[:, vcols] = y.astype(o_ref.dtype)

    @pl.when(qi < nq)
    def _():
        attend(True)

    @pl.when(qi == nq)
    def _():
        attend(False)


def _attn(qkv, lam_p, subln, lambda_init, nb, seq, ctx_len):
    t_rows, d3 = qkv.shape
    d = d3 // 3
    wide = ATTN_HEADS_PER_STEP * 2 * DIFF_HEAD_DIM
    assert d % wide == 0
    hgroups = d // wide
    tq = ctx_len
    nq = seq // tq
    lat_qblocks = nb * nq
    qrow = lambda b, qi: jnp.where(qi == nq, lat_qblocks + b, b * nq + qi)
    lat = lambda part: pl.BlockSpec((seq, wide), lambda b, h, qi: (b, part * hgroups + h))
    ctx = lambda part: pl.BlockSpec((ctx_len, wide), lambda b, h, qi: (lat_qblocks + b, part * hgroups + h))
    return pl.pallas_call(
        functools.partial(_attn_kernel, lambda_init=lambda_init, nq=nq),
        grid=(nb, hgroups, nq + 1),
        in_specs=[
            pl.BlockSpec((tq, wide), lambda b, h, qi: (qrow(b, qi), h)),
            lat(1), ctx(1), lat(2), ctx(2),
            pl.BlockSpec((4, DIFF_HEAD_DIM), lambda b, h, qi: (0, 0)),
            pl.BlockSpec((1, 2 * DIFF_HEAD_DIM), lambda b, h, qi: (0, 0)),
        ],
        out_specs=pl.BlockSpec((tq, wide), lambda b, h, qi: (qrow(b, qi), h)),
        out_shape=jax.ShapeDtypeStruct((t_rows, d), BF16),
        compiler_params=_params(("parallel", "parallel", "arbitrary")),
        name="diff_attn",
    )(qkv, qkv, qkv, qkv, qkv, lam_p, subln.reshape(1, 2 * DIFF_HEAD_DIM))


def _rope_tables(seq, nb, ctx_rows):
    rows = seq // GRID_W
    row = jnp.repeat(jnp.arange(rows, dtype=F32), GRID_W)
    col = jnp.tile(jnp.arange(GRID_W, dtype=F32), rows)
    n_freq = DIFF_HEAD_DIM // 4
    inv = ROPE_BASE ** (-jnp.arange(n_freq, dtype=F32) / n_freq)
    ang = jnp.concatenate([row[:, None] * inv, col[:, None] * inv], axis=-1)
    cos, sin = jnp.cos(ang), jnp.sin(ang)
    cos2 = jnp.concatenate([cos, cos], axis=-1)
    sin2 = jnp.concatenate([-sin, sin], axis=-1)
    cos2 = jnp.concatenate([jnp.tile(cos2, (nb, 1)), jnp.ones((ctx_rows, DIFF_HEAD_DIM), F32)], axis=0)
    sin2 = jnp.concatenate([jnp.tile(sin2, (nb, 1)), jnp.zeros((ctx_rows, DIFF_HEAD_DIM), F32)], axis=0)
    return cos2, sin2


def kernel(x, c, ctx, c_ctx, w_mod, b_mod, norm_g, w_mlp_in, w_mlp_out, fnet_w_out, hgrn_w_in,
           hgrn_lb, hgrn_gnorm, hgrn_w_out, diff_w_qkv, diff_lambda, diff_subln, diff_w_out):
    nb, seq, d = x.shape
    ctx_len = ctx.shape[1]
    depth = w_mod.shape[0]
    lat_rows, ctx_rows = nb * seq, nb * ctx_len
    assert nb + 1 <= MOD_ROWS

    xs = jnp.concatenate([x.reshape(lat_rows, d), ctx.reshape(ctx_rows, d)], axis=0)
    cc = jnp.concatenate([c, c_ctx[None, :], jnp.zeros((MOD_ROWS - nb - 1, d), F32)], axis=0)
    mods = _mods(cc, w_mod, b_mod)

    for i in range(depth):
        mixer, slot = i % N_MIXERS, i // N_MIXERS
        ctx_out = i < depth - 1
        ctx_in = ctx_out or mixer != 0
        assert ctx_out or mixer == 0
        nrows = lat_rows + ctx_rows if ctx_in else lat_rows
        common = dict(nrows=nrows, seq=seq, nb=nb)
        if mixer == 0:
            gd = d // FNET_GROUPS
            cg, sg = _dft_tables(gd)
            wcs = jnp.concatenate([cg, -sg], axis=1).astype(BF16)
            a, bn = _norm_mod(xs, norm_g, mods, i, 0, 0, wcs=wcs, **common)
            y = _posdft(a, bn, nb, seq, ctx_len if ctx_in else 0)
            w_out = fnet_w_out[slot]
        elif mixer == 1:
            h = _norm_mod(xs, norm_g, mods, i, 0, 0, **common)
            w_in = hgrn_w_in[slot]
            f = hgrn_lb.shape[-1]
            pq = _mm(h, w_in, 0, f, BF16, name="hgrn_q")
            pz = _mm(h, w_in, f, 2 * f, F32, name="hgrn_z")
            pvg = _mm(h, w_in, 3 * f, 2 * d, BF16, name="hgrn_vg")
            y = _gla(pq, pz, pvg, hgrn_lb, hgrn_gnorm[slot], i, nb, seq, ctx_len)
            w_out = hgrn_w_out[slot]
        else:
            h = _norm_mod(xs, norm_g, mods, i, 0, 0, **common)
            cos2, sin2 = _rope_tables(seq, nb, ctx_rows)
            rope = (cos2, sin2, 2 * d, d, DIFF_HEAD_DIM ** -0.5 * math.log2(math.e))
            qkv = _mm(h, diff_w_qkv[slot], 0, 3 * d, BF16, rope=rope, name="diff_qkv")
            lambda_init = 0.8 - 0.6 * math.exp(-0.3 * i)
            y = _attn(qkv, diff_lambda[slot], diff_subln[slot], lambda_init, nb, seq, ctx_len)
            w_out = diff_w_out[slot]
        xs = _mm_res(y, w_out, xs, norm_g, mods, i, 1, 2, **common)
        xs = _mlp(xs, w_mlp_in, w_mlp_out, norm_g, mods, i, **common)
    return xs[:lat_rows].reshape(nb, seq, d)
```

```python
import functools
import math

import jax
import jax.numpy as jnp
from jax import lax
from jax.experimental import pallas as pl
from jax.experimental.pallas import tpu as pltpu

F32 = jnp.float32
BF16 = jnp.bfloat16

NORM_EPS = 1e-6
ROPE_BASE = 10000.0
GRID_W = 64
N_MOD = 6
N_MIXERS = 3
FNET_GROUPS = 8
HGRN_DK = 128
DIFF_HEAD_DIM = 128
MOD_ROWS = 8

GLA_SUB = 16
GLA_CHUNK = 64
GLA_HEADS_PER_STEP = 8
ATTN_HEADS_PER_STEP = 2
GLA_FAST_MAX_DECAY = 150.0
GLA_SEG = 256
MLP_OUT_CHUNK = 512
ROW_CHUNK = 32
V7X_VMEM_LIMIT = 56 * 1024 * 1024


def _params(sem, vmem=V7X_VMEM_LIMIT):
    return pltpu.CompilerParams(dimension_semantics=sem, vmem_limit_bytes=vmem)


def _silu(v):
    return v * jax.nn.sigmoid(v)


def _rms(y, gamma):
    return y * lax.rsqrt(jnp.mean(y * y, axis=-1, keepdims=True) + NORM_EPS) * gamma


def _mod_row(ref, tile, tiles_per_batch, n_batch):
    r = jnp.minimum(tile // tiles_per_batch, n_batch)
    return ref[pl.ds(r, 1), :]


def _mods_kernel(c_ref, w_ref, b_ref, o_ref):
    a = _silu(c_ref[...]).astype(BF16)
    o_ref[...] = jnp.dot(a, w_ref[...].astype(BF16), preferred_element_type=F32) + b_ref[...]


def _mods(cc, w_mod, b_mod):
    depth, d, n = w_mod.shape
    tn = 1024
    return pl.pallas_call(
        _mods_kernel,
        grid=(depth, n // tn),
        in_specs=[
            pl.BlockSpec((MOD_ROWS, d), lambda l, j: (0, 0)),
            pl.BlockSpec((None, d, tn), lambda l, j: (l, 0, j)),
            pl.BlockSpec((None, 1, tn), lambda l, j: (l, 0, j)),
        ],
        out_specs=pl.BlockSpec((None, MOD_ROWS, tn), lambda l, j: (l, 0, j)),
        out_shape=jax.ShapeDtypeStruct((depth, MOD_ROWS, n), F32),
        compiler_params=_params(("parallel", "parallel")),
        name="mods",
    )(cc, w_mod, b_mod.reshape(depth, 1, n))


def _row_tiled(x, tm):
    if not isinstance(x, tuple):
        return [pl.BlockSpec((tm, x.shape[1]), lambda i: (i, 0))], [x], None
    lat, ctx = x
    nlat = lat.shape[0] // tm
    assert lat.shape[0] % tm == 0 and ctx.shape[0] % tm == 0
    specs = [pl.BlockSpec((tm, lat.shape[1]), lambda i: (jnp.minimum(i, nlat - 1), 0)),
             pl.BlockSpec((tm, ctx.shape[1]), lambda i: (jnp.maximum(i - nlat, 0), 0))]
    return specs, [lat, ctx], nlat


def _tile_rows(refs, nlat, rows):
    if len(refs) == 1:
        return refs[0][rows, :]
    return jnp.where(pl.program_id(0) < nlat, refs[0][rows, :], refs[1][rows, :])


def _norm_mod_kernel(*refs, n_x, nlat, tpb, nb, grow, groups):
    x_refs, (g_ref, sh_ref, sc_ref), rest = refs[:n_x], refs[n_x:n_x + 3], refs[n_x + 3:]
    i = pl.program_id(0)
    sh = _mod_row(sh_ref, i, tpb, nb)
    gmul = g_ref[grow:grow + 1, :] * (1.0 + _mod_row(sc_ref, i, tpb, nb))
    chunks = [slice(r, r + ROW_CHUNK) for r in range(0, x_refs[0].shape[0], ROW_CHUNK)]
    hbs = [(_rms(_tile_rows(x_refs, nlat, rows), gmul) + sh).astype(BF16) for rows in chunks]
    if groups == 0:
        (o_ref,) = rest
        for rows, hb in zip(chunks, hbs):
            o_ref[rows, :] = hb
    else:
        wcs_ref, a_ref, b_ref = rest
        hb = jnp.concatenate(hbs, axis=0)
        gd = hb.shape[1] // groups
        for g in range(groups):
            r = jnp.dot(hb[:, g * gd:(g + 1) * gd], wcs_ref[...], preferred_element_type=F32)
            a_ref[:, g * gd:(g + 1) * gd] = r[:, :gd].astype(BF16)
            b_ref[:, g * gd:(g + 1) * gd] = r[:, gd:].astype(BF16)


def _norm_mod(x, norm_g, mods, layer, grow, col, nrows, seq, nb, wcs=None):
    d = norm_g.shape[-1]
    tm = 512
    tpb = seq // tm
    groups = 0 if wcs is None else FNET_GROUPS
    x_specs, x_args, nlat = _row_tiled(x, tm)
    in_specs = x_specs + [
        pl.BlockSpec((None, 4, d), lambda i: (layer, 0, 0)),
        pl.BlockSpec((None, MOD_ROWS, d), lambda i: (layer, 0, col)),
        pl.BlockSpec((None, MOD_ROWS, d), lambda i: (layer, 0, col + 1)),
    ]
    args = x_args + [norm_g, mods, mods]
    out_spec = pl.BlockSpec((tm, d), lambda i: (i, 0))
    out_shape = jax.ShapeDtypeStruct((nrows, d), BF16)
    if wcs is not None:
        in_specs.append(pl.BlockSpec(wcs.shape, lambda i: (0, 0)))
        args.append(wcs)
        out_spec = [out_spec, out_spec]
        out_shape = [out_shape, out_shape]
    return pl.pallas_call(
        functools.partial(_norm_mod_kernel, n_x=len(x_args), nlat=nlat, tpb=tpb, nb=nb, grow=grow,
                          groups=groups),
        grid=(nrows // tm,),
        in_specs=in_specs,
        out_specs=out_spec,
        out_shape=out_shape,
        compiler_params=_params(("parallel",)),
        name="norm_mod",
    )(*args)


def _mm_kernel(a_ref, w_ref, *rest, rope_tiles, q_tiles, q_scale):
    if rope_tiles:
        cos_ref, sin_ref, o_ref, wb_ref = rest
    else:
        o_ref, wb_ref = rest
    j = pl.program_id(0)

    @pl.when(pl.program_id(1) == 0)
    def _():
        wb_ref[...] = w_ref[...].astype(BF16)

    acc = jnp.dot(a_ref[...], wb_ref[...], preferred_element_type=F32)
    if not rope_tiles:
        o_ref[...] = acc.astype(o_ref.dtype)
        return

    @pl.when(j < rope_tiles)
    def _():
        cos = cos_ref[...]
        sin = sin_ref[...]
        scale = jnp.where(j < q_tiles, q_scale, 1.0).astype(F32)
        hd = cos.shape[1]
        for c in range(acc.shape[1] // hd):
            blk = acc[:, c * hd:(c + 1) * hd]
            rot = pltpu.roll(blk, hd // 2, axis=1)
            o_ref[:, c * hd:(c + 1) * hd] = ((blk * cos + rot * sin) * scale).astype(o_ref.dtype)

    @pl.when(j >= rope_tiles)
    def _():
        o_ref[...] = acc.astype(o_ref.dtype)


def _mm(a, w, col0, ncols, out_dtype, rope=None, name="mm"):
    m, k = a.shape
    tm = min(1024, m)
    tn = 1024
    assert m % tm == 0 and ncols % tn == 0 and col0 % tn == 0
    off = col0 // tn
    in_specs = [
        pl.BlockSpec((tm, k), lambda j, i: (i, 0)),
        pl.BlockSpec((k, tn), lambda j, i: (0, j + off)),
    ]
    args = [a, w]
    rope_tiles = q_tiles = 0
    q_scale = 1.0
    if rope is not None:
        cos2, sin2, rope_cols, q_cols, q_scale = rope
        hd = cos2.shape[1]
        in_specs += [pl.BlockSpec((tm, hd), lambda j, i: (i, 0))] * 2
        args += [cos2, sin2]
        rope_tiles, q_tiles = rope_cols // tn, q_cols // tn
    return pl.pallas_call(
        functools.partial(_mm_kernel, rope_tiles=rope_tiles, q_tiles=q_tiles, q_scale=q_scale),
        grid=(ncols // tn, m // tm),
        in_specs=in_specs,
        out_specs=pl.BlockSpec((tm, tn), lambda j, i: (i, j)),
        out_shape=jax.ShapeDtypeStruct((m, ncols), out_dtype),
        scratch_shapes=[pltpu.VMEM((k, tn), BF16)],
        compiler_params=_params(("parallel", "arbitrary")),
        name=name,
    )(*args)


def _mm_res_kernel(*refs, n_a, nlat_a, n_x, nlat_x, tpb, nb, grow):
    a_refs, w_ref, x_refs = refs[:n_a], refs[n_a], refs[n_a + 1:n_a + 1 + n_x]
    g_ref, gate_ref, o_ref = refs[n_a + 1 + n_x:]
    a = _tile_rows(a_refs, nlat_a, slice(None))
    y = jnp.dot(a, w_ref[...], preferred_element_type=F32)
    gout = g_ref[grow:grow + 1, :] * _mod_row(gate_ref, pl.program_id(0), tpb, nb)
    for r in range(0, y.shape[0], ROW_CHUNK):
        rows = slice(r, r + ROW_CHUNK)
        o_ref[rows, :] = _tile_rows(x_refs, nlat_x, rows) + _rms(y[rows], gout)


def _mm_res(a, w, x, norm_g, mods, layer, grow, gate_col, nrows, seq, nb):
    kdim, d = w.shape
    tm = 512
    tpb = seq // tm
    a_specs, a_args, nlat_a = _row_tiled(a, tm)
    x_specs, x_args, nlat_x = _row_tiled(x, tm)
    return pl.pallas_call(
        functools.partial(_mm_res_kernel, n_a=len(a_args), nlat_a=nlat_a, n_x=len(x_args),
                          nlat_x=nlat_x, tpb=tpb, nb=nb, grow=grow),
        grid=(nrows // tm,),
        in_specs=a_specs + [pl.BlockSpec((kdim, d), lambda i: (0, 0))] + x_specs + [
            pl.BlockSpec((None, 4, d), lambda i: (layer, 0, 0)),
            pl.BlockSpec((None, MOD_ROWS, d), lambda i: (layer, 0, gate_col)),
        ],
        out_specs=pl.BlockSpec((tm, d), lambda i: (i, 0)),
        out_shape=jax.ShapeDtypeStruct((nrows, d), F32),
        compiler_params=_params(("parallel",)),
        name="mm_res",
    )(*a_args, w.astype(BF16), *x_args, norm_g, mods)


def _mlp_kernel(x_ref, g_ref, sh_ref, sc_ref, gate_ref, w1_ref, w2_ref, o_ref, h_ref, *, tpb, nb):
    i = pl.program_id(0)
    k = pl.program_id(1)

    chunks = [slice(r, r + ROW_CHUNK) for r in range(0, x_ref.shape[0], ROW_CHUNK)]

    @pl.when(k == 0)
    def _():
        gmul = g_ref[2:3, :] * (1.0 + _mod_row(sc_ref, i, tpb, nb))
        sh = _mod_row(sh_ref, i, tpb, nb)
        for rows in chunks:
            h_ref[rows, :] = (_rms(x_ref[rows, :], gmul) + sh).astype(BF16)
        o_ref[...] = jnp.zeros_like(o_ref)

    u = jnp.dot(h_ref[...], w1_ref[...].astype(BF16), preferred_element_type=F32)
    u = jnp.square(jnp.maximum(u, 0.0)).astype(BF16)
    cn = MLP_OUT_CHUNK
    for c in range(o_ref.shape[1] // cn):
        cols = slice(c * cn, (c + 1) * cn)
        o_ref[:, cols] += jnp.dot(u, w2_ref[:, cols].astype(BF16), preferred_element_type=F32)

    @pl.when(k == pl.num_programs(1) - 1)
    def _():
        gout = g_ref[3:4, :] * _mod_row(gate_ref, i, tpb, nb)
        for rows in chunks:
            o_ref[rows, :] = x_ref[rows, :] + _rms(o_ref[rows, :], gout)


def _mlp(x, w1, w2, norm_g, mods, layer, nrows, seq, nb):
    d, hid = w1.shape[1], w1.shape[2]
    tm = 1024
    th = 512
    tpb = seq // tm
    mod_spec = lambda col: pl.BlockSpec((None, MOD_ROWS, d), lambda i, k: (layer, 0, col))
    row_spec = pl.BlockSpec((tm, d), lambda i, k: (i, 0), pipeline_mode=pl.Buffered(1))
    return pl.pallas_call(
        functools.partial(_mlp_kernel, tpb=tpb, nb=nb),
        grid=(nrows // tm, hid // th),
        in_specs=[
            pl.BlockSpec((tm, d), lambda i, k: (i, 0)),
            pl.BlockSpec((None, 4, d), lambda i, k: (layer, 0, 0)),
            mod_spec(3), mod_spec(4), mod_spec(5),
            pl.BlockSpec((None, d, th), lambda i, k: (layer, 0, k)),
            pl.BlockSpec((None, th, d), lambda i, k: (layer, k, 0)),
        ],
        out_specs=row_spec,
        out_shape=jax.ShapeDtypeStruct((nrows, d), F32),
        scratch_shapes=[pltpu.VMEM((tm, d), BF16)],
        compiler_params=_params(("parallel", "arbitrary")),
        name="mlp",
    )(x, norm_g, mods, mods, mods, w1, w2)


def _dft_tables(n, w=32):
    assert n % w == 0
    k = jnp.arange(n, dtype=jnp.int32)[:, None]
    unit = 2.0 * math.pi / n
    ang_a = ((k * (w * jnp.arange(n // w, dtype=jnp.int32))[None, :]) % n).astype(F32) * unit
    ang_b = ((k * jnp.arange(w, dtype=jnp.int32)[None, :]) % n).astype(F32) * unit
    ca, sa = jnp.cos(ang_a)[:, :, None], jnp.sin(ang_a)[:, :, None]
    cb, sb = jnp.cos(ang_b)[:, None, :], jnp.sin(ang_b)[:, None, :]
    return (ca * cb - sa * sb).reshape(n, n), (sa * cb + ca * sb).reshape(n, n)


def _posdft_kernel(ce_ref, se_ref, co_ref, so_ref, ae_ref, be_ref, ao_ref, bo_ref, o_ref):
    even = jnp.dot(ce_ref[...], ae_ref[...], preferred_element_type=F32)
    even += jnp.dot(se_ref[...], be_ref[...], preferred_element_type=F32)
    odd = jnp.dot(co_ref[...], ao_ref[...], preferred_element_type=F32)
    odd += jnp.dot(so_ref[...], bo_ref[...], preferred_element_type=F32)
    o_ref[0] = (even + odd).astype(o_ref.dtype)
    o_ref[1] = (even - odd).astype(o_ref.dtype)


def _posdft(a2, bn2, nbatch, seq, seq0):
    d = a2.shape[1] // 2
    half = seq // 2
    tm = min(256, half)
    tn = 1024
    cos, sin = _dft_tables(seq)
    tables = [t[:half, par::2].astype(BF16) for par in (0, 1) for t in (cos, sin)]
    nj = d // tn
    tab_spec = pl.BlockSpec((tm, half), lambda b, j, m: (m, 0))
    src = lambda par: pl.BlockSpec((half, tn), lambda b, j, m: (seq0 + b, par * nj + j))
    y = pl.pallas_call(
        _posdft_kernel,
        grid=(nbatch, nj, half // tm),
        in_specs=[tab_spec] * 4 + [src(0), src(0), src(1), src(1)],
        out_specs=pl.BlockSpec((None, 2, tm, tn), lambda b, j, m: (b, 0, m, j)),
        out_shape=jax.ShapeDtypeStruct((nbatch, 2, half, d), BF16),
        compiler_params=_params(("parallel", "parallel", "arbitrary")),
        name="posdft",
    )(*tables, a2, bn2, a2, bn2)
    return y.reshape(nbatch * seq, d)


def _split3_dot(mask_bf16, v):
    hi = v.astype(BF16)
    r1 = v - hi.astype(F32)
    mid = r1.astype(BF16)
    lo = (r1 - mid.astype(F32)).astype(BF16)
    out = jnp.dot(mask_bf16, hi, preferred_element_type=F32)
    out += jnp.dot(mask_bf16, mid, preferred_element_type=F32)
    out += jnp.dot(mask_bf16, lo, preferred_element_type=F32)
    return out


def _dot_nt(a, b):
    return lax.dot_general(a, b, (((1,), (1,)), ((), ())), preferred_element_type=F32)


def _dot_tn(a, b):
    return lax.dot_general(a, b, (((0,), (0,)), ((), ())), preferred_element_type=F32)


_GLA_HALF = 128


def _block_sum_masks(block, fwd):
    ri = lax.broadcasted_iota(jnp.int32, (_GLA_HALF, _GLA_HALF), 0)
    ci = lax.broadcasted_iota(jnp.int32, (_GLA_HALF, _GLA_HALF), 1)
    same = (ri // block) == (ci // block)
    ahead = (ri - ci) if fwd else (ci - ri)
    m_incl = jnp.where(same & (ahead >= 0), 1.0, 0.0).astype(BF16)
    m_excl = jnp.where(same & (ahead < 0), 1.0, 0.0).astype(BF16)
    return m_incl, m_excl


def _block_sums(mask, x):
    return jnp.concatenate([_split3_dot(mask, x[r0:r0 + _GLA_HALF])
                            for r0 in range(0, x.shape[0], _GLA_HALF)], axis=0)


def _gla_fast(fwd, q, kk, bc, v_ref, st_ref, o_ref):
    cs = GLA_CHUNK
    nchunk = q.shape[0] // cs
    nhead = st_ref.shape[0]
    ri = lax.broadcasted_iota(jnp.int32, (cs, cs), 0)
    ci = lax.broadcasted_iota(jnp.int32, (cs, cs), 1)
    keep = (ci <= ri) if fwd else (ci >= ri)
    sts = [st_ref[h] for h in range(nhead)]
    for j in range(nchunk):
        c = j if fwd else nchunk - 1 - j
        rows = slice(c * cs, (c + 1) * cs)
        b = bc[rows]
        tot = b[cs - 1:cs] if fwd else b[0:1]
        mid = 0.5 * tot
        qm = (q[rows] * jnp.exp(b - mid)).astype(BF16)
        km = (kk[rows] * jnp.exp(mid - b)).astype(BF16)
        carry_in = jnp.exp(mid)
        carry_out = jnp.exp(tot)
        new_out = jnp.exp(tot - mid)
        for h in range(nhead):
            hc = slice(h * HGRN_DK, (h + 1) * HGRN_DK)
            vb = v_ref[rows, hc]
            a = jnp.where(keep, _dot_nt(qm[:, hc], km[:, hc]), 0.0).astype(BF16)
            o = jnp.dot(a, vb, preferred_element_type=F32)
            o += _dot_nt(qm[:, hc], (sts[h] * carry_in[:, hc]).astype(BF16))
            o_ref[rows, hc] = o
            sts[h] = sts[h] * carry_out[:, hc] + _dot_tn(vb, km[:, hc]) * new_out[:, hc]
    for h in range(nhead):
        st_ref[h] = sts[h]


def _gla_exact(fwd, q, kk, logf, v_ref, st_ref, o_ref):
    seg_rows = q.shape[0]
    nsub = seg_rows // GLA_SUB
    hs = GLA_SUB // 2
    v = v_ref[...].astype(F32)

    m_incl, m_excl = _block_sum_masks(GLA_SUB, fwd)
    bc_all = _block_sums(m_incl, logf)
    qd_all = (q * jnp.exp(bc_all)).astype(BF16)
    kd_all = (kk * jnp.exp(_block_sums(m_excl, logf))).astype(BF16)

    rowh = lax.broadcasted_iota(jnp.int32, (hs, HGRN_DK), 0)
    for h in range(st_ref.shape[0]):
        hc = slice(h * HGRN_DK, (h + 1) * HGRN_DK)
        st = st_ref[h]
        for j in range(nsub):
            i = j if fwd else nsub - 1 - j
            r0 = i * GLA_SUB
            o = _dot_nt(qd_all[r0:r0 + GLA_SUB, hc], st.astype(BF16))
            acc = [o[:hs], o[hs:]]
            for s in range(GLA_SUB):
                sr = r0 + s
                b_s, k_s, v_s = bc_all[sr:sr + 1, hc], kk[sr:sr + 1, hc], v[sr:sr + 1, hc]
                for hh in range(2):
                    lo_row = hh * hs
                    reach_all = (lo_row > s) if fwd else (lo_row + hs - 1 < s)
                    reach_none = (lo_row + hs - 1 < s) if fwd else (lo_row > s)
                    if reach_none:
                        continue
                    rr = slice(r0 + lo_row, r0 + lo_row + hs)
                    diff = bc_all[rr, hc] - b_s
                    if not reach_all:
                        keep = (rowh + lo_row >= s) if fwd else (rowh + lo_row <= s)
                        diff = jnp.where(keep, diff, -1e30)
                    col = jnp.sum(q[rr, hc] * jnp.exp(diff) * k_s, axis=1, keepdims=True)
                    acc[hh] = acc[hh] + col * v_s
            o_ref[r0:r0 + hs, hc] = acc[0]
            o_ref[r0 + hs:r0 + GLA_SUB, hc] = acc[1]
            last = r0 + GLA_SUB - 1 if fwd else r0
            upd = _dot_tn(v_ref[r0:r0 + GLA_SUB, hc], kd_all[r0:r0 + GLA_SUB, hc])
            st = st * jnp.exp(bc_all[last:last + 1, hc]) + upd
        st_ref[h] = st


def _gla_segment(fwd, q_ref, z_ref, v_ref, lb, st_ref, o_ref):
    z = z_ref[...]
    ez = jnp.exp(-jnp.abs(z))
    r = 1.0 / (1.0 + ez)
    pos = z >= 0.0
    sig = jnp.where(pos, r, ez * r)
    nsig = jnp.where(pos, ez * r, r)
    logf = jnp.log(lb + (1.0 - lb) * sig)
    kk = (1.0 - lb) * nsig
    q = q_ref[...].astype(F32)

    bc = _block_sums(_block_sum_masks(GLA_CHUNK, fwd)[0], logf)
    edge = GLA_CHUNK - 1 if fwd else 0
    tot = bc[edge:edge + 1]
    for c in range(1, q.shape[0] // GLA_CHUNK):
        tot = jnp.minimum(tot, bc[c * GLA_CHUNK + edge:c * GLA_CHUNK + edge + 1])
    mild = jnp.max(-tot) <= GLA_FAST_MAX_DECAY

    @pl.when(mild)
    def _():
        _gla_fast(fwd, q, kk, bc, v_ref, st_ref, o_ref)

    @pl.when(jnp.logical_not(mild))
    def _():
        _gla_exact(fwd, q, kk, logf, v_ref, st_ref, o_ref)


def _gla_kernel(q_ref, z_ref, v_ref, g_ref, lbp_ref, gn_ref, y_ref, of_ref, st_ref, o_ref,
                *, layer, nseg):
    t = pl.program_id(2)
    fwd = t < nseg
    seg = t % nseg
    seg_rows = q_ref.shape[0]

    @pl.when(seg == 0)
    def _():
        st_ref[...] = jnp.zeros_like(st_ref)

    p = lbp_ref[...]
    e = jnp.exp(p - jnp.max(p, axis=0, keepdims=True))
    sm = e / jnp.sum(e, axis=0, keepdims=True)
    lb2 = jnp.zeros_like(sm[0])
    for j in range(1, layer + 1):
        lb2 = lb2 + sm[j]

    @pl.when(fwd)
    def _():
        _gla_segment(True, q_ref, z_ref, v_ref, lb2[0:1, :], st_ref, o_ref)
        of_ref[pl.ds(pl.multiple_of(seg * seg_rows, seg_rows), seg_rows), :] = o_ref[...]

    @pl.when(jnp.logical_not(fwd))
    def _():
        _gla_segment(False, q_ref, z_ref, v_ref, lb2[1:2, :], st_ref, o_ref)
        fseg = jnp.where(seg == 0, 0, nseg - seg)
        o = o_ref[...] + of_ref[pl.ds(pl.multiple_of(fseg * seg_rows, seg_rows), seg_rows), :]
        gate = _silu(g_ref[...].astype(F32))
        for h in range(st_ref.shape[0]):
            hc = slice(h * HGRN_DK, (h + 1) * HGRN_DK)
            y_ref[:, hc] = (_rms(o[:, hc], gn_ref[...]) * gate[:, hc]).astype(y_ref.dtype)


def _gla(pq, pz, pvg, hgrn_lb, gnorm, layer, nb, seq, ctx_len):
    t_rows, f = pq.shape
    heads = f // HGRN_DK
    sr = GLA_SEG
    assert ctx_len == sr and seq % sr == 0
    nlat = seq // sr
    nseg = nlat + 1
    lat_blocks = nb * nlat
    depth = hgrn_lb.shape[0]

    def rb(b, t):
        seg = t % nseg
        lat = b * nlat + jnp.where(t < nseg, seg - 1, nlat - seg)
        return jnp.where(seg == 0, lat_blocks + b, lat)

    hp = GLA_HEADS_PER_STEP
    assert heads % hp == 0
    hgroups = heads // hp
    wide = hp * HGRN_DK
    blk = lambda colf: pl.BlockSpec((sr, wide), lambda b, h, t: (rb(b, t), colf(h, t)))
    return pl.pallas_call(
        functools.partial(_gla_kernel, layer=layer, nseg=nseg),
        grid=(nb, hgroups, 2 * nseg),
        in_specs=[
            blk(lambda h, t: h),
            blk(lambda h, t: jnp.where(t < nseg, 0, hgroups) + h),
            blk(lambda h, t: h),
            blk(lambda h, t: hgroups + h),
            pl.BlockSpec((depth, 2, wide), lambda b, h, t: (0, 0, h)),
            pl.BlockSpec((1, HGRN_DK), lambda b, h, t: (0, 0)),
        ],
        out_specs=pl.BlockSpec((sr, wide), lambda b, h, t: (rb(b, jnp.maximum(t, nseg)), h)),
        out_shape=jax.ShapeDtypeStruct((t_rows, f), BF16),
        scratch_shapes=[
            pltpu.VMEM((nseg * sr, wide), F32),
            pltpu.VMEM((hp, HGRN_DK, HGRN_DK), F32),
            pltpu.VMEM((sr, wide), F32),
        ],
        compiler_params=_params(("parallel", "parallel", "arbitrary")),
        name="gla",
    )(pq, pz, pvg, pvg, hgrn_lb, gnorm.reshape(1, HGRN_DK))


def _attn_kernel(q_ref, kl_ref, kc_ref, vl_ref, vc_ref, lam_ref, sub_ref, o_ref, *, lambda_init, nq):
    qi = pl.program_id(2)
    hd = DIFF_HEAD_DIM
    nhead = o_ref.shape[1] // (2 * hd)
    lp = lam_ref[...]
    lam = (jnp.exp(jnp.sum(lp[0:1, :] * lp[1:2, :], keepdims=True))
           - jnp.exp(jnp.sum(lp[2:3, :] * lp[3:4, :], keepdims=True)) + lambda_init)

    def probs(sc, sl):
        m = jnp.max(sc, axis=-1, keepdims=True)
        if sl is not None:
            m = jnp.maximum(m, jnp.max(sl, axis=-1, keepdims=True))
        pc = jnp.exp2(sc - m)
        den = jnp.sum(pc, axis=-1, keepdims=True)
        pl_ = None
        if sl is not None:
            pl_ = jnp.exp2(sl - m)
            den = den + jnp.sum(pl_, axis=-1, keepdims=True)
            pl_ = pl_.astype(BF16)
        return pc.astype(BF16), pl_, 1.0 / den

    def attend(with_latent_keys):
        cols = [slice(c * hd, (c + 1) * hd) for c in range(2 * nhead)]
        sc = [_dot_nt(q_ref[:, c], kc_ref[:, c]) for c in cols]
        sl = [_dot_nt(q_ref[:, c], kl_ref[:, c]) if with_latent_keys else None for c in cols]
        for h in range(nhead):
            vcols = slice(2 * h * hd, 2 * (h + 1) * hd)
            outs = []
            for mp in range(2):
                pc, pl_, inv = probs(sc[2 * h + mp], sl[2 * h + mp])
                o = jnp.dot(pc, vc_ref[:, vcols], preferred_element_type=F32)
                if pl_ is not None:
                    o += jnp.dot(pl_, vl_ref[:, vcols], preferred_element_type=F32)
                outs.append(o * (inv if mp == 0 else lam * inv))
            y = _rms(outs[0] - outs[1], sub_ref[...]) * (1.0 - lambda_init)
            o_ref[:, vcols] = y.astype(o_ref.dtype)

    @pl.when(qi < nq)
    def _():
        attend(True)

    @pl.when(qi == nq)
    def _():
        attend(False)


def _attn(qkv, lam_p, subln, lambda_init, nb, seq, ctx_len):
    t_rows, d3 = qkv.shape
    d = d3 // 3
    wide = ATTN_HEADS_PER_STEP * 2 * DIFF_HEAD_DIM
    assert d % wide == 0
    hgroups = d // wide
    tq = ctx_len
    nq = seq // tq
    lat_qblocks = nb * nq
    qrow = lambda b, qi: jnp.where(qi == nq, lat_qblocks + b, b * nq + qi)
    lat = lambda part: pl.BlockSpec((seq, wide), lambda b, h, qi: (b, part * hgroups + h))
    ctx = lambda part: pl.BlockSpec((ctx_len, wide), lambda b, h, qi: (lat_qblocks + b, part * hgroups + h))
    return pl.pallas_call(
        functools.partial(_attn_kernel, lambda_init=lambda_init, nq=nq),
        grid=(nb, hgroups, nq + 1),
        in_specs=[
            pl.BlockSpec((tq, wide), lambda b, h, qi: (qrow(b, qi), h)),
            lat(1), ctx(1), lat(2), ctx(2),
            pl.BlockSpec((4, DIFF_HEAD_DIM), lambda b, h, qi: (0, 0)),
            pl.BlockSpec((1, 2 * DIFF_HEAD_DIM), lambda b, h, qi: (0, 0)),
        ],
        out_specs=pl.BlockSpec((tq, wide), lambda b, h, qi: (qrow(b, qi), h)),
        out_shape=jax.ShapeDtypeStruct((t_rows, d), BF16),
        compiler_params=_params(("parallel", "parallel", "arbitrary")),
        name="diff_attn",
    )(qkv, qkv, qkv, qkv, qkv, lam_p, subln.reshape(1, 2 * DIFF_HEAD_DIM))


def _rope_tables(seq, nb, ctx_rows):
    rows = seq // GRID_W
    row = jnp.repeat(jnp.arange(rows, dtype=F32), GRID_W)
    col = jnp.tile(jnp.arange(GRID_W, dtype=F32), rows)
    n_freq = DIFF_HEAD_DIM // 4
    inv = ROPE_BASE ** (-jnp.arange(n_freq, dtype=F32) / n_freq)
    ang = jnp.concatenate([row[:, None] * inv, col[:, None] * inv], axis=-1)
    cos, sin = jnp.cos(ang), jnp.sin(ang)
    cos2 = jnp.concatenate([cos, cos], axis=-1)
    sin2 = jnp.concatenate([-sin, sin], axis=-1)
    cos2 = jnp.concatenate([jnp.tile(cos2, (nb, 1)), jnp.ones((ctx_rows, DIFF_HEAD_DIM), F32)], axis=0)
    sin2 = jnp.concatenate([jnp.tile(sin2, (nb, 1)), jnp.zeros((ctx_rows, DIFF_HEAD_DIM), F32)], axis=0)
    return cos2, sin2


def kernel(x, c, ctx, c_ctx, w_mod, b_mod, norm_g, w_mlp_in, w_mlp_out, fnet_w_out, hgrn_w_in,
           hgrn_lb, hgrn_gnorm, hgrn_w_out, diff_w_qkv, diff_lambda, diff_subln, diff_w_out):
    nb, seq, d = x.shape
    ctx_len = ctx.shape[1]
    depth = w_mod.shape[0]
    lat_rows, ctx_rows = nb * seq, nb * ctx_len
    assert nb + 1 <= MOD_ROWS

    xs = (x.reshape(lat_rows, d), ctx.reshape(ctx_rows, d))
    cc = jnp.concatenate([c, c_ctx[None, :], jnp.zeros((MOD_ROWS - nb - 1, d), F32)], axis=0)
    mods = _mods(cc, w_mod, b_mod)

    for i in range(depth):
        mixer, slot = i % N_MIXERS, i // N_MIXERS
        ctx_out = i < depth - 1
        ctx_in = ctx_out or mixer != 0
        assert ctx_out or mixer == 0
        nrows = lat_rows + ctx_rows if ctx_in else lat_rows
        common = dict(nrows=nrows, seq=seq, nb=nb)
        if mixer == 0:
            gd = d // FNET_GROUPS
            cg, sg = _dft_tables(gd)
            wcs = jnp.concatenate([cg, -sg], axis=1).astype(BF16)
            a, bn = _norm_mod(xs, norm_g, mods, i, 0, 0, wcs=wcs, **common)
            a2, bn2 = a.reshape(nrows // 2, 2 * d), bn.reshape(nrows // 2, 2 * d)
            y = _posdft(a2, bn2, nb, seq, 0)
            if ctx_in:
                y = (y, _posdft(a2, bn2, nb, ctx_len, lat_rows // ctx_len))
            w_out = fnet_w_out[slot]
        elif mixer == 1:
            h = _norm_mod(xs, norm_g, mods, i, 0, 0, **common)
            w_in = hgrn_w_in[slot]
            f = hgrn_lb.shape[-1]
            pq = _mm(h, w_in, 0, f, BF16, name="hgrn_q")
            pz = _mm(h, w_in, f, 2 * f, F32, name="hgrn_z")
            pvg = _mm(h, w_in, 3 * f, 2 * d, BF16, name="hgrn_vg")
            y = _gla(pq, pz, pvg, hgrn_lb, hgrn_gnorm[slot], i, nb, seq, ctx_len)
            w_out = hgrn_w_out[slot]
        else:
            h = _norm_mod(xs, norm_g, mods, i, 0, 0, **common)
            cos2, sin2 = _rope_tables(seq, nb, ctx_rows)
            rope = (cos2, sin2, 2 * d, d, DIFF_HEAD_DIM ** -0.5 * math.log2(math.e))
            qkv = _mm(h, diff_w_qkv[slot], 0, 3 * d, BF16, rope=rope, name="diff_qkv")
            lambda_init = 0.8 - 0.6 * math.exp(-0.3 * i)
            y = _attn(qkv, diff_lambda[slot], diff_subln[slot], lambda_init, nb, seq, ctx_len)
            w_out = diff_w_out[slot]
        xs = _mm_res(y, w_out, xs, norm_g, mods, i, 1, 2, **common)
        xs = _mlp(xs, w_mlp_in, w_mlp_out, norm_g, mods, i, **common)
    return xs[:lat_rows].reshape(nb, seq, d)
```

```python
import functools
import math

import jax
import jax.numpy as jnp
from jax import lax
from jax.experimental import pallas as pl
from jax.experimental.pallas import tpu as pltpu

F32 = jnp.float32
BF16 = jnp.bfloat16

NORM_EPS = 1e-6
ROPE_BASE = 10000.0
GRID_W = 64
N_MOD = 6
N_MIXERS = 3
FNET_GROUPS = 8
HGRN_DK = 128
DIFF_HEAD_DIM = 128
MOD_ROWS = 8

GLA_SUB = 16
GLA_CHUNK = 64
GLA_HEADS_PER_STEP = 8
ATTN_HEADS_PER_STEP = 2
GLA_FAST_MAX_DECAY = 150.0
GLA_SEG = 256
MLP_OUT_CHUNK = 512
ROW_CHUNK = 32
V7X_VMEM_LIMIT = 56 * 1024 * 1024


def _params(sem, vmem=V7X_VMEM_LIMIT):
    return pltpu.CompilerParams(dimension_semantics=sem, vmem_limit_bytes=vmem)


def _silu(v):
    return v * jax.nn.sigmoid(v)


def _rms(y, gamma):
    return y * lax.rsqrt(jnp.mean(y * y, axis=-1, keepdims=True) + NORM_EPS) * gamma


def _mod_row(ref, tile, tiles_per_batch, n_batch):
    r = jnp.minimum(tile // tiles_per_batch, n_batch)
    return ref[pl.ds(r, 1), :]


def _mods_kernel(c_ref, w_ref, b_ref, o_ref):
    a = _silu(c_ref[...]).astype(BF16)
    o_ref[...] = jnp.dot(a, w_ref[...].astype(BF16), preferred_element_type=F32) + b_ref[...]


def _mods(cc, w_mod, b_mod):
    depth, d, n = w_mod.shape
    tn = 1024
    return pl.pallas_call(
        _mods_kernel,
        grid=(depth, n // tn),
        in_specs=[
            pl.BlockSpec((MOD_ROWS, d), lambda l, j: (0, 0)),
            pl.BlockSpec((None, d, tn), lambda l, j: (l, 0, j)),
            pl.BlockSpec((None, 1, tn), lambda l, j: (l, 0, j)),
        ],
        out_specs=pl.BlockSpec((None, MOD_ROWS, tn), lambda l, j: (l, 0, j)),
        out_shape=jax.ShapeDtypeStruct((depth, MOD_ROWS, n), F32),
        compiler_params=_params(("parallel", "parallel")),
        name="mods",
    )(cc, w_mod, b_mod.reshape(depth, 1, n))


def _row_tiled(x, tm):
    if not isinstance(x, tuple):
        return [pl.BlockSpec((tm, x.shape[1]), lambda i: (i, 0))], [x], None
    lat, ctx = x
    nlat = lat.shape[0] // tm
    assert lat.shape[0] % tm == 0 and ctx.shape[0] % tm == 0
    specs = [pl.BlockSpec((tm, lat.shape[1]), lambda i: (jnp.minimum(i, nlat - 1), 0)),
             pl.BlockSpec((tm, ctx.shape[1]), lambda i: (jnp.maximum(i - nlat, 0), 0))]
    return specs, [lat, ctx], nlat


def _tile_rows(refs, nlat, rows):
    if len(refs) == 1:
        return refs[0][rows, :]
    return jnp.where(pl.program_id(0) < nlat, refs[0][rows, :], refs[1][rows, :])


def _norm_mod_kernel(*refs, n_x, nlat, tpb, nb, grow, groups):
    x_refs, (g_ref, sh_ref, sc_ref), rest = refs[:n_x], refs[n_x:n_x + 3], refs[n_x + 3:]
    i = pl.program_id(0)
    sh = _mod_row(sh_ref, i, tpb, nb)
    gmul = g_ref[grow:grow + 1, :] * (1.0 + _mod_row(sc_ref, i, tpb, nb))
    tm = x_refs[0].shape[0]
    norm = lambda rows: (_rms(_tile_rows(x_refs, nlat, rows), gmul) + sh).astype(BF16)
    if groups == 0:
        (o_ref,) = rest
        for r in range(0, tm, ROW_CHUNK):
            o_ref[r:r + ROW_CHUNK, :] = norm(slice(r, r + ROW_CHUNK))
        return
    wcs_ref, ae_ref, be_ref, ao_ref, bo_ref = rest
    hb = jnp.concatenate([norm(slice(r, r + ROW_CHUNK)) for r in range(0, tm, ROW_CHUNK)], axis=0)
    half = tm // 2
    dst = lax.broadcasted_iota(jnp.int32, (tm, tm), 0)
    src = lax.broadcasted_iota(jnp.int32, (tm, tm), 1)
    perm = jnp.where(src == 2 * (dst % half) + dst // half, 1.0, 0.0).astype(BF16)
    hb = jnp.dot(perm, hb, preferred_element_type=F32).astype(BF16)
    gd = hb.shape[1] // groups
    for g in range(groups):
        cols = slice(g * gd, (g + 1) * gd)
        r = jnp.dot(hb[:, cols], wcs_ref[...], preferred_element_type=F32)
        ae_ref[:, cols] = r[:half, :gd].astype(BF16)
        be_ref[:, cols] = r[:half, gd:].astype(BF16)
        ao_ref[:, cols] = r[half:, :gd].astype(BF16)
        bo_ref[:, cols] = r[half:, gd:].astype(BF16)


def _norm_mod(x, norm_g, mods, layer, grow, col, nrows, seq, nb, wcs=None):
    d = norm_g.shape[-1]
    tm = 512
    tpb = seq // tm
    groups = 0 if wcs is None else FNET_GROUPS
    x_specs, x_args, nlat = _row_tiled(x, tm)
    in_specs = x_specs + [
        pl.BlockSpec((None, 4, d), lambda i: (layer, 0, 0)),
        pl.BlockSpec((None, MOD_ROWS, d), lambda i: (layer, 0, col)),
        pl.BlockSpec((None, MOD_ROWS, d), lambda i: (layer, 0, col + 1)),
    ]
    args = x_args + [norm_g, mods, mods]
    out_spec = pl.BlockSpec((tm, d), lambda i: (i, 0))
    out_shape = jax.ShapeDtypeStruct((nrows, d), BF16)
    if wcs is not None:
        in_specs.append(pl.BlockSpec(wcs.shape, lambda i: (0, 0)))
        args.append(wcs)
        out_spec = [pl.BlockSpec((tm // 2, d), lambda i: (i, 0))] * 4
        out_shape = [jax.ShapeDtypeStruct((nrows // 2, d), BF16)] * 4
    return pl.pallas_call(
        functools.partial(_norm_mod_kernel, n_x=len(x_args), nlat=nlat, tpb=tpb, nb=nb, grow=grow,
                          groups=groups),
        grid=(nrows // tm,),
        in_specs=in_specs,
        out_specs=out_spec,
        out_shape=out_shape,
        compiler_params=_params(("parallel",)),
        name="norm_mod",
    )(*args)


def _mm_kernel(a_ref, w_ref, *rest, rope_tiles, q_tiles, q_scale):
    if rope_tiles:
        cos_ref, sin_ref, o_ref, wb_ref = rest
    else:
        o_ref, wb_ref = rest
    j = pl.program_id(0)

    @pl.when(pl.program_id(1) == 0)
    def _():
        wb_ref[...] = w_ref[...].astype(BF16)

    acc = jnp.dot(a_ref[...], wb_ref[...], preferred_element_type=F32)
    if not rope_tiles:
        o_ref[...] = acc.astype(o_ref.dtype)
        return

    @pl.when(j < rope_tiles)
    def _():
        cos = cos_ref[...]
        sin = sin_ref[...]
        scale = jnp.where(j < q_tiles, q_scale, 1.0).astype(F32)
        hd = cos.shape[1]
        for c in range(acc.shape[1] // hd):
            blk = acc[:, c * hd:(c + 1) * hd]
            rot = pltpu.roll(blk, hd // 2, axis=1)
            o_ref[:, c * hd:(c + 1) * hd] = ((blk * cos + rot * sin) * scale).astype(o_ref.dtype)

    @pl.when(j >= rope_tiles)
    def _():
        o_ref[...] = acc.astype(o_ref.dtype)


def _mm(a, w, col0, ncols, out_dtype, rope=None, name="mm"):
    m, k = a.shape
    tm = min(1024, m)
    tn = 1024
    assert m % tm == 0 and ncols % tn == 0 and col0 % tn == 0
    off = col0 // tn
    in_specs = [
        pl.BlockSpec((tm, k), lambda j, i: (i, 0)),
        pl.BlockSpec((k, tn), lambda j, i: (0, j + off)),
    ]
    args = [a, w]
    rope_tiles = q_tiles = 0
    q_scale = 1.0
    if rope is not None:
        cos2, sin2, rope_cols, q_cols, q_scale = rope
        hd = cos2.shape[1]
        in_specs += [pl.BlockSpec((tm, hd), lambda j, i: (i, 0))] * 2
        args += [cos2, sin2]
        rope_tiles, q_tiles = rope_cols // tn, q_cols // tn
    return pl.pallas_call(
        functools.partial(_mm_kernel, rope_tiles=rope_tiles, q_tiles=q_tiles, q_scale=q_scale),
        grid=(ncols // tn, m // tm),
        in_specs=in_specs,
        out_specs=pl.BlockSpec((tm, tn), lambda j, i: (i, j)),
        out_shape=jax.ShapeDtypeStruct((m, ncols), out_dtype),
        scratch_shapes=[pltpu.VMEM((k, tn), BF16)],
        compiler_params=_params(("parallel", "arbitrary")),
        name=name,
    )(*args)


def _mm_res_kernel(*refs, n_a, nlat_a, n_x, nlat_x, tpb, nb, grow):
    a_refs, w_ref, x_refs = refs[:n_a], refs[n_a], refs[n_a + 1:n_a + 1 + n_x]
    g_ref, gate_ref, o_ref = refs[n_a + 1 + n_x:]
    a = _tile_rows(a_refs, nlat_a, slice(None))
    y = jnp.dot(a, w_ref[...], preferred_element_type=F32)
    gout = g_ref[grow:grow + 1, :] * _mod_row(gate_ref, pl.program_id(0), tpb, nb)
    for r in range(0, y.shape[0], ROW_CHUNK):
        rows = slice(r, r + ROW_CHUNK)
        o_ref[rows, :] = _tile_rows(x_refs, nlat_x, rows) + _rms(y[rows], gout)


def _mm_res(a, w, x, norm_g, mods, layer, grow, gate_col, nrows, seq, nb):
    kdim, d = w.shape
    tm = 512
    tpb = seq // tm
    a_specs, a_args, nlat_a = _row_tiled(a, tm)
    x_specs, x_args, nlat_x = _row_tiled(x, tm)
    return pl.pallas_call(
        functools.partial(_mm_res_kernel, n_a=len(a_args), nlat_a=nlat_a, n_x=len(x_args),
                          nlat_x=nlat_x, tpb=tpb, nb=nb, grow=grow),
        grid=(nrows // tm,),
        in_specs=a_specs + [pl.BlockSpec((kdim, d), lambda i: (0, 0))] + x_specs + [
            pl.BlockSpec((None, 4, d), lambda i: (layer, 0, 0)),
            pl.BlockSpec((None, MOD_ROWS, d), lambda i: (layer, 0, gate_col)),
        ],
        out_specs=pl.BlockSpec((tm, d), lambda i: (i, 0)),
        out_shape=jax.ShapeDtypeStruct((nrows, d), F32),
        compiler_params=_params(("parallel",)),
        name="mm_res",
    )(*a_args, w.astype(BF16), *x_args, norm_g, mods)


def _mlp_kernel(x_ref, g_ref, sh_ref, sc_ref, gate_ref, w1_ref, w2_ref, o_ref, h_ref, *, tpb, nb):
    i = pl.program_id(0)
    k = pl.program_id(1)

    chunks = [slice(r, r + ROW_CHUNK) for r in range(0, x_ref.shape[0], ROW_CHUNK)]

    @pl.when(k == 0)
    def _():
        gmul = g_ref[2:3, :] * (1.0 + _mod_row(sc_ref, i, tpb, nb))
        sh = _mod_row(sh_ref, i, tpb, nb)
        for rows in chunks:
            h_ref[rows, :] = (_rms(x_ref[rows, :], gmul) + sh).astype(BF16)
        o_ref[...] = jnp.zeros_like(o_ref)

    u = jnp.dot(h_ref[...], w1_ref[...].astype(BF16), preferred_element_type=F32)
    u = jnp.square(jnp.maximum(u, 0.0)).astype(BF16)
    cn = MLP_OUT_CHUNK
    for c in range(o_ref.shape[1] // cn):
        cols = slice(c * cn, (c + 1) * cn)
        o_ref[:, cols] += jnp.dot(u, w2_ref[:, cols].astype(BF16), preferred_element_type=F32)

    @pl.when(k == pl.num_programs(1) - 1)
    def _():
        gout = g_ref[3:4, :] * _mod_row(gate_ref, i, tpb, nb)
        for rows in chunks:
            o_ref[rows, :] = x_ref[rows, :] + _rms(o_ref[rows, :], gout)


def _mlp(x, w1, w2, norm_g, mods, layer, nrows, seq, nb):
    d, hid = w1.shape[1], w1.shape[2]
    tm = 1024
    th = 512
    tpb = seq // tm
    mod_spec = lambda col: pl.BlockSpec((None, MOD_ROWS, d), lambda i, k: (layer, 0, col))
    row_spec = pl.BlockSpec((tm, d), lambda i, k: (i, 0), pipeline_mode=pl.Buffered(1))
    return pl.pallas_call(
        functools.partial(_mlp_kernel, tpb=tpb, nb=nb),
        grid=(nrows // tm, hid // th),
        in_specs=[
            pl.BlockSpec((tm, d), lambda i, k: (i, 0)),
            pl.BlockSpec((None, 4, d), lambda i, k: (layer, 0, 0)),
            mod_spec(3), mod_spec(4), mod_spec(5),
            pl.BlockSpec((None, d, th), lambda i, k: (layer, 0, k)),
            pl.BlockSpec((None, th, d), lambda i, k: (layer, k, 0)),
        ],
        out_specs=row_spec,
        out_shape=jax.ShapeDtypeStruct((nrows, d), F32),
        scratch_shapes=[pltpu.VMEM((tm, d), BF16)],
        compiler_params=_params(("parallel", "arbitrary")),
        name="mlp",
    )(x, norm_g, mods, mods, mods, w1, w2)


def _dft_tables(n, w=32):
    assert n % w == 0
    k = jnp.arange(n, dtype=jnp.int32)[:, None]
    unit = 2.0 * math.pi / n
    ang_a = ((k * (w * jnp.arange(n // w, dtype=jnp.int32))[None, :]) % n).astype(F32) * unit
    ang_b = ((k * jnp.arange(w, dtype=jnp.int32)[None, :]) % n).astype(F32) * unit
    ca, sa = jnp.cos(ang_a)[:, :, None], jnp.sin(ang_a)[:, :, None]
    cb, sb = jnp.cos(ang_b)[:, None, :], jnp.sin(ang_b)[:, None, :]
    return (ca * cb - sa * sb).reshape(n, n), (sa * cb + ca * sb).reshape(n, n)


def _posdft_kernel(ce_ref, se_ref, co_ref, so_ref, ae_ref, be_ref, ao_ref, bo_ref, o_ref):
    even = jnp.dot(ce_ref[...], ae_ref[...], preferred_element_type=F32)
    even += jnp.dot(se_ref[...], be_ref[...], preferred_element_type=F32)
    odd = jnp.dot(co_ref[...], ao_ref[...], preferred_element_type=F32)
    odd += jnp.dot(so_ref[...], bo_ref[...], preferred_element_type=F32)
    o_ref[0] = (even + odd).astype(o_ref.dtype)
    o_ref[1] = (even - odd).astype(o_ref.dtype)


def _posdft(parts, nbatch, seq, seq0):
    d = parts[0].shape[1]
    half = seq // 2
    tm = min(256, half)
    tn = 1024
    cos, sin = _dft_tables(seq)
    tables = [t[:half, par::2].astype(BF16) for par in (0, 1) for t in (cos, sin)]
    nj = d // tn
    tab_spec = pl.BlockSpec((tm, half), lambda b, j, m: (m, 0))
    src = pl.BlockSpec((half, tn), lambda b, j, m: (seq0 + b, j))
    y = pl.pallas_call(
        _posdft_kernel,
        grid=(nbatch, nj, half // tm),
        in_specs=[tab_spec] * 4 + [src] * 4,
        out_specs=pl.BlockSpec((None, 2, tm, tn), lambda b, j, m: (b, 0, m, j)),
        out_shape=jax.ShapeDtypeStruct((nbatch, 2, half, d), BF16),
        compiler_params=_params(("parallel", "parallel", "arbitrary")),
        name="posdft",
    )(*tables, *parts)
    return y.reshape(nbatch * seq, d)


def _split3_dot(mask_bf16, v):
    hi = v.astype(BF16)
    r1 = v - hi.astype(F32)
    mid = r1.astype(BF16)
    lo = (r1 - mid.astype(F32)).astype(BF16)
    out = jnp.dot(mask_bf16, hi, preferred_element_type=F32)
    out += jnp.dot(mask_bf16, mid, preferred_element_type=F32)
    out += jnp.dot(mask_bf16, lo, preferred_element_type=F32)
    return out


def _dot_nt(a, b):
    return lax.dot_general(a, b, (((1,), (1,)), ((), ())), preferred_element_type=F32)


def _dot_tn(a, b):
    return lax.dot_general(a, b, (((0,), (0,)), ((), ())), preferred_element_type=F32)


_GLA_HALF = 128


def _block_sum_masks(block, fwd):
    ri = lax.broadcasted_iota(jnp.int32, (_GLA_HALF, _GLA_HALF), 0)
    ci = lax.broadcasted_iota(jnp.int32, (_GLA_HALF, _GLA_HALF), 1)
    same = (ri // block) == (ci // block)
    ahead = (ri - ci) if fwd else (ci - ri)
    m_incl = jnp.where(same & (ahead >= 0), 1.0, 0.0).astype(BF16)
    m_excl = jnp.where(same & (ahead < 0), 1.0, 0.0).astype(BF16)
    return m_incl, m_excl


def _block_sums(mask, x):
    return jnp.concatenate([_split3_dot(mask, x[r0:r0 + _GLA_HALF])
                            for r0 in range(0, x.shape[0], _GLA_HALF)], axis=0)


def _gla_fast(fwd, q, kk, bc, v_ref, st_ref, o_ref):
    cs = GLA_CHUNK
    nchunk = q.shape[0] // cs
    nhead = st_ref.shape[0]
    ri = lax.broadcasted_iota(jnp.int32, (cs, cs), 0)
    ci = lax.broadcasted_iota(jnp.int32, (cs, cs), 1)
    keep = (ci <= ri) if fwd else (ci >= ri)
    sts = [st_ref[h] for h in range(nhead)]
    for j in range(nchunk):
        c = j if fwd else nchunk - 1 - j
        rows = slice(c * cs, (c + 1) * cs)
        b = bc[rows]
        tot = b[cs - 1:cs] if fwd else b[0:1]
        mid = 0.5 * tot
        qm = (q[rows] * jnp.exp(b - mid)).astype(BF16)
        km = (kk[rows] * jnp.exp(mid - b)).astype(BF16)
        carry_in = jnp.exp(mid)
        carry_out = jnp.exp(tot)
        new_out = jnp.exp(tot - mid)
        for h in range(nhead):
            hc = slice(h * HGRN_DK, (h + 1) * HGRN_DK)
            vb = v_ref[rows, hc]
            a = jnp.where(keep, _dot_nt(qm[:, hc], km[:, hc]), 0.0).astype(BF16)
            o = jnp.dot(a, vb, preferred_element_type=F32)
            o += _dot_nt(qm[:, hc], (sts[h] * carry_in[:, hc]).astype(BF16))
            o_ref[rows, hc] = o
            sts[h] = sts[h] * carry_out[:, hc] + _dot_tn(vb, km[:, hc]) * new_out[:, hc]
    for h in range(nhead):
        st_ref[h] = sts[h]


def _gla_exact(fwd, q, kk, logf, v_ref, st_ref, o_ref):
    seg_rows = q.shape[0]
    nsub = seg_rows // GLA_SUB
    hs = GLA_SUB // 2
    v = v_ref[...].astype(F32)

    m_incl, m_excl = _block_sum_masks(GLA_SUB, fwd)
    bc_all = _block_sums(m_incl, logf)
    qd_all = (q * jnp.exp(bc_all)).astype(BF16)
    kd_all = (kk * jnp.exp(_block_sums(m_excl, logf))).astype(BF16)

    rowh = lax.broadcasted_iota(jnp.int32, (hs, HGRN_DK), 0)
    for h in range(st_ref.shape[0]):
        hc = slice(h * HGRN_DK, (h + 1) * HGRN_DK)
        st = st_ref[h]
        for j in range(nsub):
            i = j if fwd else nsub - 1 - j
            r0 = i * GLA_SUB
            o = _dot_nt(qd_all[r0:r0 + GLA_SUB, hc], st.astype(BF16))
            acc = [o[:hs], o[hs:]]
            for s in range(GLA_SUB):
                sr = r0 + s
                b_s, k_s, v_s = bc_all[sr:sr + 1, hc], kk[sr:sr + 1, hc], v[sr:sr + 1, hc]
                for hh in range(2):
                    lo_row = hh * hs
                    reach_all = (lo_row > s) if fwd else (lo_row + hs - 1 < s)
                    reach_none = (lo_row + hs - 1 < s) if fwd else (lo_row > s)
                    if reach_none:
                        continue
                    rr = slice(r0 + lo_row, r0 + lo_row + hs)
                    diff = bc_all[rr, hc] - b_s
                    if not reach_all:
                        keep = (rowh + lo_row >= s) if fwd else (rowh + lo_row <= s)
                        diff = jnp.where(keep, diff, -1e30)
                    col = jnp.sum(q[rr, hc] * jnp.exp(diff) * k_s, axis=1, keepdims=True)
                    acc[hh] = acc[hh] + col * v_s
            o_ref[r0:r0 + hs, hc] = acc[0]
            o_ref[r0 + hs:r0 + GLA_SUB, hc] = acc[1]
            last = r0 + GLA_SUB - 1 if fwd else r0
            upd = _dot_tn(v_ref[r0:r0 + GLA_SUB, hc], kd_all[r0:r0 + GLA_SUB, hc])
            st = st * jnp.exp(bc_all[last:last + 1, hc]) + upd
        st_ref[h] = st


def _gla_segment(fwd, q_ref, z_ref, v_ref, lb, st_ref, o_ref):
    z = z_ref[...]
    ez = jnp.exp(-jnp.abs(z))
    r = 1.0 / (1.0 + ez)
    pos = z >= 0.0
    sig = jnp.where(pos, r, ez * r)
    nsig = jnp.where(pos, ez * r, r)
    logf = jnp.log(lb + (1.0 - lb) * sig)
    kk = (1.0 - lb) * nsig
    q = q_ref[...].astype(F32)

    bc = _block_sums(_block_sum_masks(GLA_CHUNK, fwd)[0], logf)
    edge = GLA_CHUNK - 1 if fwd else 0
    tot = bc[edge:edge + 1]
    for c in range(1, q.shape[0] // GLA_CHUNK):
        tot = jnp.minimum(tot, bc[c * GLA_CHUNK + edge:c * GLA_CHUNK + edge + 1])
    mild = jnp.max(-tot) <= GLA_FAST_MAX_DECAY

    @pl.when(mild)
    def _():
        _gla_fast(fwd, q, kk, bc, v_ref, st_ref, o_ref)

    @pl.when(jnp.logical_not(mild))
    def _():
        _gla_exact(fwd, q, kk, logf, v_ref, st_ref, o_ref)


def _gla_kernel(q_ref, z_ref, v_ref, g_ref, lbp_ref, gn_ref, y_ref, of_ref, st_ref, o_ref,
                *, layer, nseg):
    t = pl.program_id(2)
    fwd = t < nseg
    seg = t % nseg
    seg_rows = q_ref.shape[0]

    @pl.when(seg == 0)
    def _():
        st_ref[...] = jnp.zeros_like(st_ref)

    p = lbp_ref[...]
    e = jnp.exp(p - jnp.max(p, axis=0, keepdims=True))
    sm = e / jnp.sum(e, axis=0, keepdims=True)
    lb2 = jnp.zeros_like(sm[0])
    for j in range(1, layer + 1):
        lb2 = lb2 + sm[j]

    @pl.when(fwd)
    def _():
        _gla_segment(True, q_ref, z_ref, v_ref, lb2[0:1, :], st_ref, o_ref)
        of_ref[pl.ds(pl.multiple_of(seg * seg_rows, seg_rows), seg_rows), :] = o_ref[...]

    @pl.when(jnp.logical_not(fwd))
    def _():
        _gla_segment(False, q_ref, z_ref, v_ref, lb2[1:2, :], st_ref, o_ref)
        fseg = jnp.where(seg == 0, 0, nseg - seg)
        o = o_ref[...] + of_ref[pl.ds(pl.multiple_of(fseg * seg_rows, seg_rows), seg_rows), :]
        gate = _silu(g_ref[...].astype(F32))
        for h in range(st_ref.shape[0]):
            hc = slice(h * HGRN_DK, (h + 1) * HGRN_DK)
            y_ref[:, hc] = (_rms(o[:, hc], gn_ref[...]) * gate[:, hc]).astype(y_ref.dtype)


def _gla(pq, pz, pvg, hgrn_lb, gnorm, layer, nb, seq, ctx_len):
    t_rows, f = pq.shape
    heads = f // HGRN_DK
    sr = GLA_SEG
    assert ctx_len == sr and seq % sr == 0
    nlat = seq // sr
    nseg = nlat + 1
    lat_blocks = nb * nlat
    depth = hgrn_lb.shape[0]

    def rb(b, t):
        seg = t % nseg
        lat = b * nlat + jnp.where(t < nseg, seg - 1, nlat - seg)
        return jnp.where(seg == 0, lat_blocks + b, lat)

    hp = GLA_HEADS_PER_STEP
    assert heads % hp == 0
    hgroups = heads // hp
    wide = hp * HGRN_DK
    blk = lambda colf: pl.BlockSpec((sr, wide), lambda b, h, t: (rb(b, t), colf(h, t)))
    return pl.pallas_call(
        functools.partial(_gla_kernel, layer=layer, nseg=nseg),
        grid=(nb, hgroups, 2 * nseg),
        in_specs=[
            blk(lambda h, t: h),
            blk(lambda h, t: jnp.where(t < nseg, 0, hgroups) + h),
            blk(lambda h, t: h),
            blk(lambda h, t: hgroups + h),
            pl.BlockSpec((depth, 2, wide), lambda b, h, t: (0, 0, h)),
            pl.BlockSpec((1, HGRN_DK), lambda b, h, t: (0, 0)),
        ],
        out_specs=pl.BlockSpec((sr, wide), lambda b, h, t: (rb(b, jnp.maximum(t, nseg)), h)),
        out_shape=jax.ShapeDtypeStruct((t_rows, f), BF16),
        scratch_shapes=[
            pltpu.VMEM((nseg * sr, wide), F32),
            pltpu.VMEM((hp, HGRN_DK, HGRN_DK), F32),
            pltpu.VMEM((sr, wide), F32),
        ],
        compiler_params=_params(("parallel", "parallel", "arbitrary")),
        name="gla",
    )(pq, pz, pvg, pvg, hgrn_lb, gnorm.reshape(1, HGRN_DK))


def _attn_kernel(q_ref, kl_ref, kc_ref, vl_ref, vc_ref, lam_ref, sub_ref, o_ref, *, lambda_init, nq):
    qi = pl.program_id(2)
    hd = DIFF_HEAD_DIM
    nhead = o_ref.shape[1] // (2 * hd)
    lp = lam_ref[...]
    lam = (jnp.exp(jnp.sum(lp[0:1, :] * lp[1:2, :], keepdims=True))
           - jnp.exp(jnp.sum(lp[2:3, :] * lp[3:4, :], keepdims=True)) + lambda_init)

    def probs(sc, sl):
        m = jnp.max(sc, axis=-1, keepdims=True)
        if sl is not None:
            m = jnp.maximum(m, jnp.max(sl, axis=-1, keepdims=True))
        pc = jnp.exp2(sc - m)
        den = jnp.sum(pc, axis=-1, keepdims=True)
        pl_ = None
        if sl is not None:
            pl_ = jnp.exp2(sl - m)
            den = den + jnp.sum(pl_, axis=-1, keepdims=True)
            pl_ = pl_.astype(BF16)
        return pc.astype(BF16), pl_, 1.0 / den

    def attend(with_latent_keys):
        cols = [slice(c * hd, (c + 1) * hd) for c in range(2 * nhead)]
        sc = [_dot_nt(q_ref[:, c], kc_ref[:, c]) for c in cols]
        sl = [_dot_nt(q_ref[:, c], kl_ref[:, c]) if with_latent_keys else None for c in cols]
        for h in range(nhead):
            vcols = slice(2 * h * hd, 2 * (h + 1) * hd)
            outs = []
            for mp in range(2):
                pc, pl_, inv = probs(sc[2 * h + mp], sl[2 * h + mp])
                o = jnp.dot(pc, vc_ref[:, vcols], preferred_element_type=F32)
                if pl_ is not None:
                    o += jnp.dot(pl_, vl_ref[:, vcols], preferred_element_type=F32)
                outs.append(o * (inv if mp == 0 else lam * inv))
            y = _rms(outs[0] - outs[1], sub_ref[...]) * (1.0 - lambda_init)
            o_ref[:, vcols] = y.astype(o_ref.dtype)

    @pl.when(qi < nq)
    def _():
        attend(True)

    @pl.when(qi == nq)
    def _():
        attend(False)


def _attn(qkv, lam_p, subln, lambda_init, nb, seq, ctx_len):
    t_rows, d3 = qkv.shape
    d = d3 // 3
    wide = ATTN_HEADS_PER_STEP * 2 * DIFF_HEAD_DIM
    assert d % wide == 0
    hgroups = d // wide
    tq = ctx_len
    nq = seq // tq
    lat_qblocks = nb * nq
    qrow = lambda b, qi: jnp.where(qi == nq, lat_qblocks + b, b * nq + qi)
    lat = lambda part: pl.BlockSpec((seq, wide), lambda b, h, qi: (b, part * hgroups + h))
    ctx = lambda part: pl.BlockSpec((ctx_len, wide), lambda b, h, qi: (lat_qblocks + b, part * hgroups + h))
    return pl.pallas_call(
        functools.partial(_attn_kernel, lambda_init=lambda_init, nq=nq),
        grid=(nb, hgroups, nq + 1),
        in_specs=[
            pl.BlockSpec((tq, wide), lambda b, h, qi: (qrow(b, qi), h)),
            lat(1), ctx(1), lat(2), ctx(2),
            pl.BlockSpec((4, DIFF_HEAD_DIM), lambda b, h, qi: (0, 0)),
            pl.BlockSpec((1, 2 * DIFF_HEAD_DIM), lambda b, h, qi: (0, 0)),
        ],
        out_specs=pl.BlockSpec((tq, wide), lambda b, h, qi: (qrow(b, qi), h)),
        out_shape=jax.ShapeDtypeStruct((t_rows, d), BF16),
        compiler_params=_params(("parallel", "parallel", "arbitrary")),
        name="diff_attn",
    )(qkv, qkv, qkv, qkv, qkv, lam_p, subln.reshape(1, 2 * DIFF_HEAD_DIM))


def _rope_tables(seq, nb, ctx_rows):
    rows = seq // GRID_W
    row = jnp.repeat(jnp.arange(rows, dtype=F32), GRID_W)
    col = jnp.tile(jnp.arange(GRID_W, dtype=F32), rows)
    n_freq = DIFF_HEAD_DIM // 4
    inv = ROPE_BASE ** (-jnp.arange(n_freq, dtype=F32) / n_freq)
    ang = jnp.concatenate([row[:, None] * inv, col[:, None] * inv], axis=-1)
    cos, sin = jnp.cos(ang), jnp.sin(ang)
    cos2 = jnp.concatenate([cos, cos], axis=-1)
    sin2 = jnp.concatenate([-sin, sin], axis=-1)
    cos2 = jnp.concatenate([jnp.tile(cos2, (nb, 1)), jnp.ones((ctx_rows, DIFF_HEAD_DIM), F32)], axis=0)
    sin2 = jnp.concatenate([jnp.tile(sin2, (nb, 1)), jnp.zeros((ctx_rows, DIFF_HEAD_DIM), F32)], axis=0)
    return cos2, sin2


def kernel(x, c, ctx, c_ctx, w_mod, b_mod, norm_g, w_mlp_in, w_mlp_out, fnet_w_out, hgrn_w_in,
           hgrn_lb, hgrn_gnorm, hgrn_w_out, diff_w_qkv, diff_lambda, diff_subln, diff_w_out):
    nb, seq, d = x.shape
    ctx_len = ctx.shape[1]
    depth = w_mod.shape[0]
    lat_rows, ctx_rows = nb * seq, nb * ctx_len
    assert nb + 1 <= MOD_ROWS

    xs = (x.reshape(lat_rows, d), ctx.reshape(ctx_rows, d))
    cc = jnp.concatenate([c, c_ctx[None, :], jnp.zeros((MOD_ROWS - nb - 1, d), F32)], axis=0)
    mods = _mods(cc, w_mod, b_mod)

    for i in range(depth):
        mixer, slot = i % N_MIXERS, i // N_MIXERS
        ctx_out = i < depth - 1
        ctx_in = ctx_out or mixer != 0
        assert ctx_out or mixer == 0
        nrows = lat_rows + ctx_rows if ctx_in else lat_rows
        common = dict(nrows=nrows, seq=seq, nb=nb)
        if mixer == 0:
            gd = d // FNET_GROUPS
            cg, sg = _dft_tables(gd)
            wcs = jnp.concatenate([cg, -sg], axis=1).astype(BF16)
            parts = _norm_mod(xs, norm_g, mods, i, 0, 0, wcs=wcs, **common)
            y = _posdft(parts, nb, seq, 0)
            if ctx_in:
                y = (y, _posdft(parts, nb, ctx_len, lat_rows // ctx_len))
            w_out = fnet_w_out[slot]
        elif mixer == 1:
            h = _norm_mod(xs, norm_g, mods, i, 0, 0, **common)
            w_in = hgrn_w_in[slot]
            f = hgrn_lb.shape[-1]
            pq = _mm(h, w_in, 0, f, BF16, name="hgrn_q")
            pz = _mm(h, w_in, f, 2 * f, F32, name="hgrn_z")
            pvg = _mm(h, w_in, 3 * f, 2 * d, BF16, name="hgrn_vg")
            y = _gla(pq, pz, pvg, hgrn_lb, hgrn_gnorm[slot], i, nb, seq, ctx_len)
            w_out = hgrn_w_out[slot]
        else:
            h = _norm_mod(xs, norm_g, mods, i, 0, 0, **common)
            cos2, sin2 = _rope_tables(seq, nb, ctx_rows)
            rope = (cos2, sin2, 2 * d, d, DIFF_HEAD_DIM ** -0.5 * math.log2(math.e))
            qkv = _mm(h, diff_w_qkv[slot], 0, 3 * d, BF16, rope=rope, name="diff_qkv")
            lambda_init = 0.8 - 0.6 * math.exp(-0.3 * i)
            y = _attn(qkv, diff_lambda[slot], diff_subln[slot], lambda_init, nb, seq, ctx_len)
            w_out = diff_w_out[slot]
        xs = _mm_res(y, w_out, xs, norm_g, mods, i, 1, 2, **common)
        xs = _mlp(xs, w_mlp_in, w_mlp_out, norm_g, mods, i, **common)
    return xs[:lat_rows].reshape(nb, seq, d)
```

```python
import functools
import math

import jax
import jax.numpy as jnp
from jax import lax
from jax.experimental import pallas as pl
from jax.experimental.pallas import tpu as pltpu

F32 = jnp.float32
BF16 = jnp.bfloat16

NORM_EPS = 1e-6
ROPE_BASE = 10000.0
GRID_W = 64
N_MOD = 6
N_MIXERS = 3
FNET_GROUPS = 8
HGRN_DK = 128
DIFF_HEAD_DIM = 128
MOD_ROWS = 8

GLA_SUB = 16
GLA_CHUNK = 64
GLA_HEADS_PER_STEP = 8
ATTN_HEADS_PER_STEP = 2
GLA_FAST_MAX_DECAY = 150.0
GLA_SEG = 256
MLP_OUT_CHUNK = 512
ROW_CHUNK = 32
V7X_VMEM_LIMIT = 56 * 1024 * 1024


def _params(sem, vmem=V7X_VMEM_LIMIT):
    return pltpu.CompilerParams(dimension_semantics=sem, vmem_limit_bytes=vmem)


def _silu(v):
    return v * jax.nn.sigmoid(v)


def _rms(y, gamma):
    return y * lax.rsqrt(jnp.mean(y * y, axis=-1, keepdims=True) + NORM_EPS) * gamma


def _mod_row(ref, tile, tiles_per_batch, n_batch):
    r = jnp.minimum(tile // tiles_per_batch, n_batch)
    return ref[pl.ds(r, 1), :]


def _mods_kernel(c_ref, w_ref, b_ref, o_ref):
    a = _silu(c_ref[...]).astype(BF16)
    o_ref[...] = jnp.dot(a, w_ref[...].astype(BF16), preferred_element_type=F32) + b_ref[...]


def _mods(cc, w_mod, b_mod):
    depth, d, n = w_mod.shape
    tn = 1024
    return pl.pallas_call(
        _mods_kernel,
        grid=(depth, n // tn),
        in_specs=[
            pl.BlockSpec((MOD_ROWS, d), lambda l, j: (0, 0)),
            pl.BlockSpec((None, d, tn), lambda l, j: (l, 0, j)),
            pl.BlockSpec((None, 1, tn), lambda l, j: (l, 0, j)),
        ],
        out_specs=pl.BlockSpec((None, MOD_ROWS, tn), lambda l, j: (l, 0, j)),
        out_shape=jax.ShapeDtypeStruct((depth, MOD_ROWS, n), F32),
        compiler_params=_params(("parallel", "parallel")),
        name="mods",
    )(cc, w_mod, b_mod.reshape(depth, 1, n))


def _row_tiled(x, tm):
    if not isinstance(x, tuple):
        return [pl.BlockSpec((tm, x.shape[1]), lambda i: (i, 0))], [x], None
    lat, ctx = x
    nlat = lat.shape[0] // tm
    assert lat.shape[0] % tm == 0 and ctx.shape[0] % tm == 0
    specs = [pl.BlockSpec((tm, lat.shape[1]), lambda i: (jnp.minimum(i, nlat - 1), 0)),
             pl.BlockSpec((tm, ctx.shape[1]), lambda i: (jnp.maximum(i - nlat, 0), 0))]
    return specs, [lat, ctx], nlat


def _tile_rows(refs, nlat, rows):
    if len(refs) == 1:
        return refs[0][rows, :]
    return jnp.where(pl.program_id(0) < nlat, refs[0][rows, :], refs[1][rows, :])


def _norm_mod_kernel(*refs, n_x, nlat, tpb, nb, grow, groups):
    x_refs, (g_ref, sh_ref, sc_ref), rest = refs[:n_x], refs[n_x:n_x + 3], refs[n_x + 3:]
    i = pl.program_id(0)
    sh = _mod_row(sh_ref, i, tpb, nb)
    gmul = g_ref[grow:grow + 1, :] * (1.0 + _mod_row(sc_ref, i, tpb, nb))
    tm = x_refs[0].shape[0]
    norm = lambda rows: (_rms(_tile_rows(x_refs, nlat, rows), gmul) + sh).astype(BF16)
    if groups == 0:
        (o_ref,) = rest
        for r in range(0, tm, ROW_CHUNK):
            o_ref[r:r + ROW_CHUNK, :] = norm(slice(r, r + ROW_CHUNK))
        return
    wcs_ref, ae_ref, be_ref, ao_ref, bo_ref = rest
    hb = jnp.concatenate([norm(slice(r, r + ROW_CHUNK)) for r in range(0, tm, ROW_CHUNK)], axis=0)
    half = tm // 2
    dst = lax.broadcasted_iota(jnp.int32, (tm, tm), 0)
    src = lax.broadcasted_iota(jnp.int32, (tm, tm), 1)
    perm = jnp.where(src == 2 * (dst % half) + dst // half, 1.0, 0.0).astype(BF16)
    hb = jnp.dot(perm, hb, preferred_element_type=F32).astype(BF16)
    gd = hb.shape[1] // groups
    for g in range(groups):
        cols = slice(g * gd, (g + 1) * gd)
        r = jnp.dot(hb[:, cols], wcs_ref[...], preferred_element_type=F32)
        ae_ref[:, cols] = r[:half, :gd].astype(BF16)
        be_ref[:, cols] = r[:half, gd:].astype(BF16)
        ao_ref[:, cols] = r[half:, :gd].astype(BF16)
        bo_ref[:, cols] = r[half:, gd:].astype(BF16)


def _norm_mod(x, norm_g, mods, layer, grow, col, nrows, seq, nb, wcs=None):
    d = norm_g.shape[-1]
    tm = 512
    tpb = seq // tm
    groups = 0 if wcs is None else FNET_GROUPS
    x_specs, x_args, nlat = _row_tiled(x, tm)
    in_specs = x_specs + [
        pl.BlockSpec((None, 4, d), lambda i: (layer, 0, 0)),
        pl.BlockSpec((None, MOD_ROWS, d), lambda i: (layer, 0, col)),
        pl.BlockSpec((None, MOD_ROWS, d), lambda i: (layer, 0, col + 1)),
    ]
    args = x_args + [norm_g, mods, mods]
    out_spec = pl.BlockSpec((tm, d), lambda i: (i, 0))
    out_shape = jax.ShapeDtypeStruct((nrows, d), BF16)
    if wcs is not None:
        in_specs.append(pl.BlockSpec(wcs.shape, lambda i: (0, 0)))
        args.append(wcs)
        out_spec = [pl.BlockSpec((tm // 2, d), lambda i: (i, 0))] * 4
        out_shape = [jax.ShapeDtypeStruct((nrows // 2, d), BF16)] * 4
    return pl.pallas_call(
        functools.partial(_norm_mod_kernel, n_x=len(x_args), nlat=nlat, tpb=tpb, nb=nb, grow=grow,
                          groups=groups),
        grid=(nrows // tm,),
        in_specs=in_specs,
        out_specs=out_spec,
        out_shape=out_shape,
        compiler_params=_params(("parallel",)),
        name="norm_mod",
    )(*args)


def _mm_kernel(a_ref, w_ref, *rest, rope_tiles, q_tiles, q_scale):
    if rope_tiles:
        cos_ref, sin_ref, o_ref, wb_ref = rest
    else:
        o_ref, wb_ref = rest
    j = pl.program_id(0)

    @pl.when(pl.program_id(1) == 0)
    def _():
        wb_ref[...] = w_ref[...].astype(BF16)

    acc = jnp.dot(a_ref[...], wb_ref[...], preferred_element_type=F32)
    if not rope_tiles:
        o_ref[...] = acc.astype(o_ref.dtype)
        return

    @pl.when(j < rope_tiles)
    def _():
        cos = cos_ref[...]
        sin = sin_ref[...]
        scale = jnp.where(j < q_tiles, q_scale, 1.0).astype(F32)
        hd = cos.shape[1]
        for c in range(acc.shape[1] // hd):
            blk = acc[:, c * hd:(c + 1) * hd]
            rot = pltpu.roll(blk, hd // 2, axis=1)
            o_ref[:, c * hd:(c + 1) * hd] = ((blk * cos + rot * sin) * scale).astype(o_ref.dtype)

    @pl.when(j >= rope_tiles)
    def _():
        o_ref[...] = acc.astype(o_ref.dtype)


def _mm(a, w, col0, ncols, out_dtype, rope=None, name="mm"):
    m, k = a.shape
    tm = min(1024, m)
    tn = 1024
    assert m % tm == 0 and ncols % tn == 0 and col0 % tn == 0
    off = col0 // tn
    in_specs = [
        pl.BlockSpec((tm, k), lambda j, i: (i, 0)),
        pl.BlockSpec((k, tn), lambda j, i: (0, j + off)),
    ]
    args = [a, w]
    rope_tiles = q_tiles = 0
    q_scale = 1.0
    if rope is not None:
        cos2, sin2, rope_cols, q_cols, q_scale = rope
        hd = cos2.shape[1]
        in_specs += [pl.BlockSpec((tm, hd), lambda j, i: (i, 0))] * 2
        args += [cos2, sin2]
        rope_tiles, q_tiles = rope_cols // tn, q_cols // tn
    return pl.pallas_call(
        functools.partial(_mm_kernel, rope_tiles=rope_tiles, q_tiles=q_tiles, q_scale=q_scale),
        grid=(ncols // tn, m // tm),
        in_specs=in_specs,
        out_specs=pl.BlockSpec((tm, tn), lambda j, i: (i, j)),
        out_shape=jax.ShapeDtypeStruct((m, ncols), out_dtype),
        scratch_shapes=[pltpu.VMEM((k, tn), BF16)],
        compiler_params=_params(("parallel", "arbitrary")),
        name=name,
    )(*args)


def _mm_res_kernel(*refs, n_a, nlat_a, n_x, nlat_x, tpb, nb, grow):
    a_refs, w_ref, x_refs = refs[:n_a], refs[n_a], refs[n_a + 1:n_a + 1 + n_x]
    g_ref, gate_ref, o_ref = refs[n_a + 1 + n_x:]
    a = _tile_rows(a_refs, nlat_a, slice(None))
    y = jnp.dot(a, w_ref[...], preferred_element_type=F32)
    gout = g_ref[grow:grow + 1, :] * _mod_row(gate_ref, pl.program_id(0), tpb, nb)
    for r in range(0, y.shape[0], ROW_CHUNK):
        rows = slice(r, r + ROW_CHUNK)
        o_ref[rows, :] = _tile_rows(x_refs, nlat_x, rows) + _rms(y[rows], gout)


def _mm_res(a, w, x, norm_g, mods, layer, grow, gate_col, nrows, seq, nb):
    kdim, d = w.shape
    tm = 512
    tpb = seq // tm
    a_specs, a_args, nlat_a = _row_tiled(a, tm)
    x_specs, x_args, nlat_x = _row_tiled(x, tm)
    return pl.pallas_call(
        functools.partial(_mm_res_kernel, n_a=len(a_args), nlat_a=nlat_a, n_x=len(x_args),
                          nlat_x=nlat_x, tpb=tpb, nb=nb, grow=grow),
        grid=(nrows // tm,),
        in_specs=a_specs + [pl.BlockSpec((kdim, d), lambda i: (0, 0))] + x_specs + [
            pl.BlockSpec((None, 4, d), lambda i: (layer, 0, 0)),
            pl.BlockSpec((None, MOD_ROWS, d), lambda i: (layer, 0, gate_col)),
        ],
        out_specs=pl.BlockSpec((tm, d), lambda i: (i, 0)),
        out_shape=jax.ShapeDtypeStruct((nrows, d), F32),
        compiler_params=_params(("parallel",)),
        name="mm_res",
    )(*a_args, w.astype(BF16), *x_args, norm_g, mods)


def _mlp_kernel(x_ref, g_ref, sh_ref, sc_ref, gate_ref, w1_ref, w2_ref, o_ref, h_ref, *, tpb, nb):
    i = pl.program_id(0)
    k = pl.program_id(1)

    chunks = [slice(r, r + ROW_CHUNK) for r in range(0, x_ref.shape[0], ROW_CHUNK)]

    @pl.when(k == 0)
    def _():
        gmul = g_ref[2:3, :] * (1.0 + _mod_row(sc_ref, i, tpb, nb))
        sh = _mod_row(sh_ref, i, tpb, nb)
        for rows in chunks:
            h_ref[rows, :] = (_rms(x_ref[rows, :], gmul) + sh).astype(BF16)
        o_ref[...] = jnp.zeros_like(o_ref)

    u = jnp.dot(h_ref[...], w1_ref[...].astype(BF16), preferred_element_type=F32)
    u = jnp.square(jnp.maximum(u, 0.0)).astype(BF16)
    cn = MLP_OUT_CHUNK
    for c in range(o_ref.shape[1] // cn):
        cols = slice(c * cn, (c + 1) * cn)
        o_ref[:, cols] += jnp.dot(u, w2_ref[:, cols].astype(BF16), preferred_element_type=F32)

    @pl.when(k == pl.num_programs(1) - 1)
    def _():
        gout = g_ref[3:4, :] * _mod_row(gate_ref, i, tpb, nb)
        for rows in chunks:
            o_ref[rows, :] = x_ref[rows, :] + _rms(o_ref[rows, :], gout)


def _mlp(x, w1, w2, norm_g, mods, layer, nrows, seq, nb):
    d, hid = w1.shape[1], w1.shape[2]
    tm = 1024
    th = 512
    tpb = seq // tm
    mod_spec = lambda col: pl.BlockSpec((None, MOD_ROWS, d), lambda i, k: (layer, 0, col))
    return pl.pallas_call(
        functools.partial(_mlp_kernel, tpb=tpb, nb=nb),
        grid=(nrows // tm, hid // th),
        in_specs=[
            pl.BlockSpec((tm, d), lambda i, k: (i, 0)),
            pl.BlockSpec((None, 4, d), lambda i, k: (layer, 0, 0)),
            mod_spec(3), mod_spec(4), mod_spec(5),
            pl.BlockSpec((None, d, th), lambda i, k: (layer, 0, k)),
            pl.BlockSpec((None, th, d), lambda i, k: (layer, k, 0)),
        ],
        out_specs=pl.BlockSpec((tm, d), lambda i, k: (i, 0)),
        out_shape=jax.ShapeDtypeStruct((nrows, d), F32),
        scratch_shapes=[pltpu.VMEM((tm, d), BF16)],
        compiler_params=_params(("parallel", "arbitrary")),
        name="mlp",
    )(x, norm_g, mods, mods, mods, w1, w2)


def _dft_tables(n, rows=None, col0=0, col_step=1, w=32):
    rows = n if rows is None else rows
    ncol = n // col_step
    assert ncol % w == 0
    k = jnp.arange(rows, dtype=jnp.int32)[:, None]
    unit = 2.0 * math.pi / n
    la = col_step * w * jnp.arange(ncol // w, dtype=jnp.int32)[None, :]
    lb = col0 + col_step * jnp.arange(w, dtype=jnp.int32)[None, :]
    ang_a = ((k * la) % n).astype(F32) * unit
    ang_b = ((k * lb) % n).astype(F32) * unit
    ca, sa = jnp.cos(ang_a)[:, :, None], jnp.sin(ang_a)[:, :, None]
    cb, sb = jnp.cos(ang_b)[:, None, :], jnp.sin(ang_b)[:, None, :]
    return (ca * cb - sa * sb).reshape(rows, ncol), (sa * cb + ca * sb).reshape(rows, ncol)


def _posdft_kernel(ce_ref, se_ref, co_ref, so_ref, ae_ref, be_ref, ao_ref, bo_ref, o_ref):
    even = jnp.dot(ce_ref[...], ae_ref[...], preferred_element_type=F32)
    even += jnp.dot(se_ref[...], be_ref[...], preferred_element_type=F32)
    odd = jnp.dot(co_ref[...], ao_ref[...], preferred_element_type=F32)
    odd += jnp.dot(so_ref[...], bo_ref[...], preferred_element_type=F32)
    o_ref[0] = (even + odd).astype(o_ref.dtype)
    o_ref[1] = (even - odd).astype(o_ref.dtype)


def _posdft(parts, nbatch, seq, seq0):
    d = parts[0].shape[1]
    half = seq // 2
    tm = min(256, half)
    tn = 1024
    tables = [t.astype(BF16) for par in (0, 1) for t in _dft_tables(seq, half, par, 2)]
    nj = d // tn
    tab_spec = pl.BlockSpec((tm, half), lambda b, j, m: (m, 0))
    src = pl.BlockSpec((half, tn), lambda b, j, m: (seq0 + b, j))
    y = pl.pallas_call(
        _posdft_kernel,
        grid=(nbatch, nj, half // tm),
        in_specs=[tab_spec] * 4 + [src] * 4,
        out_specs=pl.BlockSpec((None, 2, tm, tn), lambda b, j, m: (b, 0, m, j)),
        out_shape=jax.ShapeDtypeStruct((nbatch, 2, half, d), BF16),
        compiler_params=_params(("parallel", "parallel", "arbitrary")),
        name="posdft",
    )(*tables, *parts)
    return y.reshape(nbatch * seq, d)


def _split3_dot(mask_bf16, v):
    hi = v.astype(BF16)
    r1 = v - hi.astype(F32)
    mid = r1.astype(BF16)
    lo = (r1 - mid.astype(F32)).astype(BF16)
    out = jnp.dot(mask_bf16, hi, preferred_element_type=F32)
    out += jnp.dot(mask_bf16, mid, preferred_element_type=F32)
    out += jnp.dot(mask_bf16, lo, preferred_element_type=F32)
    return out


def _dot_nt(a, b):
    return lax.dot_general(a, b, (((1,), (1,)), ((), ())), preferred_element_type=F32)


def _dot_tn(a, b):
    return lax.dot_general(a, b, (((0,), (0,)), ((), ())), preferred_element_type=F32)


_GLA_HALF = 128


def _block_sum_masks(block, fwd):
    ri = lax.broadcasted_iota(jnp.int32, (_GLA_HALF, _GLA_HALF), 0)
    ci = lax.broadcasted_iota(jnp.int32, (_GLA_HALF, _GLA_HALF), 1)
    same = (ri // block) == (ci // block)
    ahead = (ri - ci) if fwd else (ci - ri)
    m_incl = jnp.where(same & (ahead >= 0), 1.0, 0.0).astype(BF16)
    m_excl = jnp.where(same & (ahead < 0), 1.0, 0.0).astype(BF16)
    return m_incl, m_excl


def _block_sums(mask, x):
    return jnp.concatenate([_split3_dot(mask, x[r0:r0 + _GLA_HALF])
                            for r0 in range(0, x.shape[0], _GLA_HALF)], axis=0)


def _gla_fast(fwd, q, kk, bc, v_ref, st_ref, o_ref):
    cs = GLA_CHUNK
    nchunk = q.shape[0] // cs
    nhead = st_ref.shape[0]
    ri = lax.broadcasted_iota(jnp.int32, (cs, cs), 0)
    ci = lax.broadcasted_iota(jnp.int32, (cs, cs), 1)
    keep = (ci <= ri) if fwd else (ci >= ri)
    sts = [st_ref[h] for h in range(nhead)]
    for j in range(nchunk):
        c = j if fwd else nchunk - 1 - j
        rows = slice(c * cs, (c + 1) * cs)
        b = bc[rows]
        tot = b[cs - 1:cs] if fwd else b[0:1]
        mid = 0.5 * tot
        qm = (q[rows] * jnp.exp(b - mid)).astype(BF16)
        km = (kk[rows] * jnp.exp(mid - b)).astype(BF16)
        carry_in = jnp.exp(mid)
        carry_out = jnp.exp(tot)
        new_out = jnp.exp(tot - mid)
        for h in range(nhead):
            hc = slice(h * HGRN_DK, (h + 1) * HGRN_DK)
            vb = v_ref[rows, hc]
            a = jnp.where(keep, _dot_nt(qm[:, hc], km[:, hc]), 0.0).astype(BF16)
            o = jnp.dot(a, vb, preferred_element_type=F32)
            o += _dot_nt(qm[:, hc], (sts[h] * carry_in[:, hc]).astype(BF16))
            o_ref[rows, hc] = o
            sts[h] = sts[h] * carry_out[:, hc] + _dot_tn(vb, km[:, hc]) * new_out[:, hc]
    for h in range(nhead):
        st_ref[h] = sts[h]


def _gla_exact(fwd, q, kk, logf, v_ref, st_ref, o_ref):
    seg_rows = q.shape[0]
    nsub = seg_rows // GLA_SUB
    hs = GLA_SUB // 2
    v = v_ref[...].astype(F32)

    m_incl, m_excl = _block_sum_masks(GLA_SUB, fwd)
    bc_all = _block_sums(m_incl, logf)
    qd_all = (q * jnp.exp(bc_all)).astype(BF16)
    kd_all = (kk * jnp.exp(_block_sums(m_excl, logf))).astype(BF16)

    rowh = lax.broadcasted_iota(jnp.int32, (hs, HGRN_DK), 0)
    for h in range(st_ref.shape[0]):
        hc = slice(h * HGRN_DK, (h + 1) * HGRN_DK)
        st = st_ref[h]
        for j in range(nsub):
            i = j if fwd else nsub - 1 - j
            r0 = i * GLA_SUB
            o = _dot_nt(qd_all[r0:r0 + GLA_SUB, hc], st.astype(BF16))
            acc = [o[:hs], o[hs:]]
            for s in range(GLA_SUB):
                sr = r0 + s
                b_s, k_s, v_s = bc_all[sr:sr + 1, hc], kk[sr:sr + 1, hc], v[sr:sr + 1, hc]
                for hh in range(2):
                    lo_row = hh * hs
                    reach_all = (lo_row > s) if fwd else (lo_row + hs - 1 < s)
                    reach_none = (lo_row + hs - 1 < s) if fwd else (lo_row > s)
                    if reach_none:
                        continue
                    rr = slice(r0 + lo_row, r0 + lo_row + hs)
                    diff = bc_all[rr, hc] - b_s
                    if not reach_all:
                        keep = (rowh + lo_row >= s) if fwd else (rowh + lo_row <= s)
                        diff = jnp.where(keep, diff, -1e30)
                    col = jnp.sum(q[rr, hc] * jnp.exp(diff) * k_s, axis=1, keepdims=True)
                    acc[hh] = acc[hh] + col * v_s
            o_ref[r0:r0 + hs, hc] = acc[0]
            o_ref[r0 + hs:r0 + GLA_SUB, hc] = acc[1]
            last = r0 + GLA_SUB - 1 if fwd else r0
            upd = _dot_tn(v_ref[r0:r0 + GLA_SUB, hc], kd_all[r0:r0 + GLA_SUB, hc])
            st = st * jnp.exp(bc_all[last:last + 1, hc]) + upd
        st_ref[h] = st


def _gla_segment(fwd, q_ref, z_ref, v_ref, lb, st_ref, o_ref):
    half_gap = 0.5 * (1.0 - lb)
    f = (lb + half_gap) + half_gap * jnp.tanh(0.5 * z_ref[...])
    logf = jnp.log(f)
    kk = 1.0 - f
    q = q_ref[...].astype(F32)

    bc = _block_sums(_block_sum_masks(GLA_CHUNK, fwd)[0], logf)
    edge = GLA_CHUNK - 1 if fwd else 0
    tot = bc[edge:edge + 1]
    for c in range(1, q.shape[0] // GLA_CHUNK):
        tot = jnp.minimum(tot, bc[c * GLA_CHUNK + edge:c * GLA_CHUNK + edge + 1])
    mild = jnp.max(-tot) <= GLA_FAST_MAX_DECAY

    @pl.when(mild)
    def _():
        _gla_fast(fwd, q, kk, bc, v_ref, st_ref, o_ref)

    @pl.when(jnp.logical_not(mild))
    def _():
        _gla_exact(fwd, q, kk, logf, v_ref, st_ref, o_ref)


def _gla_kernel(q_ref, z_ref, v_ref, g_ref, lbp_ref, gn_ref, y_ref, of_ref, st_ref, o_ref,
                *, layer, nseg):
    t = pl.program_id(2)
    fwd = t < nseg
    seg = t % nseg
    seg_rows = q_ref.shape[0]

    @pl.when(seg == 0)
    def _():
        st_ref[...] = jnp.zeros_like(st_ref)

    p = lbp_ref[...]
    e = jnp.exp(p - jnp.max(p, axis=0, keepdims=True))
    sm = e / jnp.sum(e, axis=0, keepdims=True)
    lb2 = jnp.zeros_like(sm[0])
    for j in range(1, layer + 1):
        lb2 = lb2 + sm[j]

    @pl.when(fwd)
    def _():
        _gla_segment(True, q_ref, z_ref, v_ref, lb2[0:1, :], st_ref, o_ref)
        of_ref[pl.ds(pl.multiple_of(seg * seg_rows, seg_rows), seg_rows), :] = o_ref[...]

    @pl.when(jnp.logical_not(fwd))
    def _():
        _gla_segment(False, q_ref, z_ref, v_ref, lb2[1:2, :], st_ref, o_ref)
        fseg = jnp.where(seg == 0, 0, nseg - seg)
        o = o_ref[...] + of_ref[pl.ds(pl.multiple_of(fseg * seg_rows, seg_rows), seg_rows), :]
        gate = _silu(g_ref[...].astype(F32))
        for h in range(st_ref.shape[0]):
            hc = slice(h * HGRN_DK, (h + 1) * HGRN_DK)
            y_ref[:, hc] = (_rms(o[:, hc], gn_ref[...]) * gate[:, hc]).astype(y_ref.dtype)


def _gla(pq, pz, pvg, hgrn_lb, gnorm, layer, nb, seq, ctx_len):
    t_rows, f = pq.shape
    heads = f // HGRN_DK
    sr = GLA_SEG
    assert ctx_len == sr and seq % sr == 0
    nlat = seq // sr
    nseg = nlat + 1
    lat_blocks = nb * nlat
    depth = hgrn_lb.shape[0]

    def rb(b, t):
        seg = t % nseg
        lat = b * nlat + jnp.where(t < nseg, seg - 1, nlat - seg)
        return jnp.where(seg == 0, lat_blocks + b, lat)

    hp = GLA_HEADS_PER_STEP
    assert heads % hp == 0
    hgroups = heads // hp
    wide = hp * HGRN_DK
    blk = lambda colf: pl.BlockSpec((sr, wide), lambda b, h, t: (rb(b, t), colf(h, t)))
    return pl.pallas_call(
        functools.partial(_gla_kernel, layer=layer, nseg=nseg),
        grid=(nb, hgroups, 2 * nseg),
        in_specs=[
            blk(lambda h, t: h),
            blk(lambda h, t: jnp.where(t < nseg, 0, hgroups) + h),
            blk(lambda h, t: h),
            blk(lambda h, t: hgroups + h),
            pl.BlockSpec((depth, 2, wide), lambda b, h, t: (0, 0, h)),
            pl.BlockSpec((1, HGRN_DK), lambda b, h, t: (0, 0)),
        ],
        out_specs=pl.BlockSpec((sr, wide), lambda b, h, t: (rb(b, jnp.maximum(t, nseg)), h)),
        out_shape=jax.ShapeDtypeStruct((t_rows, f), BF16),
        scratch_shapes=[
            pltpu.VMEM((nseg * sr, wide), F32),
            pltpu.VMEM((hp, HGRN_DK, HGRN_DK), F32),
            pltpu.VMEM((sr, wide), F32),
        ],
        compiler_params=_params(("parallel", "parallel", "arbitrary")),
        name="gla",
    )(pq, pz, pvg, pvg, hgrn_lb, gnorm.reshape(1, HGRN_DK))


def _attn_kernel(q_ref, kl_ref, kc_ref, vl_ref, vc_ref, lam_ref, sub_ref, o_ref, *, lambda_init, nq):
    qi = pl.program_id(2)
    hd = DIFF_HEAD_DIM
    nhead = o_ref.shape[1] // (2 * hd)
    lp = lam_ref[...]
    lam = (jnp.exp(jnp.sum(lp[0:1, :] * lp[1:2, :], keepdims=True))
           - jnp.exp(jnp.sum(lp[2:3, :] * lp[3:4, :], keepdims=True)) + lambda_init)

    def probs(sc, sl):
        m = jnp.max(sc, axis=-1, keepdims=True)
        if sl is not None:
            m = jnp.maximum(m, jnp.max(sl, axis=-1, keepdims=True))
        pc = jnp.exp2(sc - m)
        den = jnp.sum(pc, axis=-1, keepdims=True)
        pl_ = None
        if sl is not None:
            pl_ = jnp.exp2(sl - m)
            den = den + jnp.sum(pl_, axis=-1, keepdims=True)
            pl_ = pl_.astype(BF16)
        return pc.astype(BF16), pl_, 1.0 / den

    def attend(with_latent_keys):
        cols = [slice(c * hd, (c + 1) * hd) for c in range(2 * nhead)]
        sc = [_dot_nt(q_ref[:, c], kc_ref[:, c]) for c in cols]
        sl = [_dot_nt(q_ref[:, c], kl_ref[:, c]) if with_latent_keys else None for c in cols]
        for h in range(nhead):
            vcols = slice(2 * h * hd, 2 * (h + 1) * hd)
            outs = []
            for mp in range(2):
                pc, pl_, inv = probs(sc[2 * h + mp], sl[2 * h + mp])
                o = jnp.dot(pc, vc_ref[:, vcols], preferred_element_type=F32)
                if pl_ is not None:
                    o += jnp.dot(pl_, vl_ref[:, vcols], preferred_element_type=F32)
                outs.append(o * (inv if mp == 0 else lam * inv))
            y = _rms(outs[0] - outs[1], sub_ref[...]) * (1.0 - lambda_init)
            o_ref[:, vcols] = y.astype(o_ref.dtype)

    @pl.when(qi < nq)
    def _():
        attend(True)

    @pl.when(qi == nq)
    def _():
        attend(False)


def _attn(qkv, lam_p, subln, lambda_init, nb, seq, ctx_len):
    t_rows, d3 = qkv.shape
    d = d3 // 3
    wide = ATTN_HEADS_PER_STEP * 2 * DIFF_HEAD_DIM
    assert d % wide == 0
    hgroups = d // wide
    tq = ctx_len
    nq = seq // tq
    lat_qblocks = nb * nq
    qrow = lambda b, qi: jnp.where(qi == nq, lat_qblocks + b, b * nq + qi)
    lat = lambda part: pl.BlockSpec((seq, wide), lambda b, h, qi: (b, part * hgroups + h))
    ctx = lambda part: pl.BlockSpec((ctx_len, wide), lambda b, h, qi: (lat_qblocks + b, part * hgroups + h))
    return pl.pallas_call(
        functools.partial(_attn_kernel, lambda_init=lambda_init, nq=nq),
        grid=(nb, hgroups, nq + 1),
        in_specs=[
            pl.BlockSpec((tq, wide), lambda b, h, qi: (qrow(b, qi), h)),
            lat(1), ctx(1), lat(2), ctx(2),
            pl.BlockSpec((4, DIFF_HEAD_DIM), lambda b, h, qi: (0, 0)),
            pl.BlockSpec((1, 2 * DIFF_HEAD_DIM), lambda b, h, qi: (0, 0)),
        ],
        out_specs=pl.BlockSpec((tq, wide), lambda b, h, qi: (qrow(b, qi), h)),
        out_shape=jax.ShapeDtypeStruct((t_rows, d), BF16),
        compiler_params=_params(("parallel", "parallel", "arbitrary")),
        name="diff_attn",
    )(qkv, qkv, qkv, qkv, qkv, lam_p, subln.reshape(1, 2 * DIFF_HEAD_DIM))


def _rope_tables(seq, nb, ctx_rows):
    rows = seq // GRID_W
    row = jnp.repeat(jnp.arange(rows, dtype=F32), GRID_W)
    col = jnp.tile(jnp.arange(GRID_W, dtype=F32), rows)
    n_freq = DIFF_HEAD_DIM // 4
    inv = ROPE_BASE ** (-jnp.arange(n_freq, dtype=F32) / n_freq)
    ang = jnp.concatenate([row[:, None] * inv, col[:, None] * inv], axis=-1)
    cos, sin = jnp.cos(ang), jnp.sin(ang)
    cos2 = jnp.concatenate([cos, cos], axis=-1)
    sin2 = jnp.concatenate([-sin, sin], axis=-1)
    cos2 = jnp.concatenate([jnp.tile(cos2, (nb, 1)), jnp.ones((ctx_rows, DIFF_HEAD_DIM), F32)], axis=0)
    sin2 = jnp.concatenate([jnp.tile(sin2, (nb, 1)), jnp.zeros((ctx_rows, DIFF_HEAD_DIM), F32)], axis=0)
    return cos2, sin2


def kernel(x, c, ctx, c_ctx, w_mod, b_mod, norm_g, w_mlp_in, w_mlp_out, fnet_w_out, hgrn_w_in,
           hgrn_lb, hgrn_gnorm, hgrn_w_out, diff_w_qkv, diff_lambda, diff_subln, diff_w_out):
    nb, seq, d = x.shape
    ctx_len = ctx.shape[1]
    depth = w_mod.shape[0]
    lat_rows, ctx_rows = nb * seq, nb * ctx_len
    assert nb + 1 <= MOD_ROWS

    xs = (x.reshape(lat_rows, d), ctx.reshape(ctx_rows, d))
    cc = jnp.concatenate([c, c_ctx[None, :], jnp.zeros((MOD_ROWS - nb - 1, d), F32)], axis=0)
    mods = _mods(cc, w_mod, b_mod)

    for i in range(depth):
        mixer, slot = i % N_MIXERS, i // N_MIXERS
        ctx_out = i < depth - 1
        ctx_in = ctx_out or mixer != 0
        assert ctx_out or mixer == 0
        nrows = lat_rows + ctx_rows if ctx_in else lat_rows
        common = dict(nrows=nrows, seq=seq, nb=nb)
        if mixer == 0:
            gd = d // FNET_GROUPS
            cg, sg = _dft_tables(gd)
            wcs = jnp.concatenate([cg, -sg], axis=1).astype(BF16)
            parts = _norm_mod(xs, norm_g, mods, i, 0, 0, wcs=wcs, **common)
            y = _posdft(parts, nb, seq, 0)
            if ctx_in:
                y = (y, _posdft(parts, nb, ctx_len, lat_rows // ctx_len))
            w_out = fnet_w_out[slot]
        elif mixer == 1:
            h = _norm_mod(xs, norm_g, mods, i, 0, 0, **common)
            w_in = hgrn_w_in[slot]
            f = hgrn_lb.shape[-1]
            pq = _mm(h, w_in, 0, f, BF16, name="hgrn_q")
            pz = _mm(h, w_in, f, 2 * f, F32, name="hgrn_z")
            pvg = _mm(h, w_in, 3 * f, 2 * d, BF16, name="hgrn_vg")
            y = _gla(pq, pz, pvg, hgrn_lb, hgrn_gnorm[slot], i, nb, seq, ctx_len)
            w_out = hgrn_w_out[slot]
        else:
            h = _norm_mod(xs, norm_g, mods, i, 0, 0, **common)
            cos2, sin2 = _rope_tables(seq, nb, ctx_rows)
            rope = (cos2, sin2, 2 * d, d, DIFF_HEAD_DIM ** -0.5 * math.log2(math.e))
            qkv = _mm(h, diff_w_qkv[slot], 0, 3 * d, BF16, rope=rope, name="diff_qkv")
            lambda_init = 0.8 - 0.6 * math.exp(-0.3 * i)
            y = _attn(qkv, diff_lambda[slot], diff_subln[slot], lambda_init, nb, seq, ctx_len)
            w_out = diff_w_out[slot]
        xs = _mm_res(y, w_out, xs, norm_g, mods, i, 1, 2, **common)
        xs = _mlp(xs, w_mlp_in, w_mlp_out, norm_g, mods, i, **common)
    return xs[:lat_rows].reshape(nb, seq, d)
```

```python
import functools
import math

import jax
import jax.numpy as jnp
from jax import lax
from jax.experimental import pallas as pl
from jax.experimental.pallas import tpu as pltpu

F32 = jnp.float32
BF16 = jnp.bfloat16

NORM_EPS = 1e-6
ROPE_BASE = 10000.0
GRID_W = 64
N_MOD = 6
N_MIXERS = 3
FNET_GROUPS = 8
HGRN_DK = 128
DIFF_HEAD_DIM = 128
MOD_ROWS = 8

GLA_SUB = 16
GLA_CHUNK = 64
GLA_HEADS_PER_STEP = 8
ATTN_HEADS_PER_STEP = 2
GLA_FAST_MAX_DECAY = 150.0
GLA_SEG = 256
MLP_OUT_CHUNK = 512
ROW_CHUNK = 32
V7X_VMEM_LIMIT = 56 * 1024 * 1024


def _params(sem, vmem=V7X_VMEM_LIMIT):
    return pltpu.CompilerParams(dimension_semantics=sem, vmem_limit_bytes=vmem)


def _silu(v):
    return v * jax.nn.sigmoid(v)


def _rms(y, gamma):
    return y * lax.rsqrt(jnp.mean(y * y, axis=-1, keepdims=True) + NORM_EPS) * gamma


def _mod_row(ref, tile, tiles_per_batch, n_batch):
    r = jnp.minimum(tile // tiles_per_batch, n_batch)
    return ref[pl.ds(r, 1), :]


def _mods_kernel(c_ref, w_ref, b_ref, o_ref):
    a = _silu(c_ref[...]).astype(BF16)
    o_ref[...] = jnp.dot(a, w_ref[...].astype(BF16), preferred_element_type=F32) + b_ref[...]


def _mods(cc, w_mod, b_mod):
    depth, d, n = w_mod.shape
    tn = 1024
    return pl.pallas_call(
        _mods_kernel,
        grid=(depth, n // tn),
        in_specs=[
            pl.BlockSpec((MOD_ROWS, d), lambda l, j: (0, 0)),
            pl.BlockSpec((None, d, tn), lambda l, j: (l, 0, j)),
            pl.BlockSpec((None, 1, tn), lambda l, j: (l, 0, j)),
        ],
        out_specs=pl.BlockSpec((None, MOD_ROWS, tn), lambda l, j: (l, 0, j)),
        out_shape=jax.ShapeDtypeStruct((depth, MOD_ROWS, n), F32),
        compiler_params=_params(("parallel", "parallel")),
        name="mods",
    )(cc, w_mod, b_mod.reshape(depth, 1, n))


def _row_tiled(x, tm):
    if not isinstance(x, tuple):
        return [pl.BlockSpec((tm, x.shape[1]), lambda i: (i, 0))], [x], None
    lat, ctx = x
    nlat = lat.shape[0] // tm
    assert lat.shape[0] % tm == 0 and ctx.shape[0] % tm == 0
    specs = [pl.BlockSpec((tm, lat.shape[1]), lambda i: (jnp.minimum(i, nlat - 1), 0)),
             pl.BlockSpec((tm, ctx.shape[1]), lambda i: (jnp.maximum(i - nlat, 0), 0))]
    return specs, [lat, ctx], nlat


def _tile_rows(refs, nlat, rows):
    if len(refs) == 1:
        return refs[0][rows, :]
    return jnp.where(pl.program_id(0) < nlat, refs[0][rows, :], refs[1][rows, :])


def _norm_mod_kernel(*refs, n_x, nlat, tpb, nb, grow, groups):
    x_refs, (g_ref, sh_ref, sc_ref), rest = refs[:n_x], refs[n_x:n_x + 3], refs[n_x + 3:]
    i = pl.program_id(0)
    sh = _mod_row(sh_ref, i, tpb, nb)
    gmul = g_ref[grow:grow + 1, :] * (1.0 + _mod_row(sc_ref, i, tpb, nb))
    tm = x_refs[0].shape[0]
    norm = lambda rows: (_rms(_tile_rows(x_refs, nlat, rows), gmul) + sh).astype(BF16)
    if groups == 0:
        (o_ref,) = rest
        for r in range(0, tm, ROW_CHUNK):
            o_ref[r:r + ROW_CHUNK, :] = norm(slice(r, r + ROW_CHUNK))
        return
    wcs_ref, ae_ref, be_ref, ao_ref, bo_ref = rest
    hb = jnp.concatenate([norm(slice(r, r + ROW_CHUNK)) for r in range(0, tm, ROW_CHUNK)], axis=0)
    half = tm // 2
    dst = lax.broadcasted_iota(jnp.int32, (tm, tm), 0)
    src = lax.broadcasted_iota(jnp.int32, (tm, tm), 1)
    perm = jnp.where(src == 2 * (dst % half) + dst // half, 1.0, 0.0).astype(BF16)
    hb = jnp.dot(perm, hb, preferred_element_type=F32).astype(BF16)
    gd = hb.shape[1] // groups
    for g in range(groups):
        cols = slice(g * gd, (g + 1) * gd)
        r = jnp.dot(hb[:, cols], wcs_ref[...], preferred_element_type=F32)
        ae_ref[:, cols] = r[:half, :gd].astype(BF16)
        be_ref[:, cols] = r[:half, gd:].astype(BF16)
        ao_ref[:, cols] = r[half:, :gd].astype(BF16)
        bo_ref[:, cols] = r[half:, gd:].astype(BF16)


def _norm_mod(x, norm_g, mods, layer, grow, col, nrows, seq, nb, wcs=None):
    d = norm_g.shape[-1]
    tm = 512
    tpb = seq // tm
    groups = 0 if wcs is None else FNET_GROUPS
    x_specs, x_args, nlat = _row_tiled(x, tm)
    in_specs = x_specs + [
        pl.BlockSpec((None, 4, d), lambda i: (layer, 0, 0)),
        pl.BlockSpec((None, MOD_ROWS, d), lambda i: (layer, 0, col)),
        pl.BlockSpec((None, MOD_ROWS, d), lambda i: (layer, 0, col + 1)),
    ]
    args = x_args + [norm_g, mods, mods]
    out_spec = pl.BlockSpec((tm, d), lambda i: (i, 0))
    out_shape = jax.ShapeDtypeStruct((nrows, d), BF16)
    if wcs is not None:
        in_specs.append(pl.BlockSpec(wcs.shape, lambda i: (0, 0)))
        args.append(wcs)
        out_spec = [pl.BlockSpec((tm // 2, d), lambda i: (i, 0))] * 4
        out_shape = [jax.ShapeDtypeStruct((nrows // 2, d), BF16)] * 4
    return pl.pallas_call(
        functools.partial(_norm_mod_kernel, n_x=len(x_args), nlat=nlat, tpb=tpb, nb=nb, grow=grow,
                          groups=groups),
        grid=(nrows // tm,),
        in_specs=in_specs,
        out_specs=out_spec,
        out_shape=out_shape,
        compiler_params=_params(("parallel",)),
        name="norm_mod",
    )(*args)


def _mm_kernel(a_ref, w_ref, *rest, rope_tiles, q_tiles, q_scale):
    if rope_tiles:
        cos_ref, sin_ref, o_ref, wb_ref = rest
    else:
        o_ref, wb_ref = rest
    j = pl.program_id(0)

    @pl.when(pl.program_id(1) == 0)
    def _():
        wb_ref[...] = w_ref[...].astype(BF16)

    acc = jnp.dot(a_ref[...], wb_ref[...], preferred_element_type=F32)
    if not rope_tiles:
        o_ref[...] = acc.astype(o_ref.dtype)
        return

    @pl.when(j < rope_tiles)
    def _():
        cos = cos_ref[...]
        sin = sin_ref[...]
        scale = jnp.where(j < q_tiles, q_scale, 1.0).astype(F32)
        hd = cos.shape[1]
        for c in range(acc.shape[1] // hd):
            blk = acc[:, c * hd:(c + 1) * hd]
            rot = pltpu.roll(blk, hd // 2, axis=1)
            o_ref[:, c * hd:(c + 1) * hd] = ((blk * cos + rot * sin) * scale).astype(o_ref.dtype)

    @pl.when(j >= rope_tiles)
    def _():
        o_ref[...] = acc.astype(o_ref.dtype)


def _mm(a, w, col0, ncols, out_dtype, rope=None, name="mm"):
    m, k = a.shape
    tm = min(1024, m)
    tn = 1024
    assert m % tm == 0 and ncols % tn == 0 and col0 % tn == 0
    off = col0 // tn
    in_specs = [
        pl.BlockSpec((tm, k), lambda j, i: (i, 0)),
        pl.BlockSpec((k, tn), lambda j, i: (0, j + off)),
    ]
    args = [a, w]
    rope_tiles = q_tiles = 0
    q_scale = 1.0
    if rope is not None:
        cos2, sin2, rope_cols, q_cols, q_scale = rope
        hd = cos2.shape[1]
        in_specs += [pl.BlockSpec((tm, hd), lambda j, i: (i, 0))] * 2
        args += [cos2, sin2]
        rope_tiles, q_tiles = rope_cols // tn, q_cols // tn
    return pl.pallas_call(
        functools.partial(_mm_kernel, rope_tiles=rope_tiles, q_tiles=q_tiles, q_scale=q_scale),
        grid=(ncols // tn, m // tm),
        in_specs=in_specs,
        out_specs=pl.BlockSpec((tm, tn), lambda j, i: (i, j)),
        out_shape=jax.ShapeDtypeStruct((m, ncols), out_dtype),
        scratch_shapes=[pltpu.VMEM((k, tn), BF16)],
        compiler_params=_params(("parallel", "arbitrary")),
        name=name,
    )(*args)


def _mm_res_kernel(*refs, n_a, nlat_a, n_x, nlat_x, tpb, nb, grow):
    a_refs, w_ref, x_refs = refs[:n_a], refs[n_a], refs[n_a + 1:n_a + 1 + n_x]
    g_ref, gate_ref, o_ref = refs[n_a + 1 + n_x:]
    a = _tile_rows(a_refs, nlat_a, slice(None))
    y = jnp.dot(a, w_ref[...], preferred_element_type=F32)
    gout = g_ref[grow:grow + 1, :] * _mod_row(gate_ref, pl.program_id(0), tpb, nb)
    for r in range(0, y.shape[0], ROW_CHUNK):
        rows = slice(r, r + ROW_CHUNK)
        o_ref[rows, :] = _tile_rows(x_refs, nlat_x, rows) + _rms(y[rows], gout)


def _mm_res(a, w, x, norm_g, mods, layer, grow, gate_col, nrows, seq, nb):
    kdim, d = w.shape
    tm = 512
    tpb = seq // tm
    a_specs, a_args, nlat_a = _row_tiled(a, tm)
    x_specs, x_args, nlat_x = _row_tiled(x, tm)
    return pl.pallas_call(
        functools.partial(_mm_res_kernel, n_a=len(a_args), nlat_a=nlat_a, n_x=len(x_args),
                          nlat_x=nlat_x, tpb=tpb, nb=nb, grow=grow),
        grid=(nrows // tm,),
        in_specs=a_specs + [pl.BlockSpec((kdim, d), lambda i: (0, 0))] + x_specs + [
            pl.BlockSpec((None, 4, d), lambda i: (layer, 0, 0)),
            pl.BlockSpec((None, MOD_ROWS, d), lambda i: (layer, 0, gate_col)),
        ],
        out_specs=pl.BlockSpec((tm, d), lambda i: (i, 0)),
        out_shape=jax.ShapeDtypeStruct((nrows, d), F32),
        compiler_params=_params(("parallel",)),
        name="mm_res",
    )(*a_args, w.astype(BF16), *x_args, norm_g, mods)


def _mlp_kernel(x_ref, g_ref, sh_ref, sc_ref, gate_ref, w1_ref, w2_ref, o_ref, h_ref, *, tpb, nb):
    i = pl.program_id(0)
    k = pl.program_id(1)

    chunks = [slice(r, r + ROW_CHUNK) for r in range(0, x_ref.shape[0], ROW_CHUNK)]

    @pl.when(k == 0)
    def _():
        gmul = g_ref[2:3, :] * (1.0 + _mod_row(sc_ref, i, tpb, nb))
        sh = _mod_row(sh_ref, i, tpb, nb)
        for rows in chunks:
            h_ref[rows, :] = (_rms(x_ref[rows, :], gmul) + sh).astype(BF16)
        o_ref[...] = jnp.zeros_like(o_ref)

    u = jnp.dot(h_ref[...], w1_ref[...].astype(BF16), preferred_element_type=F32)
    u = jnp.square(jnp.maximum(u, 0.0)).astype(BF16)
    cn = MLP_OUT_CHUNK
    for c in range(o_ref.shape[1] // cn):
        cols = slice(c * cn, (c + 1) * cn)
        o_ref[:, cols] += jnp.dot(u, w2_ref[:, cols].astype(BF16), preferred_element_type=F32)

    @pl.when(k == pl.num_programs(1) - 1)
    def _():
        gout = g_ref[3:4, :] * _mod_row(gate_ref, i, tpb, nb)
        for rows in chunks:
            o_ref[rows, :] = x_ref[rows, :] + _rms(o_ref[rows, :], gout)


def _mlp(x, w1, w2, norm_g, mods, layer, nrows, seq, nb):
    d, hid = w1.shape[1], w1.shape[2]
    tm = 1024
    th = 512
    tpb = seq // tm
    mod_spec = lambda col: pl.BlockSpec((None, MOD_ROWS, d), lambda i, k: (layer, 0, col))
    return pl.pallas_call(
        functools.partial(_mlp_kernel, tpb=tpb, nb=nb),
        grid=(nrows // tm, hid // th),
        in_specs=[
            pl.BlockSpec((tm, d), lambda i, k: (i, 0)),
            pl.BlockSpec((None, 4, d), lambda i, k: (layer, 0, 0)),
            mod_spec(3), mod_spec(4), mod_spec(5),
            pl.BlockSpec((None, d, th), lambda i, k: (layer, 0, k)),
            pl.BlockSpec((None, th, d), lambda i, k: (layer, k, 0)),
        ],
        out_specs=pl.BlockSpec((tm, d), lambda i, k: (i, 0)),
        out_shape=jax.ShapeDtypeStruct((nrows, d), F32),
        scratch_shapes=[pltpu.VMEM((tm, d), BF16)],
        compiler_params=_params(("parallel", "arbitrary")),
        name="mlp",
    )(x, norm_g, mods, mods, mods, w1, w2)


def _dft_tables(n, rows=None, col0=0, col_step=1, w=32):
    rows = n if rows is None else rows
    ncol = n // col_step
    assert ncol % w == 0
    k = jnp.arange(rows, dtype=jnp.int32)[:, None]
    unit = 2.0 * math.pi / n
    la = col_step * w * jnp.arange(ncol // w, dtype=jnp.int32)[None, :]
    lb = col0 + col_step * jnp.arange(w, dtype=jnp.int32)[None, :]
    ang_a = ((k * la) % n).astype(F32) * unit
    ang_b = ((k * lb) % n).astype(F32) * unit
    ca, sa = jnp.cos(ang_a)[:, :, None], jnp.sin(ang_a)[:, :, None]
    cb, sb = jnp.cos(ang_b)[:, None, :], jnp.sin(ang_b)[:, None, :]
    return (ca * cb - sa * sb).reshape(rows, ncol), (sa * cb + ca * sb).reshape(rows, ncol)


def _posdft_kernel(ce_ref, se_ref, co_ref, so_ref, ae_ref, be_ref, ao_ref, bo_ref, o_ref):
    even = jnp.dot(ce_ref[...], ae_ref[...], preferred_element_type=F32)
    even += jnp.dot(se_ref[...], be_ref[...], preferred_element_type=F32)
    odd = jnp.dot(co_ref[...], ao_ref[...], preferred_element_type=F32)
    odd += jnp.dot(so_ref[...], bo_ref[...], preferred_element_type=F32)
    o_ref[0] = (even + odd).astype(o_ref.dtype)
    o_ref[1] = (even - odd).astype(o_ref.dtype)


def _posdft(parts, nbatch, seq, seq0):
    d = parts[0].shape[1]
    half = seq // 2
    tm = min(256, half)
    tn = 1024
    tables = [t.astype(BF16) for par in (0, 1) for t in _dft_tables(seq, half, par, 2)]
    nj = d // tn
    tab_spec = pl.BlockSpec((tm, half), lambda b, j, m: (m, 0))
    src = pl.BlockSpec((half, tn), lambda b, j, m: (seq0 + b, j))
    y = pl.pallas_call(
        _posdft_kernel,
        grid=(nbatch, nj, half // tm),
        in_specs=[tab_spec] * 4 + [src] * 4,
        out_specs=pl.BlockSpec((None, 2, tm, tn), lambda b, j, m: (b, 0, m, j)),
        out_shape=jax.ShapeDtypeStruct((nbatch, 2, half, d), BF16),
        compiler_params=_params(("parallel", "parallel", "arbitrary")),
        name="posdft",
    )(*tables, *parts)
    return y.reshape(nbatch * seq, d)


def _split3_dot(mask_bf16, v):
    hi = v.astype(BF16)
    r1 = v - hi.astype(F32)
    mid = r1.astype(BF16)
    lo = (r1 - mid.astype(F32)).astype(BF16)
    out = jnp.dot(mask_bf16, hi, preferred_element_type=F32)
    out += jnp.dot(mask_bf16, mid, preferred_element_type=F32)
    out += jnp.dot(mask_bf16, lo, preferred_element_type=F32)
    return out


def _dot_nt(a, b):
    return lax.dot_general(a, b, (((1,), (1,)), ((), ())), preferred_element_type=F32)


def _dot_tn(a, b):
    return lax.dot_general(a, b, (((0,), (0,)), ((), ())), preferred_element_type=F32)


_GLA_HALF = 128


def _block_sum_masks(block, fwd):
    ri = lax.broadcasted_iota(jnp.int32, (_GLA_HALF, _GLA_HALF), 0)
    ci = lax.broadcasted_iota(jnp.int32, (_GLA_HALF, _GLA_HALF), 1)
    same = (ri // block) == (ci // block)
    ahead = (ri - ci) if fwd else (ci - ri)
    m_incl = jnp.where(same & (ahead >= 0), 1.0, 0.0).astype(BF16)
    m_excl = jnp.where(same & (ahead < 0), 1.0, 0.0).astype(BF16)
    return m_incl, m_excl


def _block_sums(mask, x):
    return jnp.concatenate([_split3_dot(mask, x[r0:r0 + _GLA_HALF])
                            for r0 in range(0, x.shape[0], _GLA_HALF)], axis=0)


def _gla_fast(fwd, q, kk, bc, v_ref, st_ref, o_ref):
    cs = GLA_CHUNK
    nchunk = q.shape[0] // cs
    nhead = st_ref.shape[0]
    ri = lax.broadcasted_iota(jnp.int32, (cs, cs), 0)
    ci = lax.broadcasted_iota(jnp.int32, (cs, cs), 1)
    keep = (ci <= ri) if fwd else (ci >= ri)
    heads = [slice(h * HGRN_DK, (h + 1) * HGRN_DK) for h in range(nhead)]
    order = [j if fwd else nchunk - 1 - j for j in range(nchunk)]
    chunks = []
    for c in order:
        rows = slice(c * cs, (c + 1) * cs)
        b = bc[rows]
        tot = b[cs - 1:cs] if fwd else b[0:1]
        mid = 0.5 * tot
        qm = (q[rows] * jnp.exp(b - mid)).astype(BF16)
        km = (kk[rows] * jnp.exp(mid - b)).astype(BF16)
        chunks.append((rows, qm, km, jnp.exp(mid), jnp.exp(tot), jnp.exp(tot - mid)))
    scores = [[_dot_nt(qm[:, hc], km[:, hc]) for hc in heads] for _, qm, km, *_ in chunks]
    updates = [[_dot_tn(v_ref[rows, hc], km[:, hc]) for hc in heads] for rows, _, km, *_ in chunks]
    sts = [st_ref[h] for h in range(nhead)]
    for (rows, qm, km, carry_in, carry_out, new_out), sc, upd in zip(chunks, scores, updates):
        for h, hc in enumerate(heads):
            a = jnp.where(keep, sc[h], 0.0).astype(BF16)
            o = jnp.dot(a, v_ref[rows, hc], preferred_element_type=F32)
            o += _dot_nt(qm[:, hc], (sts[h] * carry_in[:, hc]).astype(BF16))
            o_ref[rows, hc] = o
            sts[h] = sts[h] * carry_out[:, hc] + upd[h] * new_out[:, hc]
    for h in range(nhead):
        st_ref[h] = sts[h]


def _gla_exact(fwd, q, kk, logf, v_ref, st_ref, o_ref):
    seg_rows = q.shape[0]
    nsub = seg_rows // GLA_SUB
    hs = GLA_SUB // 2
    v = v_ref[...].astype(F32)

    m_incl, m_excl = _block_sum_masks(GLA_SUB, fwd)
    bc_all = _block_sums(m_incl, logf)
    qd_all = (q * jnp.exp(bc_all)).astype(BF16)
    kd_all = (kk * jnp.exp(_block_sums(m_excl, logf))).astype(BF16)

    rowh = lax.broadcasted_iota(jnp.int32, (hs, HGRN_DK), 0)
    for h in range(st_ref.shape[0]):
        hc = slice(h * HGRN_DK, (h + 1) * HGRN_DK)
        st = st_ref[h]
        for j in range(nsub):
            i = j if fwd else nsub - 1 - j
            r0 = i * GLA_SUB
            o = _dot_nt(qd_all[r0:r0 + GLA_SUB, hc], st.astype(BF16))
            acc = [o[:hs], o[hs:]]
            for s in range(GLA_SUB):
                sr = r0 + s
                b_s, k_s, v_s = bc_all[sr:sr + 1, hc], kk[sr:sr + 1, hc], v[sr:sr + 1, hc]
                for hh in range(2):
                    lo_row = hh * hs
                    reach_all = (lo_row > s) if fwd else (lo_row + hs - 1 < s)
                    reach_none = (lo_row + hs - 1 < s) if fwd else (lo_row > s)
                    if reach_none:
                        continue
                    rr = slice(r0 + lo_row, r0 + lo_row + hs)
                    diff = bc_all[rr, hc] - b_s
                    if not reach_all:
                        keep = (rowh + lo_row >= s) if fwd else (rowh + lo_row <= s)
                        diff = jnp.where(keep, diff, -1e30)
                    col = jnp.sum(q[rr, hc] * jnp.exp(diff) * k_s, axis=1, keepdims=True)
                    acc[hh] = acc[hh] + col * v_s
            o_ref[r0:r0 + hs, hc] = acc[0]
            o_ref[r0 + hs:r0 + GLA_SUB, hc] = acc[1]
            last = r0 + GLA_SUB - 1 if fwd else r0
            upd = _dot_tn(v_ref[r0:r0 + GLA_SUB, hc], kd_all[r0:r0 + GLA_SUB, hc])
            st = st * jnp.exp(bc_all[last:last + 1, hc]) + upd
        st_ref[h] = st


def _gla_segment(fwd, q_ref, z_ref, v_ref, lb, st_ref, o_ref):
    half_gap = 0.5 * (1.0 - lb)
    f = (lb + half_gap) + half_gap * jnp.tanh(0.5 * z_ref[...])
    logf = jnp.log(f)
    kk = 1.0 - f
    q = q_ref[...].astype(F32)

    bc = _block_sums(_block_sum_masks(GLA_CHUNK, fwd)[0], logf)
    edge = GLA_CHUNK - 1 if fwd else 0
    tot = bc[edge:edge + 1]
    for c in range(1, q.shape[0] // GLA_CHUNK):
        tot = jnp.minimum(tot, bc[c * GLA_CHUNK + edge:c * GLA_CHUNK + edge + 1])
    mild = jnp.max(-tot) <= GLA_FAST_MAX_DECAY

    @pl.when(mild)
    def _():
        _gla_fast(fwd, q, kk, bc, v_ref, st_ref, o_ref)

    @pl.when(jnp.logical_not(mild))
    def _():
        _gla_exact(fwd, q, kk, logf, v_ref, st_ref, o_ref)


def _gla_kernel(q_ref, z_ref, v_ref, g_ref, lbp_ref, gn_ref, y_ref, of_ref, st_ref, o_ref,
                *, layer, nseg):
    t = pl.program_id(2)
    fwd = t < nseg
    seg = t % nseg
    seg_rows = q_ref.shape[0]

    @pl.when(seg == 0)
    def _():
        st_ref[...] = jnp.zeros_like(st_ref)

    p = lbp_ref[...]
    e = jnp.exp(p - jnp.max(p, axis=0, keepdims=True))
    sm = e / jnp.sum(e, axis=0, keepdims=True)
    lb2 = jnp.zeros_like(sm[0])
    for j in range(1, layer + 1):
        lb2 = lb2 + sm[j]

    @pl.when(fwd)
    def _():
        _gla_segment(True, q_ref, z_ref, v_ref, lb2[0:1, :], st_ref, o_ref)
        of_ref[pl.ds(pl.multiple_of(seg * seg_rows, seg_rows), seg_rows), :] = o_ref[...]

    @pl.when(jnp.logical_not(fwd))
    def _():
        _gla_segment(False, q_ref, z_ref, v_ref, lb2[1:2, :], st_ref, o_ref)
        fseg = jnp.where(seg == 0, 0, nseg - seg)
        o = o_ref[...] + of_ref[pl.ds(pl.multiple_of(fseg * seg_rows, seg_rows), seg_rows), :]
        gate = _silu(g_ref[...].astype(F32))
        for h in range(st_ref.shape[0]):
            hc = slice(h * HGRN_DK, (h + 1) * HGRN_DK)
            y_ref[:, hc] = (_rms(o[:, hc], gn_ref[...]) * gate[:, hc]).astype(y_ref.dtype)


def _gla(pq, pz, pvg, hgrn_lb, gnorm, layer, nb, seq, ctx_len):
    t_rows, f = pq.shape
    heads = f // HGRN_DK
    sr = GLA_SEG
    assert ctx_len == sr and seq % sr == 0
    nlat = seq // sr
    nseg = nlat + 1
    lat_blocks = nb * nlat
    depth = hgrn_lb.shape[0]

    def rb(b, t):
        seg = t % nseg
        lat = b * nlat + jnp.where(t < nseg, seg - 1, nlat - seg)
        return jnp.where(seg == 0, lat_blocks + b, lat)

    hp = GLA_HEADS_PER_STEP
    assert heads % hp == 0
    hgroups = heads // hp
    wide = hp * HGRN_DK
    blk = lambda colf: pl.BlockSpec((sr, wide), lambda b, h, t: (rb(b, t), colf(h, t)))
    return pl.pallas_call(
        functools.partial(_gla_kernel, layer=layer, nseg=nseg),
        grid=(nb, hgroups, 2 * nseg),
        in_specs=[
            blk(lambda h, t: h),
            blk(lambda h, t: jnp.where(t < nseg, 0, hgroups) + h),
            blk(lambda h, t: h),
            blk(lambda h, t: hgroups + h),
            pl.BlockSpec((depth, 2, wide), lambda b, h, t: (0, 0, h)),
            pl.BlockSpec((1, HGRN_DK), lambda b, h, t: (0, 0)),
        ],
        out_specs=pl.BlockSpec((sr, wide), lambda b, h, t: (rb(b, jnp.maximum(t, nseg)), h)),
        out_shape=jax.ShapeDtypeStruct((t_rows, f), BF16),
        scratch_shapes=[
            pltpu.VMEM((nseg * sr, wide), F32),
            pltpu.VMEM((hp, HGRN_DK, HGRN_DK), F32),
            pltpu.VMEM((sr, wide), F32),
        ],
        compiler_params=_params(("parallel", "parallel", "arbitrary")),
        name="gla",
    )(pq, pz, pvg, pvg, hgrn_lb, gnorm.reshape(1, HGRN_DK))


def _attn_kernel(q_ref, kl_ref, kc_ref, vl_ref, vc_ref, lam_ref, sub_ref, o_ref, *, lambda_init, nq):
    qi = pl.program_id(2)
    hd = DIFF_HEAD_DIM
    nhead = o_ref.shape[1] // (2 * hd)
    lp = lam_ref[...]
    lam = (jnp.exp(jnp.sum(lp[0:1, :] * lp[1:2, :], keepdims=True))
           - jnp.exp(jnp.sum(lp[2:3, :] * lp[3:4, :], keepdims=True)) + lambda_init)

    def probs(sc, sl):
        m = jnp.max(sc, axis=-1, keepdims=True)
        if sl is not None:
            m = jnp.maximum(m, jnp.max(sl, axis=-1, keepdims=True))
        pc = jnp.exp2(sc - m)
        den = jnp.sum(pc, axis=-1, keepdims=True)
        pl_ = None
        if sl is not None:
            pl_ = jnp.exp2(sl - m)
            den = den + jnp.sum(pl_, axis=-1, keepdims=True)
            pl_ = pl_.astype(BF16)
        return pc.astype(BF16), pl_, 1.0 / den

    def attend(with_latent_keys):
        cols = [slice(c * hd, (c + 1) * hd) for c in range(2 * nhead)]
        sc = [_dot_nt(q_ref[:, c], kc_ref[:, c]) for c in cols]
        sl = [_dot_nt(q_ref[:, c], kl_ref[:, c]) if with_latent_keys else None for c in cols]
        for h in range(nhead):
            vcols = slice(2 * h * hd, 2 * (h + 1) * hd)
            outs = []
            for mp in range(2):
                pc, pl_, inv = probs(sc[2 * h + mp], sl[2 * h + mp])
                o = jnp.dot(pc, vc_ref[:, vcols], preferred_element_type=F32)
                if pl_ is not None:
                    o += jnp.dot(pl_, vl_ref[:, vcols], preferred_element_type=F32)
                outs.append(o * (inv if mp == 0 else lam * inv))
            y = _rms(outs[0] - outs[1], sub_ref[...]) * (1.0 - lambda_init)
            o_ref[:, vcols] = y.astype(o_ref.dtype)

    @pl.when(qi < nq)
    def _():
        attend(True)

    @pl.when(qi == nq)
    def _():
        attend(False)


def _attn(qkv, lam_p, subln, lambda_init, nb, seq, ctx_len):
    t_rows, d3 = qkv.shape
    d = d3 // 3
    wide = ATTN_HEADS_PER_STEP * 2 * DIFF_HEAD_DIM
    assert d % wide == 0
    hgroups = d // wide
    tq = ctx_len
    nq = seq // tq
    lat_qblocks = nb * nq
    qrow = lambda b, qi: jnp.where(qi == nq, lat_qblocks + b, b * nq + qi)
    lat = lambda part: pl.BlockSpec((seq, wide), lambda b, h, qi: (b, part * hgroups + h))
    ctx = lambda part: pl.BlockSpec((ctx_len, wide), lambda b, h, qi: (lat_qblocks + b, part * hgroups + h))
    return pl.pallas_call(
        functools.partial(_attn_kernel, lambda_init=lambda_init, nq=nq),
        grid=(nb, hgroups, nq + 1),
        in_specs=[
            pl.BlockSpec((tq, wide), lambda b, h, qi: (qrow(b, qi), h)),
            lat(1), ctx(1), lat(2), ctx(2),
            pl.BlockSpec((4, DIFF_HEAD_DIM), lambda b, h, qi: (0, 0)),
            pl.BlockSpec((1, 2 * DIFF_HEAD_DIM), lambda b, h, qi: (0, 0)),
        ],
        out_specs=pl.BlockSpec((tq, wide), lambda b, h, qi: (qrow(b, qi), h)),
        out_shape=jax.ShapeDtypeStruct((t_rows, d), BF16),
        compiler_params=_params(("parallel", "parallel", "arbitrary")),
        name="diff_attn",
    )(qkv, qkv, qkv, qkv, qkv, lam_p, subln.reshape(1, 2 * DIFF_HEAD_DIM))


def _rope_tables(seq, nb, ctx_rows):
    rows = seq // GRID_W
    row = jnp.repeat(jnp.arange(rows, dtype=F32), GRID_W)
    col = jnp.tile(jnp.arange(GRID_W, dtype=F32), rows)
    n_freq = DIFF_HEAD_DIM // 4
    inv = ROPE_BASE ** (-jnp.arange(n_freq, dtype=F32) / n_freq)
    ang = jnp.concatenate([row[:, None] * inv, col[:, None] * inv], axis=-1)
    cos, sin = jnp.cos(ang), jnp.sin(ang)
    cos2 = jnp.concatenate([cos, cos], axis=-1)
    sin2 = jnp.concatenate([-sin, sin], axis=-1)
    cos2 = jnp.concatenate([jnp.tile(cos2, (nb, 1)), jnp.ones((ctx_rows, DIFF_HEAD_DIM), F32)], axis=0)
    sin2 = jnp.concatenate([jnp.tile(sin2, (nb, 1)), jnp.zeros((ctx_rows, DIFF_HEAD_DIM), F32)], axis=0)
    return cos2, sin2


def kernel(x, c, ctx, c_ctx, w_mod, b_mod, norm_g, w_mlp_in, w_mlp_out, fnet_w_out, hgrn_w_in,
           hgrn_lb, hgrn_gnorm, hgrn_w_out, diff_w_qkv, diff_lambda, diff_subln, diff_w_out):
    nb, seq, d = x.shape
    ctx_len = ctx.shape[1]
    depth = w_mod.shape[0]
    lat_rows, ctx_rows = nb * seq, nb * ctx_len
    assert nb + 1 <= MOD_ROWS

    xs = (x.reshape(lat_rows, d), ctx.reshape(ctx_rows, d))
    cc = jnp.concatenate([c, c_ctx[None, :], jnp.zeros((MOD_ROWS - nb - 1, d), F32)], axis=0)
    mods = _mods(cc, w_mod, b_mod)

    for i in range(depth):
        mixer, slot = i % N_MIXERS, i // N_MIXERS
        ctx_out = i < depth - 1
        ctx_in = ctx_out or mixer != 0
        assert ctx_out or mixer == 0
        nrows = lat_rows + ctx_rows if ctx_in else lat_rows
        common = dict(nrows=nrows, seq=seq, nb=nb)
        if mixer == 0:
            gd = d // FNET_GROUPS
            cg, sg = _dft_tables(gd)
            wcs = jnp.concatenate([cg, -sg], axis=1).astype(BF16)
            parts = _norm_mod(xs, norm_g, mods, i, 0, 0, wcs=wcs, **common)
            y = _posdft(parts, nb, seq, 0)
            if ctx_in:
                y = (y, _posdft(parts, nb, ctx_len, lat_rows // ctx_len))
            w_out = fnet_w_out[slot]
        elif mixer == 1:
            h = _norm_mod(xs, norm_g, mods, i, 0, 0, **common)
            w_in = hgrn_w_in[slot]
            f = hgrn_lb.shape[-1]
            pq = _mm(h, w_in, 0, f, BF16, name="hgrn_q")
            pz = _mm(h, w_in, f, 2 * f, F32, name="hgrn_z")
            pvg = _mm(h, w_in, 3 * f, 2 * d, BF16, name="hgrn_vg")
            y = _gla(pq, pz, pvg, hgrn_lb, hgrn_gnorm[slot], i, nb, seq, ctx_len)
            w_out = hgrn_w_out[slot]
        else:
            h = _norm_mod(xs, norm_g, mods, i, 0, 0, **common)
            cos2, sin2 = _rope_tables(seq, nb, ctx_rows)
            rope = (cos2, sin2, 2 * d, d, DIFF_HEAD_DIM ** -0.5 * math.log2(math.e))
            qkv = _mm(h, diff_w_qkv[slot], 0, 3 * d, BF16, rope=rope, name="diff_qkv")
            lambda_init = 0.8 - 0.6 * math.exp(-0.3 * i)
            y = _attn(qkv, diff_lambda[slot], diff_subln[slot], lambda_init, nb, seq, ctx_len)
            w_out = diff_w_out[slot]
        xs = _mm_res(y, w_out, xs, norm_g, mods, i, 1, 2, **common)
        xs = _mlp(xs, w_mlp_in, w_mlp_out, norm_g, mods, i, **common)
    return xs[:lat_rows].reshape(nb, seq, d)
```

```python
import functools
import math

import jax
import jax.numpy as jnp
from jax import lax
from jax.experimental import pallas as pl
from jax.experimental.pallas import tpu as pltpu

F32 = jnp.float32
BF16 = jnp.bfloat16

NORM_EPS = 1e-6
ROPE_BASE = 10000.0
GRID_W = 64
N_MOD = 6
N_MIXERS = 3
FNET_GROUPS = 8
HGRN_DK = 128
DIFF_HEAD_DIM = 128
MOD_ROWS = 8

GLA_SUB = 16
GLA_CHUNK = 64
GLA_HEADS_PER_STEP = 8
ATTN_HEADS_PER_STEP = 2
GLA_FAST_MAX_DECAY = 150.0
GLA_SEG = 256
MLP_OUT_CHUNK = 512
ROW_CHUNK = 32
MASKED_LOG_DECAY = -1e30

MODS_TN = 1024
ROW_TM = 512
MM_TM, MM_TN = 1024, 1024
MLP_TM, MLP_TH = 1024, 512
DFT_TM, DFT_TN = 256, 1024
V7X_VMEM_LIMIT = 56 * 1024 * 1024


def _params(sem, vmem=V7X_VMEM_LIMIT):
    return pltpu.CompilerParams(dimension_semantics=sem, vmem_limit_bytes=vmem)


def _silu(v):
    return v * jax.nn.sigmoid(v)


def _rms(y, gamma):
    return y * lax.rsqrt(jnp.mean(y * y, axis=-1, keepdims=True) + NORM_EPS) * gamma


def _mod_row(ref, tile, tiles_per_batch, n_batch):
    r = jnp.minimum(tile // tiles_per_batch, n_batch)
    return ref[pl.ds(r, 1), :]


def _mods_kernel(c_ref, w_ref, b_ref, o_ref):
    a = _silu(c_ref[...]).astype(BF16)
    o_ref[...] = jnp.dot(a, w_ref[...].astype(BF16), preferred_element_type=F32) + b_ref[...]


def _mods(cc, w_mod, b_mod):
    depth, d, n = w_mod.shape
    tn = MODS_TN
    return pl.pallas_call(
        _mods_kernel,
        grid=(depth, n // tn),
        in_specs=[
            pl.BlockSpec((MOD_ROWS, d), lambda l, j: (0, 0)),
            pl.BlockSpec((None, d, tn), lambda l, j: (l, 0, j)),
            pl.BlockSpec((None, 1, tn), lambda l, j: (l, 0, j)),
        ],
        out_specs=pl.BlockSpec((None, MOD_ROWS, tn), lambda l, j: (l, 0, j)),
        out_shape=jax.ShapeDtypeStruct((depth, MOD_ROWS, n), F32),
        compiler_params=_params(("parallel", "parallel")),
        name="mods",
    )(cc, w_mod, b_mod.reshape(depth, 1, n))


def _row_tiled(x, tm):
    if not isinstance(x, tuple):
        return [pl.BlockSpec((tm, x.shape[1]), lambda i: (i, 0))], [x], None
    lat, ctx = x
    nlat = lat.shape[0] // tm
    assert lat.shape[0] % tm == 0 and ctx.shape[0] % tm == 0
    specs = [pl.BlockSpec((tm, lat.shape[1]), lambda i: (jnp.minimum(i, nlat - 1), 0)),
             pl.BlockSpec((tm, ctx.shape[1]), lambda i: (jnp.maximum(i - nlat, 0), 0))]
    return specs, [lat, ctx], nlat


def _tile_rows(refs, nlat, rows):
    if len(refs) == 1:
        return refs[0][rows, :]
    return jnp.where(pl.program_id(0) < nlat, refs[0][rows, :], refs[1][rows, :])


def _norm_mod_kernel(*refs, n_x, nlat, tpb, nb, grow, groups):
    x_refs, (g_ref, sh_ref, sc_ref), rest = refs[:n_x], refs[n_x:n_x + 3], refs[n_x + 3:]
    i = pl.program_id(0)
    sh = _mod_row(sh_ref, i, tpb, nb)
    gmul = g_ref[grow:grow + 1, :] * (1.0 + _mod_row(sc_ref, i, tpb, nb))
    tm = x_refs[0].shape[0]
    norm = lambda rows: (_rms(_tile_rows(x_refs, nlat, rows), gmul) + sh).astype(BF16)
    if groups == 0:
        (o_ref,) = rest
        for r in range(0, tm, ROW_CHUNK):
            o_ref[r:r + ROW_CHUNK, :] = norm(slice(r, r + ROW_CHUNK))
        return
    wcs_ref, ae_ref, be_ref, ao_ref, bo_ref = rest
    hb = jnp.concatenate([norm(slice(r, r + ROW_CHUNK)) for r in range(0, tm, ROW_CHUNK)], axis=0)
    half = tm // 2
    dst = lax.broadcasted_iota(jnp.int32, (tm, tm), 0)
    src = lax.broadcasted_iota(jnp.int32, (tm, tm), 1)
    perm = jnp.where(src == 2 * (dst % half) + dst // half, 1.0, 0.0).astype(BF16)
    hb = jnp.dot(perm, hb, preferred_element_type=F32).astype(BF16)
    gd = hb.shape[1] // groups
    for g in range(groups):
        cols = slice(g * gd, (g + 1) * gd)
        r = jnp.dot(hb[:, cols], wcs_ref[...], preferred_element_type=F32)
        ae_ref[:, cols] = r[:half, :gd].astype(BF16)
        be_ref[:, cols] = r[:half, gd:].astype(BF16)
        ao_ref[:, cols] = r[half:, :gd].astype(BF16)
        bo_ref[:, cols] = r[half:, gd:].astype(BF16)


def _norm_mod(x, norm_g, mods, layer, grow, col, nrows, seq, nb, wcs=None):
    d = norm_g.shape[-1]
    tm = ROW_TM
    tpb = seq // tm
    groups = 0 if wcs is None else FNET_GROUPS
    x_specs, x_args, nlat = _row_tiled(x, tm)
    in_specs = x_specs + [
        pl.BlockSpec((None, 4, d), lambda i: (layer, 0, 0)),
        pl.BlockSpec((None, MOD_ROWS, d), lambda i: (layer, 0, col)),
        pl.BlockSpec((None, MOD_ROWS, d), lambda i: (layer, 0, col + 1)),
    ]
    args = x_args + [norm_g, mods, mods]
    out_spec = pl.BlockSpec((tm, d), lambda i: (i, 0))
    out_shape = jax.ShapeDtypeStruct((nrows, d), BF16)
    if wcs is not None:
        in_specs.append(pl.BlockSpec(wcs.shape, lambda i: (0, 0)))
        args.append(wcs)
        out_spec = [pl.BlockSpec((tm // 2, d), lambda i: (i, 0))] * 4
        out_shape = [jax.ShapeDtypeStruct((nrows // 2, d), BF16)] * 4
    return pl.pallas_call(
        functools.partial(_norm_mod_kernel, n_x=len(x_args), nlat=nlat, tpb=tpb, nb=nb, grow=grow,
                          groups=groups),
        grid=(nrows // tm,),
        in_specs=in_specs,
        out_specs=out_spec,
        out_shape=out_shape,
        compiler_params=_params(("parallel",)),
        name="norm_mod",
    )(*args)


def _mm_kernel(a_ref, w_ref, *rest, rope, q_tiles, q_scale):
    if rope:
        cos_ref, sin_ref, o_ref, wb_ref = rest
    else:
        o_ref, wb_ref = rest

    @pl.when(pl.program_id(1) == 0)
    def _():
        wb_ref[...] = w_ref[...].astype(BF16)

    acc = jnp.dot(a_ref[...], wb_ref[...], preferred_element_type=F32)
    if not rope:
        o_ref[...] = acc.astype(o_ref.dtype)
        return
    cos = cos_ref[...]
    sin = sin_ref[...]
    scale = jnp.where(pl.program_id(0) < q_tiles, q_scale, 1.0).astype(F32)
    hd = cos.shape[1]
    for c in range(acc.shape[1] // hd):
        blk = acc[:, c * hd:(c + 1) * hd]
        rot = pltpu.roll(blk, hd // 2, axis=1)
        o_ref[:, c * hd:(c + 1) * hd] = ((blk * cos + rot * sin) * scale).astype(o_ref.dtype)


def _mm(a, w, col0, ncols, out_dtype, rope=None, name="mm"):
    m, k = a.shape
    tm = min(MM_TM, m)
    tn = MM_TN
    assert m % tm == 0 and ncols % tn == 0 and col0 % tn == 0
    off = col0 // tn
    in_specs = [
        pl.BlockSpec((tm, k), lambda j, i: (i, 0)),
        pl.BlockSpec((k, tn), lambda j, i: (0, j + off)),
    ]
    args = [a, w]
    q_tiles = 0
    q_scale = 1.0
    if rope is not None:
        cos2, sin2, q_cols, q_scale = rope
        hd = cos2.shape[1]
        in_specs += [pl.BlockSpec((tm, hd), lambda j, i: (i, 0))] * 2
        args += [cos2, sin2]
        q_tiles = q_cols // tn
    return pl.pallas_call(
        functools.partial(_mm_kernel, rope=rope is not None, q_tiles=q_tiles, q_scale=q_scale),
        grid=(ncols // tn, m // tm),
        in_specs=in_specs,
        out_specs=pl.BlockSpec((tm, tn), lambda j, i: (i, j)),
        out_shape=jax.ShapeDtypeStruct((m, ncols), out_dtype),
        scratch_shapes=[pltpu.VMEM((k, tn), BF16)],
        compiler_params=_params(("parallel", "arbitrary")),
        name=name,
    )(*args)


def _mm_res_kernel(*refs, n_a, nlat_a, n_x, nlat_x, tpb, nb, grow):
    a_refs, w_ref, x_refs = refs[:n_a], refs[n_a], refs[n_a + 1:n_a + 1 + n_x]
    g_ref, gate_ref, o_ref = refs[n_a + 1 + n_x:]
    a = _tile_rows(a_refs, nlat_a, slice(None))
    y = jnp.dot(a, w_ref[...], preferred_element_type=F32)
    gout = g_ref[grow:grow + 1, :] * _mod_row(gate_ref, pl.program_id(0), tpb, nb)
    for r in range(0, y.shape[0], ROW_CHUNK):
        rows = slice(r, r + ROW_CHUNK)
        o_ref[rows, :] = _tile_rows(x_refs, nlat_x, rows) + _rms(y[rows], gout)


def _mm_res(a, w, x, norm_g, mods, layer, grow, gate_col, nrows, seq, nb):
    kdim, d = w.shape
    tm = ROW_TM
    tpb = seq // tm
    a_specs, a_args, nlat_a = _row_tiled(a, tm)
    x_specs, x_args, nlat_x = _row_tiled(x, tm)
    return pl.pallas_call(
        functools.partial(_mm_res_kernel, n_a=len(a_args), nlat_a=nlat_a, n_x=len(x_args),
                          nlat_x=nlat_x, tpb=tpb, nb=nb, grow=grow),
        grid=(nrows // tm,),
        in_specs=a_specs + [pl.BlockSpec((kdim, d), lambda i: (0, 0))] + x_specs + [
            pl.BlockSpec((None, 4, d), lambda i: (layer, 0, 0)),
            pl.BlockSpec((None, MOD_ROWS, d), lambda i: (layer, 0, gate_col)),
        ],
        out_specs=pl.BlockSpec((tm, d), lambda i: (i, 0)),
        out_shape=jax.ShapeDtypeStruct((nrows, d), F32),
        compiler_params=_params(("parallel",)),
        name="mm_res",
    )(*a_args, w.astype(BF16), *x_args, norm_g, mods)


def _mlp_kernel(x_ref, g_ref, sh_ref, sc_ref, gate_ref, w1_ref, w2_ref, o_ref, h_ref, *, tpb, nb):
    i = pl.program_id(0)
    k = pl.program_id(1)

    chunks = [slice(r, r + ROW_CHUNK) for r in range(0, x_ref.shape[0], ROW_CHUNK)]

    @pl.when(k == 0)
    def _():
        gmul = g_ref[2:3, :] * (1.0 + _mod_row(sc_ref, i, tpb, nb))
        sh = _mod_row(sh_ref, i, tpb, nb)
        for rows in chunks:
            h_ref[rows, :] = (_rms(x_ref[rows, :], gmul) + sh).astype(BF16)
        o_ref[...] = jnp.zeros_like(o_ref)

    u = jnp.dot(h_ref[...], w1_ref[...].astype(BF16), preferred_element_type=F32)
    u = jnp.square(jnp.maximum(u, 0.0)).astype(BF16)
    cn = MLP_OUT_CHUNK
    for c in range(o_ref.shape[1] // cn):
        cols = slice(c * cn, (c + 1) * cn)
        o_ref[:, cols] += jnp.dot(u, w2_ref[:, cols].astype(BF16), preferred_element_type=F32)

    @pl.when(k == pl.num_programs(1) - 1)
    def _():
        gout = g_ref[3:4, :] * _mod_row(gate_ref, i, tpb, nb)
        for rows in chunks:
            o_ref[rows, :] = x_ref[rows, :] + _rms(o_ref[rows, :], gout)


def _mlp(x, w1, w2, norm_g, mods, layer, nrows, seq, nb):
    d, hid = w1.shape[1], w1.shape[2]
    tm = MLP_TM
    th = MLP_TH
    tpb = seq // tm
    mod_spec = lambda col: pl.BlockSpec((None, MOD_ROWS, d), lambda i, k: (layer, 0, col))
    return pl.pallas_call(
        functools.partial(_mlp_kernel, tpb=tpb, nb=nb),
        grid=(nrows // tm, hid // th),
        in_specs=[
            pl.BlockSpec((tm, d), lambda i, k: (i, 0)),
            pl.BlockSpec((None, 4, d), lambda i, k: (layer, 0, 0)),
            mod_spec(3), mod_spec(4), mod_spec(5),
            pl.BlockSpec((None, d, th), lambda i, k: (layer, 0, k)),
            pl.BlockSpec((None, th, d), lambda i, k: (layer, k, 0)),
        ],
        out_specs=pl.BlockSpec((tm, d), lambda i, k: (i, 0)),
        out_shape=jax.ShapeDtypeStruct((nrows, d), F32),
        scratch_shapes=[pltpu.VMEM((tm, d), BF16)],
        compiler_params=_params(("parallel", "arbitrary")),
        name="mlp",
    )(x, norm_g, mods, mods, mods, w1, w2)


def _dft_tables(n, rows=None, col0=0, col_step=1, w=32):
    rows = n if rows is None else rows
    ncol = n // col_step
    assert ncol % w == 0
    k = jnp.arange(rows, dtype=jnp.int32)[:, None]
    unit = 2.0 * math.pi / n
    la = col_step * w * jnp.arange(ncol // w, dtype=jnp.int32)[None, :]
    lb = col0 + col_step * jnp.arange(w, dtype=jnp.int32)[None, :]
    ang_a = ((k * la) % n).astype(F32) * unit
    ang_b = ((k * lb) % n).astype(F32) * unit
    ca, sa = jnp.cos(ang_a)[:, :, None], jnp.sin(ang_a)[:, :, None]
    cb, sb = jnp.cos(ang_b)[:, None, :], jnp.sin(ang_b)[:, None, :]
    return (ca * cb - sa * sb).reshape(rows, ncol), (sa * cb + ca * sb).reshape(rows, ncol)


def _posdft_kernel(ce_ref, se_ref, co_ref, so_ref, ae_ref, be_ref, ao_ref, bo_ref, o_ref):
    even = jnp.dot(ce_ref[...], ae_ref[...], preferred_element_type=F32)
    even += jnp.dot(se_ref[...], be_ref[...], preferred_element_type=F32)
    odd = jnp.dot(co_ref[...], ao_ref[...], preferred_element_type=F32)
    odd += jnp.dot(so_ref[...], bo_ref[...], preferred_element_type=F32)
    o_ref[0] = (even + odd).astype(o_ref.dtype)
    o_ref[1] = (even - odd).astype(o_ref.dtype)


def _posdft(parts, nbatch, seq, seq0):
    d = parts[0].shape[1]
    half = seq // 2
    tm = min(DFT_TM, half)
    tn = DFT_TN
    tables = [t.astype(BF16) for par in (0, 1) for t in _dft_tables(seq, half, par, 2)]
    nj = d // tn
    tab_spec = pl.BlockSpec((tm, half), lambda b, j, m: (m, 0))
    src = pl.BlockSpec((half, tn), lambda b, j, m: (seq0 + b, j))
    y = pl.pallas_call(
        _posdft_kernel,
        grid=(nbatch, nj, half // tm),
        in_specs=[tab_spec] * 4 + [src] * 4,
        out_specs=pl.BlockSpec((None, 2, tm, tn), lambda b, j, m: (b, 0, m, j)),
        out_shape=jax.ShapeDtypeStruct((nbatch, 2, half, d), BF16),
        compiler_params=_params(("parallel", "parallel", "arbitrary")),
        name="posdft",
    )(*tables, *parts)
    return y.reshape(nbatch * seq, d)


def _split3_dot(mask_bf16, v):
    hi = v.astype(BF16)
    r1 = v - hi.astype(F32)
    mid = r1.astype(BF16)
    lo = (r1 - mid.astype(F32)).astype(BF16)
    out = jnp.dot(mask_bf16, hi, preferred_element_type=F32)
    out += jnp.dot(mask_bf16, mid, preferred_element_type=F32)
    out += jnp.dot(mask_bf16, lo, preferred_element_type=F32)
    return out


def _dot_nt(a, b):
    return lax.dot_general(a, b, (((1,), (1,)), ((), ())), preferred_element_type=F32)


def _dot_tn(a, b):
    return lax.dot_general(a, b, (((0,), (0,)), ((), ())), preferred_element_type=F32)


_GLA_HALF = 128


def _block_sum_masks(block, fwd):
    ri = lax.broadcasted_iota(jnp.int32, (_GLA_HALF, _GLA_HALF), 0)
    ci = lax.broadcasted_iota(jnp.int32, (_GLA_HALF, _GLA_HALF), 1)
    same = (ri // block) == (ci // block)
    ahead = (ri - ci) if fwd else (ci - ri)
    m_incl = jnp.where(same & (ahead >= 0), 1.0, 0.0).astype(BF16)
    m_excl = jnp.where(same & (ahead < 0), 1.0, 0.0).astype(BF16)
    return m_incl, m_excl


def _block_sums(mask, x):
    return jnp.concatenate([_split3_dot(mask, x[r0:r0 + _GLA_HALF])
                            for r0 in range(0, x.shape[0], _GLA_HALF)], axis=0)


def _gla_fast(fwd, q, kk, bc, v_ref, st_ref, o_ref):
    cs = GLA_CHUNK
    nchunk = q.shape[0] // cs
    nhead = st_ref.shape[0]
    ri = lax.broadcasted_iota(jnp.int32, (cs, cs), 0)
    ci = lax.broadcasted_iota(jnp.int32, (cs, cs), 1)
    keep = (ci <= ri) if fwd else (ci >= ri)
    heads = [slice(h * HGRN_DK, (h + 1) * HGRN_DK) for h in range(nhead)]
    order = [j if fwd else nchunk - 1 - j for j in range(nchunk)]
    chunks = []
    for c in order:
        rows = slice(c * cs, (c + 1) * cs)
        b = bc[rows]
        tot = b[cs - 1:cs] if fwd else b[0:1]
        mid = 0.5 * tot
        qm = (q[rows] * jnp.exp(b - mid)).astype(BF16)
        km = (kk[rows] * jnp.exp(mid - b)).astype(BF16)
        chunks.append((rows, qm, km, jnp.exp(mid), jnp.exp(tot), jnp.exp(tot - mid)))
    scores = [[_dot_nt(qm[:, hc], km[:, hc]) for hc in heads] for _, qm, km, *_ in chunks]
    updates = [[_dot_tn(v_ref[rows, hc], km[:, hc]) for hc in heads] for rows, _, km, *_ in chunks]
    sts = [st_ref[h] for h in range(nhead)]
    for (rows, qm, km, carry_in, carry_out, new_out), sc, upd in zip(chunks, scores, updates):
        for h, hc in enumerate(heads):
            a = jnp.where(keep, sc[h], 0.0).astype(BF16)
            o = jnp.dot(a, v_ref[rows, hc], preferred_element_type=F32)
            o += _dot_nt(qm[:, hc], (sts[h] * carry_in[:, hc]).astype(BF16))
            o_ref[rows, hc] = o
            sts[h] = sts[h] * carry_out[:, hc] + upd[h] * new_out[:, hc]
    for h in range(nhead):
        st_ref[h] = sts[h]


def _gla_exact(fwd, q, kk, logf, v_ref, st_ref, o_ref):
    seg_rows = q.shape[0]
    nsub = seg_rows // GLA_SUB
    hs = GLA_SUB // 2
    v = v_ref[...].astype(F32)

    m_incl, m_excl = _block_sum_masks(GLA_SUB, fwd)
    bc_all = _block_sums(m_incl, logf)
    qd_all = (q * jnp.exp(bc_all)).astype(BF16)
    kd_all = (kk * jnp.exp(_block_sums(m_excl, logf))).astype(BF16)

    rowh = lax.broadcasted_iota(jnp.int32, (hs, HGRN_DK), 0)
    for h in range(st_ref.shape[0]):
        hc = slice(h * HGRN_DK, (h + 1) * HGRN_DK)
        st = st_ref[h]
        for j in range(nsub):
            i = j if fwd else nsub - 1 - j
            r0 = i * GLA_SUB
            o = _dot_nt(qd_all[r0:r0 + GLA_SUB, hc], st.astype(BF16))
            acc = [o[:hs], o[hs:]]
            for s in range(GLA_SUB):
                sr = r0 + s
                b_s, k_s, v_s = bc_all[sr:sr + 1, hc], kk[sr:sr + 1, hc], v[sr:sr + 1, hc]
                for hh in range(2):
                    lo_row = hh * hs
                    reach_all = (lo_row > s) if fwd else (lo_row + hs - 1 < s)
                    reach_none = (lo_row + hs - 1 < s) if fwd else (lo_row > s)
                    if reach_none:
                        continue
                    rr = slice(r0 + lo_row, r0 + lo_row + hs)
                    diff = bc_all[rr, hc] - b_s
                    if not reach_all:
                        keep = (rowh + lo_row >= s) if fwd else (rowh + lo_row <= s)
                        diff = jnp.where(keep, diff, MASKED_LOG_DECAY)
                    col = jnp.sum(q[rr, hc] * jnp.exp(diff) * k_s, axis=1, keepdims=True)
                    acc[hh] = acc[hh] + col * v_s
            o_ref[r0:r0 + hs, hc] = acc[0]
            o_ref[r0 + hs:r0 + GLA_SUB, hc] = acc[1]
            last = r0 + GLA_SUB - 1 if fwd else r0
            upd = _dot_tn(v_ref[r0:r0 + GLA_SUB, hc], kd_all[r0:r0 + GLA_SUB, hc])
            st = st * jnp.exp(bc_all[last:last + 1, hc]) + upd
        st_ref[h] = st


def _gla_segment(fwd, q_ref, z_ref, v_ref, lb, st_ref, o_ref):
    half_gap = 0.5 * (1.0 - lb)
    f = (lb + half_gap) + half_gap * jnp.tanh(0.5 * z_ref[...])
    logf = jnp.log(f)
    kk = 1.0 - f
    q = q_ref[...].astype(F32)

    bc = _block_sums(_block_sum_masks(GLA_CHUNK, fwd)[0], logf)
    edge = GLA_CHUNK - 1 if fwd else 0
    tot = bc[edge:edge + 1]
    for c in range(1, q.shape[0] // GLA_CHUNK):
        tot = jnp.minimum(tot, bc[c * GLA_CHUNK + edge:c * GLA_CHUNK + edge + 1])
    mild = jnp.max(-tot) <= GLA_FAST_MAX_DECAY

    @pl.when(mild)
    def _():
        _gla_fast(fwd, q, kk, bc, v_ref, st_ref, o_ref)

    @pl.when(jnp.logical_not(mild))
    def _():
        _gla_exact(fwd, q, kk, logf, v_ref, st_ref, o_ref)


def _gla_kernel(q_ref, z_ref, v_ref, g_ref, lbp_ref, gn_ref, y_ref, of_ref, st_ref, o_ref,
                *, layer, nseg):
    t = pl.program_id(2)
    fwd = t < nseg
    seg = t % nseg
    seg_rows = q_ref.shape[0]

    @pl.when(seg == 0)
    def _():
        st_ref[...] = jnp.zeros_like(st_ref)

    p = lbp_ref[...]
    e = jnp.exp(p - jnp.max(p, axis=0, keepdims=True))
    sm = e / jnp.sum(e, axis=0, keepdims=True)
    lb2 = jnp.zeros_like(sm[0])
    for j in range(1, layer + 1):
        lb2 = lb2 + sm[j]

    @pl.when(fwd)
    def _():
        _gla_segment(True, q_ref, z_ref, v_ref, lb2[0:1, :], st_ref, o_ref)
        of_ref[pl.ds(pl.multiple_of(seg * seg_rows, seg_rows), seg_rows), :] = o_ref[...]

    @pl.when(jnp.logical_not(fwd))
    def _():
        _gla_segment(False, q_ref, z_ref, v_ref, lb2[1:2, :], st_ref, o_ref)
        fseg = jnp.where(seg == 0, 0, nseg - seg)
        o = o_ref[...] + of_ref[pl.ds(pl.multiple_of(fseg * seg_rows, seg_rows), seg_rows), :]
        gate = _silu(g_ref[...].astype(F32))
        for h in range(st_ref.shape[0]):
            hc = slice(h * HGRN_DK, (h + 1) * HGRN_DK)
            y_ref[:, hc] = (_rms(o[:, hc], gn_ref[...]) * gate[:, hc]).astype(y_ref.dtype)


def _gla(pq, pz, pvg, hgrn_lb, gnorm, layer, nb, seq, ctx_len):
    t_rows, f = pq.shape
    heads = f // HGRN_DK
    sr = GLA_SEG
    assert ctx_len == sr and seq % sr == 0
    nlat = seq // sr
    nseg = nlat + 1
    lat_blocks = nb * nlat
    depth = hgrn_lb.shape[0]

    def rb(b, t):
        seg = t % nseg
        lat = b * nlat + jnp.where(t < nseg, seg - 1, nlat - seg)
        return jnp.where(seg == 0, lat_blocks + b, lat)

    hp = GLA_HEADS_PER_STEP
    assert heads % hp == 0
    hgroups = heads // hp
    wide = hp * HGRN_DK
    blk = lambda colf: pl.BlockSpec((sr, wide), lambda b, h, t: (rb(b, t), colf(h, t)))
    return pl.pallas_call(
        functools.partial(_gla_kernel, layer=layer, nseg=nseg),
        grid=(nb, hgroups, 2 * nseg),
        in_specs=[
            blk(lambda h, t: h),
            blk(lambda h, t: jnp.where(t < nseg, 0, hgroups) + h),
            blk(lambda h, t: h),
            blk(lambda h, t: hgroups + h),
            pl.BlockSpec((depth, 2, wide), lambda b, h, t: (0, 0, h)),
            pl.BlockSpec((1, HGRN_DK), lambda b, h, t: (0, 0)),
        ],
        out_specs=pl.BlockSpec((sr, wide), lambda b, h, t: (rb(b, jnp.maximum(t, nseg)), h)),
        out_shape=jax.ShapeDtypeStruct((t_rows, f), BF16),
        scratch_shapes=[
            pltpu.VMEM((nseg * sr, wide), F32),
            pltpu.VMEM((hp, HGRN_DK, HGRN_DK), F32),
            pltpu.VMEM((sr, wide), F32),
        ],
        compiler_params=_params(("parallel", "parallel", "arbitrary")),
        name="gla",
    )(pq, pz, pvg, pvg, hgrn_lb, gnorm.reshape(1, HGRN_DK))


def _attn_kernel(q_ref, kl_ref, kc_ref, vl_ref, vc_ref, lam_ref, sub_ref, o_ref, *, lambda_init, nq):
    qi = pl.program_id(2)
    hd = DIFF_HEAD_DIM
    nhead = o_ref.shape[1] // (2 * hd)
    lp = lam_ref[...]
    lam = (jnp.exp(jnp.sum(lp[0:1, :] * lp[1:2, :], keepdims=True))
           - jnp.exp(jnp.sum(lp[2:3, :] * lp[3:4, :], keepdims=True)) + lambda_init)

    def probs(sc, sl):
        m = jnp.max(sc, axis=-1, keepdims=True)
        if sl is not None:
            m = jnp.maximum(m, jnp.max(sl, axis=-1, keepdims=True))
        pc = jnp.exp2(sc - m)
        den = jnp.sum(pc, axis=-1, keepdims=True)
        pl_ = None
        if sl is not None:
            pl_ = jnp.exp2(sl - m)
            den = den + jnp.sum(pl_, axis=-1, keepdims=True)
            pl_ = pl_.astype(BF16)
        return pc.astype(BF16), pl_, 1.0 / den

    def attend(with_latent_keys):
        cols = [slice(c * hd, (c + 1) * hd) for c in range(2 * nhead)]
        sc = [_dot_nt(q_ref[:, c], kc_ref[:, c]) for c in cols]
        sl = [_dot_nt(q_ref[:, c], kl_ref[:, c]) if with_latent_keys else None for c in cols]
        for h in range(nhead):
            vcols = slice(2 * h * hd, 2 * (h + 1) * hd)
            outs = []
            for mp in range(2):
                pc, pl_, inv = probs(sc[2 * h + mp], sl[2 * h + mp])
                o = jnp.dot(pc, vc_ref[:, vcols], preferred_element_type=F32)
                if pl_ is not None:
                    o += jnp.dot(pl_, vl_ref[:, vcols], preferred_element_type=F32)
                outs.append(o * (inv if mp == 0 else lam * inv))
            y = _rms(outs[0] - outs[1], sub_ref[...]) * (1.0 - lambda_init)
            o_ref[:, vcols] = y.astype(o_ref.dtype)

    @pl.when(qi < nq)
    def _():
        attend(True)

    @pl.when(qi == nq)
    def _():
        attend(False)


def _attn(qk, v, lam_p, subln, lambda_init, nb, seq, ctx_len):
    t_rows, d = v.shape
    wide = ATTN_HEADS_PER_STEP * 2 * DIFF_HEAD_DIM
    assert d % wide == 0
    hgroups = d // wide
    tq = ctx_len
    nq = seq // tq
    lat_qblocks = nb * nq
    qrow = lambda b, qi: jnp.where(qi == nq, lat_qblocks + b, b * nq + qi)
    lat = lambda part: pl.BlockSpec((seq, wide), lambda b, h, qi: (b, part * hgroups + h))
    ctx = lambda part: pl.BlockSpec((ctx_len, wide), lambda b, h, qi: (lat_qblocks + b, part * hgroups + h))
    return pl.pallas_call(
        functools.partial(_attn_kernel, lambda_init=lambda_init, nq=nq),
        grid=(nb, hgroups, nq + 1),
        in_specs=[
            pl.BlockSpec((tq, wide), lambda b, h, qi: (qrow(b, qi), h)),
            lat(1), ctx(1), lat(0), ctx(0),
            pl.BlockSpec((4, DIFF_HEAD_DIM), lambda b, h, qi: (0, 0)),
            pl.BlockSpec((1, 2 * DIFF_HEAD_DIM), lambda b, h, qi: (0, 0)),
        ],
        out_specs=pl.BlockSpec((tq, wide), lambda b, h, qi: (qrow(b, qi), h)),
        out_shape=jax.ShapeDtypeStruct((t_rows, d), BF16),
        compiler_params=_params(("parallel", "parallel", "arbitrary")),
        name="diff_attn",
    )(qk, qk, qk, v, v, lam_p, subln.reshape(1, 2 * DIFF_HEAD_DIM))


def _rope_tables(seq, nb, ctx_rows):
    rows = seq // GRID_W
    row = jnp.repeat(jnp.arange(rows, dtype=F32), GRID_W)
    col = jnp.tile(jnp.arange(GRID_W, dtype=F32), rows)
    n_freq = DIFF_HEAD_DIM // 4
    inv = ROPE_BASE ** (-jnp.arange(n_freq, dtype=F32) / n_freq)
    ang = jnp.concatenate([row[:, None] * inv, col[:, None] * inv], axis=-1)
    cos, sin = jnp.cos(ang), jnp.sin(ang)
    cos2 = jnp.concatenate([cos, cos], axis=-1)
    sin2 = jnp.concatenate([-sin, sin], axis=-1)
    cos2 = jnp.concatenate([jnp.tile(cos2, (nb, 1)), jnp.ones((ctx_rows, DIFF_HEAD_DIM), F32)], axis=0)
    sin2 = jnp.concatenate([jnp.tile(sin2, (nb, 1)), jnp.zeros((ctx_rows, DIFF_HEAD_DIM), F32)], axis=0)
    return cos2, sin2


def kernel(x, c, ctx, c_ctx, w_mod, b_mod, norm_g, w_mlp_in, w_mlp_out, fnet_w_out, hgrn_w_in,
           hgrn_lb, hgrn_gnorm, hgrn_w_out, diff_w_qkv, diff_lambda, diff_subln, diff_w_out):
    nb, seq, d = x.shape
    ctx_len = ctx.shape[1]
    depth = w_mod.shape[0]
    lat_rows, ctx_rows = nb * seq, nb * ctx_len
    assert nb + 1 <= MOD_ROWS

    xs = (x.reshape(lat_rows, d), ctx.reshape(ctx_rows, d))
    cc = jnp.concatenate([c, c_ctx[None, :], jnp.zeros((MOD_ROWS - nb - 1, d), F32)], axis=0)
    mods = _mods(cc, w_mod, b_mod)

    for i in range(depth):
        mixer, slot = i % N_MIXERS, i // N_MIXERS
        ctx_out = i < depth - 1
        ctx_in = ctx_out or mixer != 0
        assert ctx_out or mixer == 0
        nrows = lat_rows + ctx_rows if ctx_in else lat_rows
        common = dict(nrows=nrows, seq=seq, nb=nb)
        if mixer == 0:
            gd = d // FNET_GROUPS
            cg, sg = _dft_tables(gd)
            wcs = jnp.concatenate([cg, -sg], axis=1).astype(BF16)
            parts = _norm_mod(xs, norm_g, mods, i, 0, 0, wcs=wcs, **common)
            y = _posdft(parts, nb, seq, 0)
            if ctx_in:
                y = (y, _posdft(parts, nb, ctx_len, lat_rows // ctx_len))
            w_out = fnet_w_out[slot]
        elif mixer == 1:
            h = _norm_mod(xs, norm_g, mods, i, 0, 0, **common)
            w_in = hgrn_w_in[slot]
            f = hgrn_lb.shape[-1]
            pq = _mm(h, w_in, 0, f, BF16, name="hgrn_q")
            pz = _mm(h, w_in, f, 2 * f, F32, name="hgrn_z")
            pvg = _mm(h, w_in, 3 * f, 2 * d, BF16, name="hgrn_vg")
            y = _gla(pq, pz, pvg, hgrn_lb, hgrn_gnorm[slot], i, nb, seq, ctx_len)
            w_out = hgrn_w_out[slot]
        else:
            h = _norm_mod(xs, norm_g, mods, i, 0, 0, **common)
            cos2, sin2 = _rope_tables(seq, nb, ctx_rows)
            rope = (cos2, sin2, d, DIFF_HEAD_DIM ** -0.5 * math.log2(math.e))
            qk = _mm(h, diff_w_qkv[slot], 0, 2 * d, BF16, rope=rope, name="diff_qk")
            v = _mm(h, diff_w_qkv[slot], 2 * d, d, BF16, name="diff_v")
            lambda_init = 0.8 - 0.6 * math.exp(-0.3 * i)
            y = _attn(qk, v, diff_lambda[slot], diff_subln[slot], lambda_init, nb, seq, ctx_len)
            w_out = diff_w_out[slot]
        xs = _mm_res(y, w_out, xs, norm_g, mods, i, 1, 2, **common)
        xs = _mlp(xs, w_mlp_in, w_mlp_out, norm_g, mods, i, **common)
    return xs[:lat_rows].reshape(nb, seq, d)
```

```python
import functools
import math

import jax
import jax.numpy as jnp
from jax import lax
from jax.experimental import pallas as pl
from jax.experimental.pallas import tpu as pltpu

F32 = jnp.float32
BF16 = jnp.bfloat16

NORM_EPS = 1e-6
ROPE_BASE = 10000.0
GRID_W = 64
N_MOD = 6
N_MIXERS = 3
FNET_GROUPS = 8
HGRN_DK = 128
DIFF_HEAD_DIM = 128
MOD_ROWS = 8

GLA_SUB = 16
GLA_CHUNK = 64
GLA_HEADS_PER_STEP = 8
ATTN_HEADS_PER_STEP = 2
GLA_FAST_MAX_DECAY = 120.0
GLA_SEG = 256
MLP_OUT_CHUNK = 512
ROW_CHUNK = 32
MASKED_LOG_DECAY = -1e30

MODS_TN = 1024
ROW_TM = 512
MM_TM, MM_TN = 1024, 1024
MLP_TM, MLP_TH = 1024, 512
DFT_TM, DFT_TN = 256, 1024
V7X_VMEM_LIMIT = 56 * 1024 * 1024


def _params(sem, vmem=V7X_VMEM_LIMIT):
    return pltpu.CompilerParams(dimension_semantics=sem, vmem_limit_bytes=vmem)


def _silu(v):
    return v * jax.nn.sigmoid(v)


def _rms(y, gamma):
    return y * lax.rsqrt(jnp.mean(y * y, axis=-1, keepdims=True) + NORM_EPS) * gamma


def _mod_row(ref, tile, tiles_per_batch, n_batch):
    r = jnp.minimum(tile // tiles_per_batch, n_batch)
    return ref[pl.ds(r, 1), :]


def _mods_kernel(c_ref, w_ref, b_ref, o_ref):
    a = _silu(c_ref[...]).astype(BF16)
    o_ref[...] = jnp.dot(a, w_ref[...].astype(BF16), preferred_element_type=F32) + b_ref[...]


def _mods(cc, w_mod, b_mod):
    depth, d, n = w_mod.shape
    tn = MODS_TN
    return pl.pallas_call(
        _mods_kernel,
        grid=(depth, n // tn),
        in_specs=[
            pl.BlockSpec((MOD_ROWS, d), lambda l, j: (0, 0)),
            pl.BlockSpec((None, d, tn), lambda l, j: (l, 0, j)),
            pl.BlockSpec((None, 1, tn), lambda l, j: (l, 0, j)),
        ],
        out_specs=pl.BlockSpec((None, MOD_ROWS, tn), lambda l, j: (l, 0, j)),
        out_shape=jax.ShapeDtypeStruct((depth, MOD_ROWS, n), F32),
        compiler_params=_params(("parallel", "parallel")),
        name="mods",
    )(cc, w_mod, b_mod.reshape(depth, 1, n))


def _row_tiled(x, tm):
    if not isinstance(x, tuple):
        return [pl.BlockSpec((tm, x.shape[1]), lambda i: (i, 0))], [x], None
    lat, ctx = x
    nlat = lat.shape[0] // tm
    assert lat.shape[0] % tm == 0 and ctx.shape[0] % tm == 0
    specs = [pl.BlockSpec((tm, lat.shape[1]), lambda i: (jnp.minimum(i, nlat - 1), 0)),
             pl.BlockSpec((tm, ctx.shape[1]), lambda i: (jnp.maximum(i - nlat, 0), 0))]
    return specs, [lat, ctx], nlat


def _tile_rows(refs, nlat, rows):
    if len(refs) == 1:
        return refs[0][rows, :]
    return jnp.where(pl.program_id(0) < nlat, refs[0][rows, :], refs[1][rows, :])


def _norm_mod_kernel(*refs, n_x, nlat, tpb, nb, grow, groups):
    x_refs, (g_ref, sh_ref, sc_ref), rest = refs[:n_x], refs[n_x:n_x + 3], refs[n_x + 3:]
    i = pl.program_id(0)
    sh = _mod_row(sh_ref, i, tpb, nb)
    gmul = g_ref[grow:grow + 1, :] * (1.0 + _mod_row(sc_ref, i, tpb, nb))
    tm = x_refs[0].shape[0]
    norm = lambda rows: (_rms(_tile_rows(x_refs, nlat, rows), gmul) + sh).astype(BF16)
    if groups == 0:
        (o_ref,) = rest
        for r in range(0, tm, ROW_CHUNK):
            o_ref[r:r + ROW_CHUNK, :] = norm(slice(r, r + ROW_CHUNK))
        return
    wcs_ref, ae_ref, be_ref, ao_ref, bo_ref = rest
    hb = jnp.concatenate([norm(slice(r, r + ROW_CHUNK)) for r in range(0, tm, ROW_CHUNK)], axis=0)
    half = tm // 2
    dst = lax.broadcasted_iota(jnp.int32, (tm, tm), 0)
    src = lax.broadcasted_iota(jnp.int32, (tm, tm), 1)
    perm = jnp.where(src == 2 * (dst % half) + dst // half, 1.0, 0.0).astype(BF16)
    hb = jnp.dot(perm, hb, preferred_element_type=F32).astype(BF16)
    gd = hb.shape[1] // groups
    for g in range(groups):
        cols = slice(g * gd, (g + 1) * gd)
        r = jnp.dot(hb[:, cols], wcs_ref[...], preferred_element_type=F32)
        ae_ref[:, cols] = r[:half, :gd].astype(BF16)
        be_ref[:, cols] = r[:half, gd:].astype(BF16)
        ao_ref[:, cols] = r[half:, :gd].astype(BF16)
        bo_ref[:, cols] = r[half:, gd:].astype(BF16)


def _norm_mod(x, norm_g, mods, layer, grow, col, nrows, seq, nb, wcs=None):
    d = norm_g.shape[-1]
    tm = ROW_TM
    tpb = seq // tm
    groups = 0 if wcs is None else FNET_GROUPS
    x_specs, x_args, nlat = _row_tiled(x, tm)
    in_specs = x_specs + [
        pl.BlockSpec((None, 4, d), lambda i: (layer, 0, 0)),
        pl.BlockSpec((None, MOD_ROWS, d), lambda i: (layer, 0, col)),
        pl.BlockSpec((None, MOD_ROWS, d), lambda i: (layer, 0, col + 1)),
    ]
    args = x_args + [norm_g, mods, mods]
    out_spec = pl.BlockSpec((tm, d), lambda i: (i, 0))
    out_shape = jax.ShapeDtypeStruct((nrows, d), BF16)
    if wcs is not None:
        in_specs.append(pl.BlockSpec(wcs.shape, lambda i: (0, 0)))
        args.append(wcs)
        out_spec = [pl.BlockSpec((tm // 2, d), lambda i: (i, 0))] * 4
        out_shape = [jax.ShapeDtypeStruct((nrows // 2, d), BF16)] * 4
    return pl.pallas_call(
        functools.partial(_norm_mod_kernel, n_x=len(x_args), nlat=nlat, tpb=tpb, nb=nb, grow=grow,
                          groups=groups),
        grid=(nrows // tm,),
        in_specs=in_specs,
        out_specs=out_spec,
        out_shape=out_shape,
        compiler_params=_params(("parallel",)),
        name="norm_mod",
    )(*args)


def _mm_kernel(a_ref, w_ref, *rest, rope, q_tiles, q_scale):
    if rope:
        cos_ref, sin_ref, o_ref, wb_ref = rest
    else:
        o_ref, wb_ref = rest

    @pl.when(pl.program_id(1) == 0)
    def _():
        wb_ref[...] = w_ref[...].astype(BF16)

    acc = jnp.dot(a_ref[...], wb_ref[...], preferred_element_type=F32)
    if not rope:
        o_ref[...] = acc.astype(o_ref.dtype)
        return
    cos = cos_ref[...]
    sin = sin_ref[...]
    scale = jnp.where(pl.program_id(0) < q_tiles, q_scale, 1.0).astype(F32)
    hd = cos.shape[1]
    for c in range(acc.shape[1] // hd):
        blk = acc[:, c * hd:(c + 1) * hd]
        rot = pltpu.roll(blk, hd // 2, axis=1)
        o_ref[:, c * hd:(c + 1) * hd] = ((blk * cos + rot * sin) * scale).astype(o_ref.dtype)


def _mm(a, w, col0, ncols, out_dtype, rope=None, name="mm"):
    m, k = a.shape
    tm = min(MM_TM, m)
    tn = MM_TN
    assert m % tm == 0 and ncols % tn == 0 and col0 % tn == 0
    off = col0 // tn
    in_specs = [
        pl.BlockSpec((tm, k), lambda j, i: (i, 0)),
        pl.BlockSpec((k, tn), lambda j, i: (0, j + off)),
    ]
    args = [a, w]
    q_tiles = 0
    q_scale = 1.0
    if rope is not None:
        cos2, sin2, q_cols, q_scale = rope
        hd = cos2.shape[1]
        in_specs += [pl.BlockSpec((tm, hd), lambda j, i: (i, 0))] * 2
        args += [cos2, sin2]
        q_tiles = q_cols // tn
    return pl.pallas_call(
        functools.partial(_mm_kernel, rope=rope is not None, q_tiles=q_tiles, q_scale=q_scale),
        grid=(ncols // tn, m // tm),
        in_specs=in_specs,
        out_specs=pl.BlockSpec((tm, tn), lambda j, i: (i, j)),
        out_shape=jax.ShapeDtypeStruct((m, ncols), out_dtype),
        scratch_shapes=[pltpu.VMEM((k, tn), BF16)],
        compiler_params=_params(("parallel", "arbitrary")),
        name=name,
    )(*args)


def _mm_res_kernel(*refs, n_a, nlat_a, n_x, nlat_x, tpb, nb, grow):
    a_refs, w_ref, x_refs = refs[:n_a], refs[n_a], refs[n_a + 1:n_a + 1 + n_x]
    g_ref, gate_ref, o_ref = refs[n_a + 1 + n_x:]
    a = _tile_rows(a_refs, nlat_a, slice(None))
    y = jnp.dot(a, w_ref[...], preferred_element_type=F32)
    gout = g_ref[grow:grow + 1, :] * _mod_row(gate_ref, pl.program_id(0), tpb, nb)
    for r in range(0, y.shape[0], ROW_CHUNK):
        rows = slice(r, r + ROW_CHUNK)
        o_ref[rows, :] = _tile_rows(x_refs, nlat_x, rows) + _rms(y[rows], gout)


def _mm_res(a, w, x, norm_g, mods, layer, grow, gate_col, nrows, seq, nb):
    kdim, d = w.shape
    tm = ROW_TM
    tpb = seq // tm
    a_specs, a_args, nlat_a = _row_tiled(a, tm)
    x_specs, x_args, nlat_x = _row_tiled(x, tm)
    return pl.pallas_call(
        functools.partial(_mm_res_kernel, n_a=len(a_args), nlat_a=nlat_a, n_x=len(x_args),
                          nlat_x=nlat_x, tpb=tpb, nb=nb, grow=grow),
        grid=(nrows // tm,),
        in_specs=a_specs + [pl.BlockSpec((kdim, d), lambda i: (0, 0))] + x_specs + [
            pl.BlockSpec((None, 4, d), lambda i: (layer, 0, 0)),
            pl.BlockSpec((None, MOD_ROWS, d), lambda i: (layer, 0, gate_col)),
        ],
        out_specs=pl.BlockSpec((tm, d), lambda i: (i, 0)),
        out_shape=jax.ShapeDtypeStruct((nrows, d), F32),
        compiler_params=_params(("parallel",)),
        name="mm_res",
    )(*a_args, w.astype(BF16), *x_args, norm_g, mods)


def _mlp_kernel(x_ref, g_ref, sh_ref, sc_ref, gate_ref, w1_ref, w2_ref, o_ref, h_ref, *, tpb, nb):
    i = pl.program_id(0)
    k = pl.program_id(1)

    chunks = [slice(r, r + ROW_CHUNK) for r in range(0, x_ref.shape[0], ROW_CHUNK)]

    @pl.when(k == 0)
    def _():
        gmul = g_ref[2:3, :] * (1.0 + _mod_row(sc_ref, i, tpb, nb))
        sh = _mod_row(sh_ref, i, tpb, nb)
        for rows in chunks:
            h_ref[rows, :] = (_rms(x_ref[rows, :], gmul) + sh).astype(BF16)
        o_ref[...] = jnp.zeros_like(o_ref)

    u = jnp.dot(h_ref[...], w1_ref[...].astype(BF16), preferred_element_type=F32)
    u = jnp.square(jnp.maximum(u, 0.0)).astype(BF16)
    cn = MLP_OUT_CHUNK
    for c in range(o_ref.shape[1] // cn):
        cols = slice(c * cn, (c + 1) * cn)
        o_ref[:, cols] += jnp.dot(u, w2_ref[:, cols].astype(BF16), preferred_element_type=F32)

    @pl.when(k == pl.num_programs(1) - 1)
    def _():
        gout = g_ref[3:4, :] * _mod_row(gate_ref, i, tpb, nb)
        for rows in chunks:
            o_ref[rows, :] = x_ref[rows, :] + _rms(o_ref[rows, :], gout)


def _mlp(x, w1, w2, norm_g, mods, layer, nrows, seq, nb):
    d, hid = w1.shape[1], w1.shape[2]
    tm = MLP_TM
    th = MLP_TH
    tpb = seq // tm
    mod_spec = lambda col: pl.BlockSpec((None, MOD_ROWS, d), lambda i, k: (layer, 0, col))
    return pl.pallas_call(
        functools.partial(_mlp_kernel, tpb=tpb, nb=nb),
        grid=(nrows // tm, hid // th),
        in_specs=[
            pl.BlockSpec((tm, d), lambda i, k: (i, 0)),
            pl.BlockSpec((None, 4, d), lambda i, k: (layer, 0, 0)),
            mod_spec(3), mod_spec(4), mod_spec(5),
            pl.BlockSpec((None, d, th), lambda i, k: (layer, 0, k)),
            pl.BlockSpec((None, th, d), lambda i, k: (layer, k, 0)),
        ],
        out_specs=pl.BlockSpec((tm, d), lambda i, k: (i, 0)),
        out_shape=jax.ShapeDtypeStruct((nrows, d), F32),
        scratch_shapes=[pltpu.VMEM((tm, d), BF16)],
        compiler_params=_params(("parallel", "arbitrary")),
        name="mlp",
    )(x, norm_g, mods, mods, mods, w1, w2)


def _dft_tables(n, rows=None, col0=0, col_step=1, w=32):
    rows = n if rows is None else rows
    ncol = n // col_step
    assert ncol % w == 0
    k = jnp.arange(rows, dtype=jnp.int32)[:, None]
    unit = 2.0 * math.pi / n
    la = col_step * w * jnp.arange(ncol // w, dtype=jnp.int32)[None, :]
    lb = col0 + col_step * jnp.arange(w, dtype=jnp.int32)[None, :]
    ang_a = ((k * la) % n).astype(F32) * unit
    ang_b = ((k * lb) % n).astype(F32) * unit
    ca, sa = jnp.cos(ang_a)[:, :, None], jnp.sin(ang_a)[:, :, None]
    cb, sb = jnp.cos(ang_b)[:, None, :], jnp.sin(ang_b)[:, None, :]
    return (ca * cb - sa * sb).reshape(rows, ncol), (sa * cb + ca * sb).reshape(rows, ncol)


def _posdft_kernel(ce_ref, se_ref, co_ref, so_ref, ae_ref, be_ref, ao_ref, bo_ref, o_ref):
    even = jnp.dot(ce_ref[...], ae_ref[...], preferred_element_type=F32)
    even += jnp.dot(se_ref[...], be_ref[...], preferred_element_type=F32)
    odd = jnp.dot(co_ref[...], ao_ref[...], preferred_element_type=F32)
    odd += jnp.dot(so_ref[...], bo_ref[...], preferred_element_type=F32)
    o_ref[0] = (even + odd).astype(o_ref.dtype)
    o_ref[1] = (even - odd).astype(o_ref.dtype)


def _posdft(parts, nbatch, seq, seq0):
    d = parts[0].shape[1]
    half = seq // 2
    tm = min(DFT_TM, half)
    tn = DFT_TN
    tables = [t.astype(BF16) for par in (0, 1) for t in _dft_tables(seq, half, par, 2)]
    nj = d // tn
    tab_spec = pl.BlockSpec((tm, half), lambda b, j, m: (m, 0))
    src = pl.BlockSpec((half, tn), lambda b, j, m: (seq0 + b, j))
    y = pl.pallas_call(
        _posdft_kernel,
        grid=(nbatch, nj, half // tm),
        in_specs=[tab_spec] * 4 + [src] * 4,
        out_specs=pl.BlockSpec((None, 2, tm, tn), lambda b, j, m: (b, 0, m, j)),
        out_shape=jax.ShapeDtypeStruct((nbatch, 2, half, d), BF16),
        compiler_params=_params(("parallel", "parallel", "arbitrary")),
        name="posdft",
    )(*tables, *parts)
    return y.reshape(nbatch * seq, d)


def _split3_dot(mask_bf16, v):
    hi = v.astype(BF16)
    r1 = v - hi.astype(F32)
    mid = r1.astype(BF16)
    lo = (r1 - mid.astype(F32)).astype(BF16)
    out = jnp.dot(mask_bf16, hi, preferred_element_type=F32)
    out += jnp.dot(mask_bf16, mid, preferred_element_type=F32)
    out += jnp.dot(mask_bf16, lo, preferred_element_type=F32)
    return out


def _dot_nt(a, b):
    return lax.dot_general(a, b, (((1,), (1,)), ((), ())), preferred_element_type=F32)


def _dot_tn(a, b):
    return lax.dot_general(a, b, (((0,), (0,)), ((), ())), preferred_element_type=F32)


_GLA_HALF = 128


def _block_sum_masks(block, fwd):
    ri = lax.broadcasted_iota(jnp.int32, (_GLA_HALF, _GLA_HALF), 0)
    ci = lax.broadcasted_iota(jnp.int32, (_GLA_HALF, _GLA_HALF), 1)
    same = (ri // block) == (ci // block)
    ahead = (ri - ci) if fwd else (ci - ri)
    m_incl = jnp.where(same & (ahead >= 0), 1.0, 0.0).astype(BF16)
    m_excl = jnp.where(same & (ahead < 0), 1.0, 0.0).astype(BF16)
    return m_incl, m_excl


def _block_sums(mask, x):
    return jnp.concatenate([_split3_dot(mask, x[r0:r0 + _GLA_HALF])
                            for r0 in range(0, x.shape[0], _GLA_HALF)], axis=0)


def _gla_fast(fwd, q, kk, bc, v_ref, st_ref, o_ref):
    cs = GLA_CHUNK
    nchunk = q.shape[0] // cs
    nhead = st_ref.shape[0]
    ri = lax.broadcasted_iota(jnp.int32, (cs, cs), 0)
    ci = lax.broadcasted_iota(jnp.int32, (cs, cs), 1)
    keep = (ci <= ri) if fwd else (ci >= ri)
    heads = [slice(h * HGRN_DK, (h + 1) * HGRN_DK) for h in range(nhead)]
    order = [j if fwd else nchunk - 1 - j for j in range(nchunk)]
    chunks = []
    for c in order:
        rows = slice(c * cs, (c + 1) * cs)
        b = bc[rows]
        tot = b[cs - 1:cs] if fwd else b[0:1]
        mid = 0.5 * tot
        qm = (q[rows] * jnp.exp(b - mid)).astype(BF16)
        km = (kk[rows] * jnp.exp(mid - b)).astype(BF16)
        chunks.append((rows, qm, km, jnp.exp(mid), jnp.exp(tot), jnp.exp(tot - mid)))
    scores = [[_dot_nt(qm[:, hc], km[:, hc]) for hc in heads] for _, qm, km, *_ in chunks]
    updates = [[_dot_tn(v_ref[rows, hc], km[:, hc]) for hc in heads] for rows, _, km, *_ in chunks]
    sts = [st_ref[h] for h in range(nhead)]
    for (rows, qm, km, carry_in, carry_out, new_out), sc, upd in zip(chunks, scores, updates):
        for h, hc in enumerate(heads):
            a = jnp.where(keep, sc[h], 0.0).astype(BF16)
            o = jnp.dot(a, v_ref[rows, hc], preferred_element_type=F32)
            o += _dot_nt(qm[:, hc], (sts[h] * carry_in[:, hc]).astype(BF16))
            o_ref[rows, hc] = o
            sts[h] = sts[h] * carry_out[:, hc] + upd[h] * new_out[:, hc]
    for h in range(nhead):
        st_ref[h] = sts[h]


def _gla_exact(fwd, q, kk, logf, v_ref, st_ref, o_ref):
    seg_rows = q.shape[0]
    nsub = seg_rows // GLA_SUB
    hs = GLA_SUB // 2
    v = v_ref[...].astype(F32)

    m_incl, m_excl = _block_sum_masks(GLA_SUB, fwd)
    bc_all = _block_sums(m_incl, logf)
    qd_all = (q * jnp.exp(bc_all)).astype(BF16)
    kd_all = (kk * jnp.exp(_block_sums(m_excl, logf))).astype(BF16)

    rowh = lax.broadcasted_iota(jnp.int32, (hs, HGRN_DK), 0)
    for h in range(st_ref.shape[0]):
        hc = slice(h * HGRN_DK, (h + 1) * HGRN_DK)
        st = st_ref[h]
        for j in range(nsub):
            i = j if fwd else nsub - 1 - j
            r0 = i * GLA_SUB
            o = _dot_nt(qd_all[r0:r0 + GLA_SUB, hc], st.astype(BF16))
            acc = [o[:hs], o[hs:]]
            for s in range(GLA_SUB):
                sr = r0 + s
                b_s, k_s, v_s = bc_all[sr:sr + 1, hc], kk[sr:sr + 1, hc], v[sr:sr + 1, hc]
                for hh in range(2):
                    lo_row = hh * hs
                    reach_all = (lo_row > s) if fwd else (lo_row + hs - 1 < s)
                    reach_none = (lo_row + hs - 1 < s) if fwd else (lo_row > s)
                    if reach_none:
                        continue
                    rr = slice(r0 + lo_row, r0 + lo_row + hs)
                    diff = bc_all[rr, hc] - b_s
                    if not reach_all:
                        keep = (rowh + lo_row >= s) if fwd else (rowh + lo_row <= s)
                        diff = jnp.where(keep, diff, MASKED_LOG_DECAY)
                    col = jnp.sum(q[rr, hc] * jnp.exp(diff) * k_s, axis=1, keepdims=True)
                    acc[hh] = acc[hh] + col * v_s
            o_ref[r0:r0 + hs, hc] = acc[0]
            o_ref[r0 + hs:r0 + GLA_SUB, hc] = acc[1]
            last = r0 + GLA_SUB - 1 if fwd else r0
            upd = _dot_tn(v_ref[r0:r0 + GLA_SUB, hc], kd_all[r0:r0 + GLA_SUB, hc])
            st = st * jnp.exp(bc_all[last:last + 1, hc]) + upd
        st_ref[h] = st


def _gla_segment(fwd, q_ref, z_ref, v_ref, lb, st_ref, o_ref):
    half_gap = 0.5 * (1.0 - lb)
    f = (lb + half_gap) + half_gap * jnp.tanh(0.5 * z_ref[...])
    logf = jnp.log(f)
    kk = 1.0 - f
    q = q_ref[...].astype(F32)

    bc = _block_sums(_block_sum_masks(GLA_CHUNK, fwd)[0], logf)
    edge = GLA_CHUNK - 1 if fwd else 0
    tot = bc[edge:edge + 1]
    for c in range(1, q.shape[0] // GLA_CHUNK):
        tot = jnp.minimum(tot, bc[c * GLA_CHUNK + edge:c * GLA_CHUNK + edge + 1])
    mild = jnp.max(-tot) <= GLA_FAST_MAX_DECAY

    @pl.when(mild)
    def _():
        _gla_fast(fwd, q, kk, bc, v_ref, st_ref, o_ref)

    @pl.when(jnp.logical_not(mild))
    def _():
        _gla_exact(fwd, q, kk, logf, v_ref, st_ref, o_ref)


def _gla_kernel(q_ref, z_ref, v_ref, g_ref, lbp_ref, gn_ref, y_ref, of_ref, st_ref, o_ref,
                *, layer, nseg):
    t = pl.program_id(2)
    fwd = t < nseg
    seg = t % nseg
    seg_rows = q_ref.shape[0]

    @pl.when(seg == 0)
    def _():
        st_ref[...] = jnp.zeros_like(st_ref)

    p = lbp_ref[...]
    e = jnp.exp(p - jnp.max(p, axis=0, keepdims=True))
    sm = e / jnp.sum(e, axis=0, keepdims=True)
    lb2 = jnp.zeros_like(sm[0])
    for j in range(1, layer + 1):
        lb2 = lb2 + sm[j]

    @pl.when(fwd)
    def _():
        _gla_segment(True, q_ref, z_ref, v_ref, lb2[0:1, :], st_ref, o_ref)
        of_ref[pl.ds(pl.multiple_of(seg * seg_rows, seg_rows), seg_rows), :] = o_ref[...]

    @pl.when(jnp.logical_not(fwd))
    def _():
        _gla_segment(False, q_ref, z_ref, v_ref, lb2[1:2, :], st_ref, o_ref)
        fseg = jnp.where(seg == 0, 0, nseg - seg)
        o = o_ref[...] + of_ref[pl.ds(pl.multiple_of(fseg * seg_rows, seg_rows), seg_rows), :]
        gate = _silu(g_ref[...].astype(F32))
        for h in range(st_ref.shape[0]):
            hc = slice(h * HGRN_DK, (h + 1) * HGRN_DK)
            y_ref[:, hc] = (_rms(o[:, hc], gn_ref[...]) * gate[:, hc]).astype(y_ref.dtype)


def _gla(pq, pz, pvg, hgrn_lb, gnorm, layer, nb, seq, ctx_len):
    t_rows, f = pq.shape
    heads = f // HGRN_DK
    sr = GLA_SEG
    assert ctx_len == sr and seq % sr == 0
    nlat = seq // sr
    nseg = nlat + 1
    lat_blocks = nb * nlat
    depth = hgrn_lb.shape[0]

    def rb(b, t):
        seg = t % nseg
        lat = b * nlat + jnp.where(t < nseg, seg - 1, nlat - seg)
        return jnp.where(seg == 0, lat_blocks + b, lat)

    hp = GLA_HEADS_PER_STEP
    assert heads % hp == 0
    hgroups = heads // hp
    wide = hp * HGRN_DK
    blk = lambda colf: pl.BlockSpec((sr, wide), lambda b, h, t: (rb(b, t), colf(h, t)))
    return pl.pallas_call(
        functools.partial(_gla_kernel, layer=layer, nseg=nseg),
        grid=(nb, hgroups, 2 * nseg),
        in_specs=[
            blk(lambda h, t: h),
            blk(lambda h, t: jnp.where(t < nseg, 0, hgroups) + h),
            blk(lambda h, t: h),
            blk(lambda h, t: hgroups + h),
            pl.BlockSpec((depth, 2, wide), lambda b, h, t: (0, 0, h)),
            pl.BlockSpec((1, HGRN_DK), lambda b, h, t: (0, 0)),
        ],
        out_specs=pl.BlockSpec((sr, wide), lambda b, h, t: (rb(b, jnp.maximum(t, nseg)), h)),
        out_shape=jax.ShapeDtypeStruct((t_rows, f), BF16),
        scratch_shapes=[
            pltpu.VMEM((nseg * sr, wide), F32),
            pltpu.VMEM((hp, HGRN_DK, HGRN_DK), F32),
            pltpu.VMEM((sr, wide), F32),
        ],
        compiler_params=_params(("parallel", "parallel", "arbitrary")),
        name="gla",
    )(pq, pz, pvg, pvg, hgrn_lb, gnorm.reshape(1, HGRN_DK))


def _attn_kernel(q_ref, kl_ref, kc_ref, vl_ref, vc_ref, lam_ref, sub_ref, o_ref, *, lambda_init, nq):
    qi = pl.program_id(2)
    hd = DIFF_HEAD_DIM
    nhead = o_ref.shape[1] // (2 * hd)
    lp = lam_ref[...]
    lam = (jnp.exp(jnp.sum(lp[0:1, :] * lp[1:2, :], keepdims=True))
           - jnp.exp(jnp.sum(lp[2:3, :] * lp[3:4, :], keepdims=True)) + lambda_init)

    def probs(sc, sl):
        m = jnp.max(sc, axis=-1, keepdims=True)
        if sl is not None:
            m = jnp.maximum(m, jnp.max(sl, axis=-1, keepdims=True))
        pc = jnp.exp2(sc - m)
        den = jnp.sum(pc, axis=-1, keepdims=True)
        pl_ = None
        if sl is not None:
            pl_ = jnp.exp2(sl - m)
            den = den + jnp.sum(pl_, axis=-1, keepdims=True)
            pl_ = pl_.astype(BF16)
        return pc.astype(BF16), pl_, 1.0 / den

    def attend(with_latent_keys):
        cols = [slice(c * hd, (c + 1) * hd) for c in range(2 * nhead)]
        sc = [_dot_nt(q_ref[:, c], kc_ref[:, c]) for c in cols]
        sl = [_dot_nt(q_ref[:, c], kl_ref[:, c]) if with_latent_keys else None for c in cols]
        for h in range(nhead):
            vcols = slice(2 * h * hd, 2 * (h + 1) * hd)
            outs = []
            for mp in range(2):
                pc, pl_, inv = probs(sc[2 * h + mp], sl[2 * h + mp])
                o = jnp.dot(pc, vc_ref[:, vcols], preferred_element_type=F32)
                if pl_ is not None:
                    o += jnp.dot(pl_, vl_ref[:, vcols], preferred_element_type=F32)
                outs.append(o * (inv if mp == 0 else lam * inv))
            y = _rms(outs[0] - outs[1], sub_ref[...]) * (1.0 - lambda_init)
            o_ref[:, vcols] = y.astype(o_ref.dtype)

    @pl.when(qi < nq)
    def _():
        attend(True)

    @pl.when(qi == nq)
    def _():
        attend(False)


def _attn(qk, v, lam_p, subln, lambda_init, nb, seq, ctx_len):
    t_rows, d = v.shape
    wide = ATTN_HEADS_PER_STEP * 2 * DIFF_HEAD_DIM
    assert d % wide == 0
    hgroups = d // wide
    tq = ctx_len
    nq = seq // tq
    lat_qblocks = nb * nq
    qrow = lambda b, qi: jnp.where(qi == nq, lat_qblocks + b, b * nq + qi)
    lat = lambda part: pl.BlockSpec((seq, wide), lambda b, h, qi: (b, part * hgroups + h))
    ctx = lambda part: pl.BlockSpec((ctx_len, wide), lambda b, h, qi: (lat_qblocks + b, part * hgroups + h))
    return pl.pallas_call(
        functools.partial(_attn_kernel, lambda_init=lambda_init, nq=nq),
        grid=(nb, hgroups, nq + 1),
        in_specs=[
            pl.BlockSpec((tq, wide), lambda b, h, qi: (qrow(b, qi), h)),
            lat(1), ctx(1), lat(0), ctx(0),
            pl.BlockSpec((4, DIFF_HEAD_DIM), lambda b, h, qi: (0, 0)),
            pl.BlockSpec((1, 2 * DIFF_HEAD_DIM), lambda b, h, qi: (0, 0)),
        ],
        out_specs=pl.BlockSpec((tq, wide), lambda b, h, qi: (qrow(b, qi), h)),
        out_shape=jax.ShapeDtypeStruct((t_rows, d), BF16),
        compiler_params=_params(("parallel", "parallel", "arbitrary")),
        name="diff_attn",
    )(qk, qk, qk, v, v, lam_p, subln.reshape(1, 2 * DIFF_HEAD_DIM))


def _rope_tables(seq, nb, ctx_rows):
    rows = seq // GRID_W
    row = jnp.repeat(jnp.arange(rows, dtype=F32), GRID_W)
    col = jnp.tile(jnp.arange(GRID_W, dtype=F32), rows)
    n_freq = DIFF_HEAD_DIM // 4
    inv = ROPE_BASE ** (-jnp.arange(n_freq, dtype=F32) / n_freq)
    ang = jnp.concatenate([row[:, None] * inv, col[:, None] * inv], axis=-1)
    cos, sin = jnp.cos(ang), jnp.sin(ang)
    cos2 = jnp.concatenate([cos, cos], axis=-1)
    sin2 = jnp.concatenate([-sin, sin], axis=-1)
    cos2 = jnp.concatenate([jnp.tile(cos2, (nb, 1)), jnp.ones((ctx_rows, DIFF_HEAD_DIM), F32)], axis=0)
    sin2 = jnp.concatenate([jnp.tile(sin2, (nb, 1)), jnp.zeros((ctx_rows, DIFF_HEAD_DIM), F32)], axis=0)
    return cos2, sin2


def kernel(x, c, ctx, c_ctx, w_mod, b_mod, norm_g, w_mlp_in, w_mlp_out, fnet_w_out, hgrn_w_in,
           hgrn_lb, hgrn_gnorm, hgrn_w_out, diff_w_qkv, diff_lambda, diff_subln, diff_w_out):
    nb, seq, d = x.shape
    ctx_len = ctx.shape[1]
    depth = w_mod.shape[0]
    lat_rows, ctx_rows = nb * seq, nb * ctx_len
    assert nb + 1 <= MOD_ROWS

    xs = (x.reshape(lat_rows, d), ctx.reshape(ctx_rows, d))
    cc = jnp.concatenate([c, c_ctx[None, :], jnp.zeros((MOD_ROWS - nb - 1, d), F32)], axis=0)
    mods = _mods(cc, w_mod, b_mod)

    for i in range(depth):
        mixer, slot = i % N_MIXERS, i // N_MIXERS
        ctx_out = i < depth - 1
        ctx_in = ctx_out or mixer != 0
        assert ctx_out or mixer == 0
        nrows = lat_rows + ctx_rows if ctx_in else lat_rows
        common = dict(nrows=nrows, seq=seq, nb=nb)
        if mixer == 0:
            gd = d // FNET_GROUPS
            cg, sg = _dft_tables(gd)
            wcs = jnp.concatenate([cg, -sg], axis=1).astype(BF16)
            parts = _norm_mod(xs, norm_g, mods, i, 0, 0, wcs=wcs, **common)
            y = _posdft(parts, nb, seq, 0)
            if ctx_in:
                y = (y, _posdft(parts, nb, ctx_len, lat_rows // ctx_len))
            w_out = fnet_w_out[slot]
        elif mixer == 1:
            h = _norm_mod(xs, norm_g, mods, i, 0, 0, **common)
            w_in = hgrn_w_in[slot]
            f = hgrn_lb.shape[-1]
            pq = _mm(h, w_in, 0, f, BF16, name="hgrn_q")
            pz = _mm(h, w_in, f, 2 * f, F32, name="hgrn_z")
            pvg = _mm(h, w_in, 3 * f, 2 * d, BF16, name="hgrn_vg")
            y = _gla(pq, pz, pvg, hgrn_lb, hgrn_gnorm[slot], i, nb, seq, ctx_len)
            w_out = hgrn_w_out[slot]
        else:
            h = _norm_mod(xs, norm_g, mods, i, 0, 0, **common)
            cos2, sin2 = _rope_tables(seq, nb, ctx_rows)
            rope = (cos2, sin2, d, DIFF_HEAD_DIM ** -0.5 * math.log2(math.e))
            qk = _mm(h, diff_w_qkv[slot], 0, 2 * d, BF16, rope=rope, name="diff_qk")
            v = _mm(h, diff_w_qkv[slot], 2 * d, d, BF16, name="diff_v")
            lambda_init = 0.8 - 0.6 * math.exp(-0.3 * i)
            y = _attn(qk, v, diff_lambda[slot], diff_subln[slot], lambda_init, nb, seq, ctx_len)
            w_out = diff_w_out[slot]
        xs = _mm_res(y, w_out, xs, norm_g, mods, i, 1, 2, **common)
        xs = _mlp(xs, w_mlp_in, w_mlp_out, norm_g, mods, i, **common)
    return xs[:lat_rows].reshape(nb, seq, d)
```

```python
import functools
import math

import jax
import jax.numpy as jnp
from jax import lax
from jax.experimental import pallas as pl
from jax.experimental.pallas import tpu as pltpu

F32 = jnp.float32
BF16 = jnp.bfloat16

NORM_EPS = 1e-6
ROPE_BASE = 10000.0
GRID_W = 64
N_MOD = 6
N_MIXERS = 3
FNET_GROUPS = 8
HGRN_DK = 128
DIFF_HEAD_DIM = 128
MOD_ROWS = 8

GLA_SUB = 16
GLA_CHUNK = 64
GLA_HEADS_PER_STEP = 8
ATTN_HEADS_PER_STEP = 4
ATTN_TQ = 128
GLA_FAST_MAX_DECAY = 120.0
GLA_SEG = 256
MLP_OUT_CHUNK = 512
ROW_CHUNK = 32
MASKED_LOG_DECAY = -1e30

MODS_TN = 1024
ROW_TM = 512
MM_TM, MM_TN = 1024, 1024
MLP_TM, MLP_TH = 1024, 512
DFT_TM, DFT_TN = 256, 1024
V7X_VMEM_LIMIT = 56 * 1024 * 1024


def _params(sem, vmem=V7X_VMEM_LIMIT):
    return pltpu.CompilerParams(dimension_semantics=sem, vmem_limit_bytes=vmem)


def _silu(v):
    return v * jax.nn.sigmoid(v)


def _rms(y, gamma):
    return y * lax.rsqrt(jnp.mean(y * y, axis=-1, keepdims=True) + NORM_EPS) * gamma


def _mod_row(ref, tile, tiles_per_batch, n_batch):
    r = jnp.minimum(tile // tiles_per_batch, n_batch)
    return ref[pl.ds(r, 1), :]


def _mods_kernel(c_ref, w_ref, b_ref, o_ref):
    a = _silu(c_ref[...]).astype(BF16)
    o_ref[...] = jnp.dot(a, w_ref[...].astype(BF16), preferred_element_type=F32) + b_ref[...]


def _mods(cc, w_mod, b_mod):
    depth, d, n = w_mod.shape
    tn = MODS_TN
    return pl.pallas_call(
        _mods_kernel,
        grid=(depth, n // tn),
        in_specs=[
            pl.BlockSpec((MOD_ROWS, d), lambda l, j: (0, 0)),
            pl.BlockSpec((None, d, tn), lambda l, j: (l, 0, j)),
            pl.BlockSpec((None, 1, tn), lambda l, j: (l, 0, j)),
        ],
        out_specs=pl.BlockSpec((None, MOD_ROWS, tn), lambda l, j: (l, 0, j)),
        out_shape=jax.ShapeDtypeStruct((depth, MOD_ROWS, n), F32),
        compiler_params=_params(("parallel", "parallel")),
        name="mods",
    )(cc, w_mod, b_mod.reshape(depth, 1, n))


def _row_tiled(x, tm):
    if not isinstance(x, tuple):
        return [pl.BlockSpec((tm, x.shape[1]), lambda i: (i, 0))], [x], None
    lat, ctx = x
    nlat = lat.shape[0] // tm
    assert lat.shape[0] % tm == 0 and ctx.shape[0] % tm == 0
    specs = [pl.BlockSpec((tm, lat.shape[1]), lambda i: (jnp.minimum(i, nlat - 1), 0)),
             pl.BlockSpec((tm, ctx.shape[1]), lambda i: (jnp.maximum(i - nlat, 0), 0))]
    return specs, [lat, ctx], nlat


def _tile_rows(refs, nlat, rows):
    if len(refs) == 1:
        return refs[0][rows, :]
    return jnp.where(pl.program_id(0) < nlat, refs[0][rows, :], refs[1][rows, :])


def _norm_mod_kernel(*refs, n_x, nlat, tpb, nb, grow, groups):
    x_refs, (g_ref, sh_ref, sc_ref), rest = refs[:n_x], refs[n_x:n_x + 3], refs[n_x + 3:]
    i = pl.program_id(0)
    sh = _mod_row(sh_ref, i, tpb, nb)
    gmul = g_ref[grow:grow + 1, :] * (1.0 + _mod_row(sc_ref, i, tpb, nb))
    tm = x_refs[0].shape[0]
    norm = lambda rows: (_rms(_tile_rows(x_refs, nlat, rows), gmul) + sh).astype(BF16)
    if groups == 0:
        (o_ref,) = rest
        for r in range(0, tm, ROW_CHUNK):
            o_ref[r:r + ROW_CHUNK, :] = norm(slice(r, r + ROW_CHUNK))
        return
    wcs_ref, ae_ref, be_ref, ao_ref, bo_ref = rest
    hb = jnp.concatenate([norm(slice(r, r + ROW_CHUNK)) for r in range(0, tm, ROW_CHUNK)], axis=0)
    half = tm // 2
    dst = lax.broadcasted_iota(jnp.int32, (tm, tm), 0)
    src = lax.broadcasted_iota(jnp.int32, (tm, tm), 1)
    perm = jnp.where(src == 2 * (dst % half) + dst // half, 1.0, 0.0).astype(BF16)
    hb = jnp.dot(perm, hb, preferred_element_type=F32).astype(BF16)
    gd = hb.shape[1] // groups
    for g in range(groups):
        cols = slice(g * gd, (g + 1) * gd)
        r = jnp.dot(hb[:, cols], wcs_ref[...], preferred_element_type=F32)
        ae_ref[:, cols] = r[:half, :gd].astype(BF16)
        be_ref[:, cols] = r[:half, gd:].astype(BF16)
        ao_ref[:, cols] = r[half:, :gd].astype(BF16)
        bo_ref[:, cols] = r[half:, gd:].astype(BF16)


def _norm_mod(x, norm_g, mods, layer, grow, col, nrows, seq, nb, wcs=None):
    d = norm_g.shape[-1]
    tm = ROW_TM
    tpb = seq // tm
    groups = 0 if wcs is None else FNET_GROUPS
    x_specs, x_args, nlat = _row_tiled(x, tm)
    in_specs = x_specs + [
        pl.BlockSpec((None, 4, d), lambda i: (layer, 0, 0)),
        pl.BlockSpec((None, MOD_ROWS, d), lambda i: (layer, 0, col)),
        pl.BlockSpec((None, MOD_ROWS, d), lambda i: (layer, 0, col + 1)),
    ]
    args = x_args + [norm_g, mods, mods]
    out_spec = pl.BlockSpec((tm, d), lambda i: (i, 0))
    out_shape = jax.ShapeDtypeStruct((nrows, d), BF16)
    if wcs is not None:
        in_specs.append(pl.BlockSpec(wcs.shape, lambda i: (0, 0)))
        args.append(wcs)
        out_spec = [pl.BlockSpec((tm // 2, d), lambda i: (i, 0))] * 4
        out_shape = [jax.ShapeDtypeStruct((nrows // 2, d), BF16)] * 4
    return pl.pallas_call(
        functools.partial(_norm_mod_kernel, n_x=len(x_args), nlat=nlat, tpb=tpb, nb=nb, grow=grow,
                          groups=groups),
        grid=(nrows // tm,),
        in_specs=in_specs,
        out_specs=out_spec,
        out_shape=out_shape,
        compiler_params=_params(("parallel",)),
        name="norm_mod",
    )(*args)


def _mm_kernel(a_ref, w_ref, *rest, rope, q_tiles, q_scale):
    if rope:
        cos_ref, sin_ref, o_ref, wb_ref = rest
    else:
        o_ref, wb_ref = rest

    @pl.when(pl.program_id(1) == 0)
    def _():
        wb_ref[...] = w_ref[...].astype(BF16)

    acc = jnp.dot(a_ref[...], wb_ref[...], preferred_element_type=F32)
    if not rope:
        o_ref[...] = acc.astype(o_ref.dtype)
        return
    cos = cos_ref[...]
    sin = sin_ref[...]
    scale = jnp.where(pl.program_id(0) < q_tiles, q_scale, 1.0).astype(F32)
    hd = cos.shape[1]
    for c in range(acc.shape[1] // hd):
        blk = acc[:, c * hd:(c + 1) * hd]
        rot = pltpu.roll(blk, hd // 2, axis=1)
        o_ref[:, c * hd:(c + 1) * hd] = ((blk * cos + rot * sin) * scale).astype(o_ref.dtype)


def _mm(a, w, col0, ncols, out_dtype, rope=None, name="mm"):
    m, k = a.shape
    tm = min(MM_TM, m)
    tn = MM_TN
    assert m % tm == 0 and ncols % tn == 0 and col0 % tn == 0
    off = col0 // tn
    in_specs = [
        pl.BlockSpec((tm, k), lambda j, i: (i, 0)),
        pl.BlockSpec((k, tn), lambda j, i: (0, j + off)),
    ]
    args = [a, w]
    q_tiles = 0
    q_scale = 1.0
    if rope is not None:
        cos2, sin2, q_cols, q_scale = rope
        hd = cos2.shape[1]
        in_specs += [pl.BlockSpec((tm, hd), lambda j, i: (i, 0))] * 2
        args += [cos2, sin2]
        q_tiles = q_cols // tn
    return pl.pallas_call(
        functools.partial(_mm_kernel, rope=rope is not None, q_tiles=q_tiles, q_scale=q_scale),
        grid=(ncols // tn, m // tm),
        in_specs=in_specs,
        out_specs=pl.BlockSpec((tm, tn), lambda j, i: (i, j)),
        out_shape=jax.ShapeDtypeStruct((m, ncols), out_dtype),
        scratch_shapes=[pltpu.VMEM((k, tn), BF16)],
        compiler_params=_params(("parallel", "arbitrary")),
        name=name,
    )(*args)


def _mm_res_kernel(*refs, n_a, nlat_a, n_x, nlat_x, tpb, nb, grow):
    a_refs, w_ref, x_refs = refs[:n_a], refs[n_a], refs[n_a + 1:n_a + 1 + n_x]
    g_ref, gate_ref, o_ref = refs[n_a + 1 + n_x:]
    a = _tile_rows(a_refs, nlat_a, slice(None))
    y = jnp.dot(a, w_ref[...], preferred_element_type=F32)
    gout = g_ref[grow:grow + 1, :] * _mod_row(gate_ref, pl.program_id(0), tpb, nb)
    for r in range(0, y.shape[0], ROW_CHUNK):
        rows = slice(r, r + ROW_CHUNK)
        o_ref[rows, :] = _tile_rows(x_refs, nlat_x, rows) + _rms(y[rows], gout)


def _mm_res(a, w, x, norm_g, mods, layer, grow, gate_col, nrows, seq, nb):
    kdim, d = w.shape
    tm = ROW_TM
    tpb = seq // tm
    a_specs, a_args, nlat_a = _row_tiled(a, tm)
    x_specs, x_args, nlat_x = _row_tiled(x, tm)
    return pl.pallas_call(
        functools.partial(_mm_res_kernel, n_a=len(a_args), nlat_a=nlat_a, n_x=len(x_args),
                          nlat_x=nlat_x, tpb=tpb, nb=nb, grow=grow),
        grid=(nrows // tm,),
        in_specs=a_specs + [pl.BlockSpec((kdim, d), lambda i: (0, 0))] + x_specs + [
            pl.BlockSpec((None, 4, d), lambda i: (layer, 0, 0)),
            pl.BlockSpec((None, MOD_ROWS, d), lambda i: (layer, 0, gate_col)),
        ],
        out_specs=pl.BlockSpec((tm, d), lambda i: (i, 0)),
        out_shape=jax.ShapeDtypeStruct((nrows, d), F32),
        compiler_params=_params(("parallel",)),
        name="mm_res",
    )(*a_args, w.astype(BF16), *x_args, norm_g, mods)


def _mlp_kernel(x_ref, g_ref, sh_ref, sc_ref, gate_ref, w1_ref, w2_ref, o_ref, h_ref, *, tpb, nb):
    i = pl.program_id(0)
    k = pl.program_id(1)

    chunks = [slice(r, r + ROW_CHUNK) for r in range(0, x_ref.shape[0], ROW_CHUNK)]

    @pl.when(k == 0)
    def _():
        gmul = g_ref[2:3, :] * (1.0 + _mod_row(sc_ref, i, tpb, nb))
        sh = _mod_row(sh_ref, i, tpb, nb)
        for rows in chunks:
            h_ref[rows, :] = (_rms(x_ref[rows, :], gmul) + sh).astype(BF16)
        o_ref[...] = jnp.zeros_like(o_ref)

    u = jnp.dot(h_ref[...], w1_ref[...].astype(BF16), preferred_element_type=F32)
    u = jnp.square(jnp.maximum(u, 0.0)).astype(BF16)
    cn = MLP_OUT_CHUNK
    for c in range(o_ref.shape[1] // cn):
        cols = slice(c * cn, (c + 1) * cn)
        o_ref[:, cols] += jnp.dot(u, w2_ref[:, cols].astype(BF16), preferred_element_type=F32)

    @pl.when(k == pl.num_programs(1) - 1)
    def _():
        gout = g_ref[3:4, :] * _mod_row(gate_ref, i, tpb, nb)
        for rows in chunks:
            o_ref[rows, :] = x_ref[rows, :] + _rms(o_ref[rows, :], gout)


def _mlp(x, w1, w2, norm_g, mods, layer, nrows, seq, nb):
    d, hid = w1.shape[1], w1.shape[2]
    tm = MLP_TM
    th = MLP_TH
    tpb = seq // tm
    mod_spec = lambda col: pl.BlockSpec((None, MOD_ROWS, d), lambda i, k: (layer, 0, col))
    return pl.pallas_call(
        functools.partial(_mlp_kernel, tpb=tpb, nb=nb),
        grid=(nrows // tm, hid // th),
        in_specs=[
            pl.BlockSpec((tm, d), lambda i, k: (i, 0)),
            pl.BlockSpec((None, 4, d), lambda i, k: (layer, 0, 0)),
            mod_spec(3), mod_spec(4), mod_spec(5),
            pl.BlockSpec((None, d, th), lambda i, k: (layer, 0, k)),
            pl.BlockSpec((None, th, d), lambda i, k: (layer, k, 0)),
        ],
        out_specs=pl.BlockSpec((tm, d), lambda i, k: (i, 0)),
        out_shape=jax.ShapeDtypeStruct((nrows, d), F32),
        scratch_shapes=[pltpu.VMEM((tm, d), BF16)],
        compiler_params=_params(("parallel", "arbitrary")),
        name="mlp",
    )(x, norm_g, mods, mods, mods, w1, w2)


def _dft_tables(n, rows=None, col0=0, col_step=1, w=32):
    rows = n if rows is None else rows
    ncol = n // col_step
    assert ncol % w == 0
    k = jnp.arange(rows, dtype=jnp.int32)[:, None]
    unit = 2.0 * math.pi / n
    la = col_step * w * jnp.arange(ncol // w, dtype=jnp.int32)[None, :]
    lb = col0 + col_step * jnp.arange(w, dtype=jnp.int32)[None, :]
    ang_a = ((k * la) % n).astype(F32) * unit
    ang_b = ((k * lb) % n).astype(F32) * unit
    ca, sa = jnp.cos(ang_a)[:, :, None], jnp.sin(ang_a)[:, :, None]
    cb, sb = jnp.cos(ang_b)[:, None, :], jnp.sin(ang_b)[:, None, :]
    return (ca * cb - sa * sb).reshape(rows, ncol), (sa * cb + ca * sb).reshape(rows, ncol)


def _posdft_kernel(ce_ref, se_ref, co_ref, so_ref, ae_ref, be_ref, ao_ref, bo_ref, o_ref):
    even = jnp.dot(ce_ref[...], ae_ref[...], preferred_element_type=F32)
    even += jnp.dot(se_ref[...], be_ref[...], preferred_element_type=F32)
    odd = jnp.dot(co_ref[...], ao_ref[...], preferred_element_type=F32)
    odd += jnp.dot(so_ref[...], bo_ref[...], preferred_element_type=F32)
    o_ref[0] = (even + odd).astype(o_ref.dtype)
    o_ref[1] = (even - odd).astype(o_ref.dtype)


def _posdft(parts, nbatch, seq, seq0):
    d = parts[0].shape[1]
    half = seq // 2
    tm = min(DFT_TM, half)
    tn = DFT_TN
    tables = [t.astype(BF16) for par in (0, 1) for t in _dft_tables(seq, half, par, 2)]
    nj = d // tn
    tab_spec = pl.BlockSpec((tm, half), lambda b, j, m: (m, 0))
    src = pl.BlockSpec((half, tn), lambda b, j, m: (seq0 + b, j))
    y = pl.pallas_call(
        _posdft_kernel,
        grid=(nbatch, nj, half // tm),
        in_specs=[tab_spec] * 4 + [src] * 4,
        out_specs=pl.BlockSpec((None, 2, tm, tn), lambda b, j, m: (b, 0, m, j)),
        out_shape=jax.ShapeDtypeStruct((nbatch, 2, half, d), BF16),
        compiler_params=_params(("parallel", "parallel", "arbitrary")),
        name="posdft",
    )(*tables, *parts)
    return y.reshape(nbatch * seq, d)


def _split3_dot(mask_bf16, v):
    hi = v.astype(BF16)
    r1 = v - hi.astype(F32)
    mid = r1.astype(BF16)
    lo = (r1 - mid.astype(F32)).astype(BF16)
    out = jnp.dot(mask_bf16, hi, preferred_element_type=F32)
    out += jnp.dot(mask_bf16, mid, preferred_element_type=F32)
    out += jnp.dot(mask_bf16, lo, preferred_element_type=F32)
    return out


def _dot_nt(a, b):
    return lax.dot_general(a, b, (((1,), (1,)), ((), ())), preferred_element_type=F32)


def _dot_tn(a, b):
    return lax.dot_general(a, b, (((0,), (0,)), ((), ())), preferred_element_type=F32)


_GLA_HALF = 128


def _block_sum_masks(block, fwd):
    ri = lax.broadcasted_iota(jnp.int32, (_GLA_HALF, _GLA_HALF), 0)
    ci = lax.broadcasted_iota(jnp.int32, (_GLA_HALF, _GLA_HALF), 1)
    same = (ri // block) == (ci // block)
    ahead = (ri - ci) if fwd else (ci - ri)
    m_incl = jnp.where(same & (ahead >= 0), 1.0, 0.0).astype(BF16)
    m_excl = jnp.where(same & (ahead < 0), 1.0, 0.0).astype(BF16)
    return m_incl, m_excl


def _block_sums(mask, x):
    return jnp.concatenate([_split3_dot(mask, x[r0:r0 + _GLA_HALF])
                            for r0 in range(0, x.shape[0], _GLA_HALF)], axis=0)


def _gla_fast(fwd, q, kk, bc, v_ref, st_ref, o_ref):
    cs = GLA_CHUNK
    nchunk = q.shape[0] // cs
    nhead = st_ref.shape[0]
    ri = lax.broadcasted_iota(jnp.int32, (cs, cs), 0)
    ci = lax.broadcasted_iota(jnp.int32, (cs, cs), 1)
    keep = (ci <= ri) if fwd else (ci >= ri)
    heads = [slice(h * HGRN_DK, (h + 1) * HGRN_DK) for h in range(nhead)]
    order = [j if fwd else nchunk - 1 - j for j in range(nchunk)]
    chunks = []
    for c in order:
        rows = slice(c * cs, (c + 1) * cs)
        b = bc[rows]
        tot = b[cs - 1:cs] if fwd else b[0:1]
        mid = 0.5 * tot
        qm = (q[rows] * jnp.exp(b - mid)).astype(BF16)
        km = (kk[rows] * jnp.exp(mid - b)).astype(BF16)
        chunks.append((rows, qm, km, jnp.exp(mid), jnp.exp(tot), jnp.exp(tot - mid)))
    scores = [[_dot_nt(qm[:, hc], km[:, hc]) for hc in heads] for _, qm, km, *_ in chunks]
    updates = [[_dot_tn(v_ref[rows, hc], km[:, hc]) for hc in heads] for rows, _, km, *_ in chunks]
    sts = [st_ref[h] for h in range(nhead)]
    for (rows, qm, km, carry_in, carry_out, new_out), sc, upd in zip(chunks, scores, updates):
        for h, hc in enumerate(heads):
            a = jnp.where(keep, sc[h], 0.0).astype(BF16)
            o = jnp.dot(a, v_ref[rows, hc], preferred_element_type=F32)
            o += _dot_nt(qm[:, hc], (sts[h] * carry_in[:, hc]).astype(BF16))
            o_ref[rows, hc] = o
            sts[h] = sts[h] * carry_out[:, hc] + upd[h] * new_out[:, hc]
    for h in range(nhead):
        st_ref[h] = sts[h]


def _gla_exact(fwd, q, kk, logf, v_ref, st_ref, o_ref):
    seg_rows = q.shape[0]
    nsub = seg_rows // GLA_SUB
    hs = GLA_SUB // 2
    v = v_ref[...].astype(F32)

    m_incl, m_excl = _block_sum_masks(GLA_SUB, fwd)
    bc_all = _block_sums(m_incl, logf)
    qd_all = (q * jnp.exp(bc_all)).astype(BF16)
    kd_all = (kk * jnp.exp(_block_sums(m_excl, logf))).astype(BF16)

    rowh = lax.broadcasted_iota(jnp.int32, (hs, HGRN_DK), 0)
    for h in range(st_ref.shape[0]):
        hc = slice(h * HGRN_DK, (h + 1) * HGRN_DK)
        st = st_ref[h]
        for j in range(nsub):
            i = j if fwd else nsub - 1 - j
            r0 = i * GLA_SUB
            o = _dot_nt(qd_all[r0:r0 + GLA_SUB, hc], st.astype(BF16))
            acc = [o[:hs], o[hs:]]
            for s in range(GLA_SUB):
                sr = r0 + s
                b_s, k_s, v_s = bc_all[sr:sr + 1, hc], kk[sr:sr + 1, hc], v[sr:sr + 1, hc]
                for hh in range(2):
                    lo_row = hh * hs
                    reach_all = (lo_row > s) if fwd else (lo_row + hs - 1 < s)
                    reach_none = (lo_row + hs - 1 < s) if fwd else (lo_row > s)
                    if reach_none:
                        continue
                    rr = slice(r0 + lo_row, r0 + lo_row + hs)
                    diff = bc_all[rr, hc] - b_s
                    if not reach_all:
                        keep = (rowh + lo_row >= s) if fwd else (rowh + lo_row <= s)
                        diff = jnp.where(keep, diff, MASKED_LOG_DECAY)
                    col = jnp.sum(q[rr, hc] * jnp.exp(diff) * k_s, axis=1, keepdims=True)
                    acc[hh] = acc[hh] + col * v_s
            o_ref[r0:r0 + hs, hc] = acc[0]
            o_ref[r0 + hs:r0 + GLA_SUB, hc] = acc[1]
            last = r0 + GLA_SUB - 1 if fwd else r0
            upd = _dot_tn(v_ref[r0:r0 + GLA_SUB, hc], kd_all[r0:r0 + GLA_SUB, hc])
            st = st * jnp.exp(bc_all[last:last + 1, hc]) + upd
        st_ref[h] = st


def _gla_segment(fwd, q_ref, z_ref, v_ref, lb, st_ref, o_ref):
    half_gap = 0.5 * (1.0 - lb)
    f = (lb + half_gap) + half_gap * jnp.tanh(0.5 * z_ref[...])
    logf = jnp.log(f)
    kk = 1.0 - f
    q = q_ref[...].astype(F32)

    bc = _block_sums(_block_sum_masks(GLA_CHUNK, fwd)[0], logf)
    edge = GLA_CHUNK - 1 if fwd else 0
    tot = bc[edge:edge + 1]
    for c in range(1, q.shape[0] // GLA_CHUNK):
        tot = jnp.minimum(tot, bc[c * GLA_CHUNK + edge:c * GLA_CHUNK + edge + 1])
    mild = jnp.max(-tot) <= GLA_FAST_MAX_DECAY

    @pl.when(mild)
    def _():
        _gla_fast(fwd, q, kk, bc, v_ref, st_ref, o_ref)

    @pl.when(jnp.logical_not(mild))
    def _():
        _gla_exact(fwd, q, kk, logf, v_ref, st_ref, o_ref)


def _gla_kernel(q_ref, z_ref, v_ref, g_ref, lbp_ref, gn_ref, y_ref, of_ref, st_ref, o_ref,
                *, layer, nseg):
    t = pl.program_id(2)
    fwd = t < nseg
    seg = t % nseg
    seg_rows = q_ref.shape[0]

    @pl.when(seg == 0)
    def _():
        st_ref[...] = jnp.zeros_like(st_ref)

    p = lbp_ref[...]
    e = jnp.exp(p - jnp.max(p, axis=0, keepdims=True))
    sm = e / jnp.sum(e, axis=0, keepdims=True)
    lb2 = jnp.zeros_like(sm[0])
    for j in range(1, layer + 1):
        lb2 = lb2 + sm[j]

    @pl.when(fwd)
    def _():
        _gla_segment(True, q_ref, z_ref, v_ref, lb2[0:1, :], st_ref, o_ref)
        of_ref[pl.ds(pl.multiple_of(seg * seg_rows, seg_rows), seg_rows), :] = o_ref[...]

    @pl.when(jnp.logical_not(fwd))
    def _():
        _gla_segment(False, q_ref, z_ref, v_ref, lb2[1:2, :], st_ref, o_ref)
        fseg = jnp.where(seg == 0, 0, nseg - seg)
        o = o_ref[...] + of_ref[pl.ds(pl.multiple_of(fseg * seg_rows, seg_rows), seg_rows), :]
        gate = _silu(g_ref[...].astype(F32))
        for h in range(st_ref.shape[0]):
            hc = slice(h * HGRN_DK, (h + 1) * HGRN_DK)
            y_ref[:, hc] = (_rms(o[:, hc], gn_ref[...]) * gate[:, hc]).astype(y_ref.dtype)


def _gla(pq, pz, pvg, hgrn_lb, gnorm, layer, nb, seq, ctx_len):
    t_rows, f = pq.shape
    heads = f // HGRN_DK
    sr = GLA_SEG
    assert ctx_len == sr and seq % sr == 0
    nlat = seq // sr
    nseg = nlat + 1
    lat_blocks = nb * nlat
    depth = hgrn_lb.shape[0]

    def rb(b, t):
        seg = t % nseg
        lat = b * nlat + jnp.where(t < nseg, seg - 1, nlat - seg)
        return jnp.where(seg == 0, lat_blocks + b, lat)

    hp = GLA_HEADS_PER_STEP
    assert heads % hp == 0
    hgroups = heads // hp
    wide = hp * HGRN_DK
    blk = lambda colf: pl.BlockSpec((sr, wide), lambda b, h, t: (rb(b, t), colf(h, t)))
    return pl.pallas_call(
        functools.partial(_gla_kernel, layer=layer, nseg=nseg),
        grid=(nb, hgroups, 2 * nseg),
        in_specs=[
            blk(lambda h, t: h),
            blk(lambda h, t: jnp.where(t < nseg, 0, hgroups) + h),
            blk(lambda h, t: h),
            blk(lambda h, t: hgroups + h),
            pl.BlockSpec((depth, 2, wide), lambda b, h, t: (0, 0, h)),
            pl.BlockSpec((1, HGRN_DK), lambda b, h, t: (0, 0)),
        ],
        out_specs=pl.BlockSpec((sr, wide), lambda b, h, t: (rb(b, jnp.maximum(t, nseg)), h)),
        out_shape=jax.ShapeDtypeStruct((t_rows, f), BF16),
        scratch_shapes=[
            pltpu.VMEM((nseg * sr, wide), F32),
            pltpu.VMEM((hp, HGRN_DK, HGRN_DK), F32),
            pltpu.VMEM((sr, wide), F32),
        ],
        compiler_params=_params(("parallel", "parallel", "arbitrary")),
        name="gla",
    )(pq, pz, pvg, pvg, hgrn_lb, gnorm.reshape(1, HGRN_DK))


def _attn_kernel(q_ref, kl_ref, kc_ref, vl_ref, vc_ref, lam_ref, sub_ref, o_ref, *, lambda_init, nq):
    qi = pl.program_id(2)
    hd = DIFF_HEAD_DIM
    nhead = o_ref.shape[1] // (2 * hd)
    lp = lam_ref[...]
    lam = (jnp.exp(jnp.sum(lp[0:1, :] * lp[1:2, :], keepdims=True))
           - jnp.exp(jnp.sum(lp[2:3, :] * lp[3:4, :], keepdims=True)) + lambda_init)

    def probs(sc, sl):
        m = jnp.max(sc, axis=-1, keepdims=True)
        if sl is not None:
            m = jnp.maximum(m, jnp.max(sl, axis=-1, keepdims=True))
        pc = jnp.exp2(sc - m)
        den = jnp.sum(pc, axis=-1, keepdims=True)
        pl_ = None
        if sl is not None:
            pl_ = jnp.exp2(sl - m)
            den = den + jnp.sum(pl_, axis=-1, keepdims=True)
            pl_ = pl_.astype(BF16)
        return pc.astype(BF16), pl_, 1.0 / den

    def attend(with_latent_keys):
        cols = [slice(c * hd, (c + 1) * hd) for c in range(2 * nhead)]
        sc = [_dot_nt(q_ref[:, c], kc_ref[:, c]) for c in cols]
        sl = [_dot_nt(q_ref[:, c], kl_ref[:, c]) if with_latent_keys else None for c in cols]
        for h in range(nhead):
            vcols = slice(2 * h * hd, 2 * (h + 1) * hd)
            outs = []
            for mp in range(2):
                pc, pl_, inv = probs(sc[2 * h + mp], sl[2 * h + mp])
                o = jnp.dot(pc, vc_ref[:, vcols], preferred_element_type=F32)
                if pl_ is not None:
                    o += jnp.dot(pl_, vl_ref[:, vcols], preferred_element_type=F32)
                outs.append(o * (inv if mp == 0 else lam * inv))
            y = _rms(outs[0] - outs[1], sub_ref[...]) * (1.0 - lambda_init)
            o_ref[:, vcols] = y.astype(o_ref.dtype)

    @pl.when(qi < nq)
    def _():
        attend(True)

    @pl.when(qi >= nq)
    def _():
        attend(False)


def _attn(qk, v, lam_p, subln, lambda_init, nb, seq, ctx_len):
    t_rows, d = v.shape
    wide = ATTN_HEADS_PER_STEP * 2 * DIFF_HEAD_DIM
    assert d % wide == 0
    hgroups = d // wide
    tq = ATTN_TQ
    assert seq % tq == 0 and ctx_len % tq == 0
    nq = seq // tq
    nqc = ctx_len // tq
    lat_qblocks = nb * nq
    lat_kblocks = nb * seq // ctx_len
    qrow = lambda b, qi: jnp.where(qi >= nq, lat_qblocks + b * nqc + (qi - nq), b * nq + qi)
    lat = lambda part: pl.BlockSpec((seq, wide), lambda b, h, qi: (b, part * hgroups + h))
    ctx = lambda part: pl.BlockSpec((ctx_len, wide), lambda b, h, qi: (lat_kblocks + b, part * hgroups + h))
    return pl.pallas_call(
        functools.partial(_attn_kernel, lambda_init=lambda_init, nq=nq),
        grid=(nb, hgroups, nq + nqc),
        in_specs=[
            pl.BlockSpec((tq, wide), lambda b, h, qi: (qrow(b, qi), h)),
            lat(1), ctx(1), lat(0), ctx(0),
            pl.BlockSpec((4, DIFF_HEAD_DIM), lambda b, h, qi: (0, 0)),
            pl.BlockSpec((1, 2 * DIFF_HEAD_DIM), lambda b, h, qi: (0, 0)),
        ],
        out_specs=pl.BlockSpec((tq, wide), lambda b, h, qi: (qrow(b, qi), h)),
        out_shape=jax.ShapeDtypeStruct((t_rows, d), BF16),
        compiler_params=_params(("parallel", "parallel", "arbitrary")),
        name="diff_attn",
    )(qk, qk, qk, v, v, lam_p, subln.reshape(1, 2 * DIFF_HEAD_DIM))


def _rope_tables(seq, nb, ctx_rows):
    rows = seq // GRID_W
    row = jnp.repeat(jnp.arange(rows, dtype=F32), GRID_W)
    col = jnp.tile(jnp.arange(GRID_W, dtype=F32), rows)
    n_freq = DIFF_HEAD_DIM // 4
    inv = ROPE_BASE ** (-jnp.arange(n_freq, dtype=F32) / n_freq)
    ang = jnp.concatenate([row[:, None] * inv, col[:, None] * inv], axis=-1)
    cos, sin = jnp.cos(ang), jnp.sin(ang)
    cos2 = jnp.concatenate([cos, cos], axis=-1)
    sin2 = jnp.concatenate([-sin, sin], axis=-1)
    cos2 = jnp.concatenate([jnp.tile(cos2, (nb, 1)), jnp.ones((ctx_rows, DIFF_HEAD_DIM), F32)], axis=0)
    sin2 = jnp.concatenate([jnp.tile(sin2, (nb, 1)), jnp.zeros((ctx_rows, DIFF_HEAD_DIM), F32)], axis=0)
    return cos2, sin2


def kernel(x, c, ctx, c_ctx, w_mod, b_mod, norm_g, w_mlp_in, w_mlp_out, fnet_w_out, hgrn_w_in,
           hgrn_lb, hgrn_gnorm, hgrn_w_out, diff_w_qkv, diff_lambda, diff_subln, diff_w_out):
    nb, seq, d = x.shape
    ctx_len = ctx.shape[1]
    depth = w_mod.shape[0]
    lat_rows, ctx_rows = nb * seq, nb * ctx_len
    assert nb + 1 <= MOD_ROWS

    xs = (x.reshape(lat_rows, d), ctx.reshape(ctx_rows, d))
    cc = jnp.concatenate([c, c_ctx[None, :], jnp.zeros((MOD_ROWS - nb - 1, d), F32)], axis=0)
    mods = _mods(cc, w_mod, b_mod)

    for i in range(depth):
        mixer, slot = i % N_MIXERS, i // N_MIXERS
        ctx_out = i < depth - 1
        ctx_in = ctx_out or mixer != 0
        assert ctx_out or mixer == 0
        nrows = lat_rows + ctx_rows if ctx_in else lat_rows
        common = dict(nrows=nrows, seq=seq, nb=nb)
        if mixer == 0:
            gd = d // FNET_GROUPS
            cg, sg = _dft_tables(gd)
            wcs = jnp.concatenate([cg, -sg], axis=1).astype(BF16)
            parts = _norm_mod(xs, norm_g, mods, i, 0, 0, wcs=wcs, **common)
            y = _posdft(parts, nb, seq, 0)
            if ctx_in:
                y = (y, _posdft(parts, nb, ctx_len, lat_rows // ctx_len))
            w_out = fnet_w_out[slot]
        elif mixer == 1:
            h = _norm_mod(xs, norm_g, mods, i, 0, 0, **common)
            w_in = hgrn_w_in[slot]
            f = hgrn_lb.shape[-1]
            pq = _mm(h, w_in, 0, f, BF16, name="hgrn_q")
            pz = _mm(h, w_in, f, 2 * f, F32, name="hgrn_z")
            pvg = _mm(h, w_in, 3 * f, 2 * d, BF16, name="hgrn_vg")
            y = _gla(pq, pz, pvg, hgrn_lb, hgrn_gnorm[slot], i, nb, seq, ctx_len)
            w_out = hgrn_w_out[slot]
        else:
            h = _norm_mod(xs, norm_g, mods, i, 0, 0, **common)
            cos2, sin2 = _rope_tables(seq, nb, ctx_rows)
            rope = (cos2, sin2, d, DIFF_HEAD_DIM ** -0.5 * math.log2(math.e))
            qk = _mm(h, diff_w_qkv[slot], 0, 2 * d, BF16, rope=rope, name="diff_qk")
            v = _mm(h, diff_w_qkv[slot], 2 * d, d, BF16, name="diff_v")
            lambda_init = 0.8 - 0.6 * math.exp(-0.3 * i)
            y = _attn(qk, v, diff_lambda[slot], diff_subln[slot], lambda_init, nb, seq, ctx_len)
            w_out = diff_w_out[slot]
        xs = _mm_res(y, w_out, xs, norm_g, mods, i, 1, 2, **common)
        xs = _mlp(xs, w_mlp_in, w_mlp_out, norm_g, mods, i, **common)
    return xs[:lat_rows].reshape(nb, seq, d)
```
